```python
import math
import jax, jax.numpy as jnp
from jax import lax
import numpy as np

D_MODEL = 2048
BATCH = 16
SEQ = 256
DEPTH = 4
DEC_BATCH = 2
DEC_SEQ = 4096
PAST_LEN = 256

GRID_W = 64
BLOCK = 128
A_HEADS = 16
A_HEAD_DIM = 64
A_WIDTH = A_HEADS * A_HEAD_DIM
W_LORA = 64
ICL_LORA = 64
G_LORA = 128
A_COLS = 3 * A_WIDTH + G_LORA + 2 * W_LORA + 2 * ICL_LORA
A_GN_EPS = 64e-5
B_HEADS = 4
B_KV_HEADS = 2
B_HEAD_DIM = 128
B_WIDTH = B_HEADS * B_HEAD_DIM
B_KV_WIDTH = B_KV_HEADS * B_HEAD_DIM
WINDOW = 128
C_HEADS = 4
C_HEAD_DIM = 64
C_WIDTH = C_HEADS * 2 * C_HEAD_DIM
IN_COLS = A_COLS + B_WIDTH + 2 * B_KV_WIDTH + 3 * C_WIDTH + 3 * D_MODEL
PEER_HEADS = 8
N_KEYS = 128
N_EXPERTS = N_KEYS * N_KEYS
PEER_QDIM = 256
PEER_TOPK = 16
ROPE_BASE = 10000.0
LN_EPS = 1e-5
NEG_INF = -1e30
DEEPNORM_ALPHA = (2 * DEPTH) ** 0.25
DEEPNORM_BETA = (8 * DEPTH) ** -0.25

kernel_name = 'hybrid_rwkv7_swa_diffattn_peer_dit_step'


def split_cols(z, sizes):
    return jnp.split(z, [int(s) for s in np.cumsum(sizes)[:-1]], axis=-1)


def layer_norm(x, g=None, b=None):
    xf = x.astype(jnp.float32)
    mu = jnp.mean(xf, axis=-1, keepdims=True)
    var = jnp.mean(jnp.square(xf - mu), axis=-1, keepdims=True)
    y = (xf - mu) * lax.rsqrt(var + LN_EPS)
    if g is not None:
        y = y * g + b
    return y.astype(x.dtype)


def centred_conv3(x, w):
    xp = jnp.pad(x, ((0, 0), (1, 1), (0, 0)))
    return xp[:, :-2] * w[0] + xp[:, 1:-1] * w[1] + xp[:, 2:] * w[2]


def axial_rope_tables(n_tokens, head_dim):
    rows = n_tokens // GRID_W
    row_pos = jnp.repeat(jnp.arange(rows, dtype=jnp.float32), GRID_W)
    col_pos = jnp.tile(jnp.arange(GRID_W, dtype=jnp.float32), rows)
    n_freq = head_dim // 4
    freqs = ROPE_BASE ** (-jnp.arange(n_freq, dtype=jnp.float32) / n_freq)
    ang = jnp.concatenate([row_pos[:, None] * freqs, col_pos[:, None] * freqs], axis=-1)
    return jnp.cos(ang), jnp.sin(ang)


def apply_rope(x, cos, sin):
    xf = x.astype(jnp.float32)
    x1, x2 = xf[..., 0::2], xf[..., 1::2]
    c, s = cos[None, :, None, :], sin[None, :, None, :]
    out = jnp.stack([x1 * c - x2 * s, x1 * s + x2 * c], axis=-1).reshape(x.shape)
    return out.astype(x.dtype)


def rwkv7_scan(S0, r, w, k, v, a, b, reverse):
    def step(S, inp):
        r_t, w_t, k_t, v_t, a_t, b_t = inp
        sa = jnp.einsum('bhvk,bhk->bhv', S, a_t)
        S = S * w_t[:, :, None, :] + sa[..., None] * b_t[:, :, None, :] + v_t[..., None] * k_t[:, :, None, :]
        return S, jnp.einsum('bhvk,bhk->bhv', S, r_t)
    xs = tuple(jnp.moveaxis(t.astype(jnp.float32), 1, 0) for t in (r, w, k, v, a, b))
    S_fin, ys = lax.scan(step, S0.astype(jnp.float32), xs, reverse=reverse)
    return S_fin, jnp.moveaxis(ys, 0, 1)


def rwkv7_mixer(za, S0_f, S0_b, lp):
    Bsz, T, _ = za.shape
    r, k, v, g_d, wd_f, wd_b, ad_f, ad_b = split_cols(
        za, [A_WIDTH, A_WIDTH, A_WIDTH, G_LORA, W_LORA, W_LORA, ICL_LORA, ICL_LORA])
    heads = lambda t: t.astype(jnp.float32).reshape(Bsz, T, A_HEADS, A_HEAD_DIM)
    rh, vh = heads(r), heads(v)
    kk = heads(k * lp['a_k_k'])
    kk = kk * lax.rsqrt(jnp.sum(kk * kk, axis=-1, keepdims=True) + 1e-12)
    ys, states, kds = [], [], []
    for d, (wd, ad, S0) in enumerate(((wd_f, ad_f, S0_f), (wd_b, ad_b, S0_b))):
        wz = (lp['a_w0'][d] + jnp.tanh(wd) @ lp['a_w_up'][d]).astype(jnp.float32)
        decay = jnp.exp(-math.exp(-0.5) * jax.nn.sigmoid(wz))
        icl = jax.nn.sigmoid((lp['a_a0'][d] + ad @ lp['a_a_up'][d]).astype(jnp.float32))
        kd = heads(k.astype(jnp.float32) * (1.0 + (icl - 1.0) * lp['a_k_a']))
        S_fin, y = rwkv7_scan(S0, rh, heads(decay), kd, vh, -kk, kk * heads(icl), reverse=(d == 1))
        ys.append(y)
        states.append(S_fin)
        kds.append(kd)
    y = ys[0] + ys[1]
    mu = jnp.mean(y, axis=-1, keepdims=True)
    var = jnp.mean(jnp.square(y - mu), axis=-1, keepdims=True)
    y = ((y - mu) * lax.rsqrt(var + A_GN_EPS)).reshape(Bsz, T, A_WIDTH) * lp['a_gn_g'] + lp['a_gn_b']
    bonus = jnp.sum(rh * (kds[0] + kds[1]) * lp['a_r_k'], axis=-1, keepdims=True) * vh
    g = jax.nn.sigmoid(g_d) @ lp['a_g_up']
    out = (y + bonus.reshape(Bsz, T, A_WIDTH)) * g
    return out.astype(za.dtype), states[0], states[1]


def gqa_context_attn(q, k, v, sink):
    Bsz, T = q.shape[:2]
    groups = B_HEADS // B_KV_HEADS
    qb = jnp.moveaxis(q.reshape(Bsz, T // BLOCK, BLOCK, B_KV_HEADS, groups, B_HEAD_DIM), 1, 0)
    sink_col = sink.astype(jnp.float32).reshape(1, B_KV_HEADS, groups, 1, 1)

    def one(qx):
        s = jnp.einsum('bqkgd,blkd->bkgql', qx, k).astype(jnp.float32) * (B_HEAD_DIM ** -0.5)
        sc = jnp.broadcast_to(sink_col, s.shape[:-1] + (1,))
        p = jax.nn.softmax(jnp.concatenate([s, sc], axis=-1), axis=-1)[..., :-1]
        return jnp.einsum('bkgql,blkd->bqkgd', p.astype(v.dtype), v)

    o = lax.map(one, qb)
    return jnp.moveaxis(o, 0, 1).reshape(Bsz, T, B_WIDTH)


def gqa_window_attn(q, k, v, kc, vc, sink):
    Bsz, T = q.shape[:2]
    nb = T // BLOCK
    groups = B_HEADS // B_KV_HEADS
    qb = q.reshape(Bsz, nb, BLOCK, B_KV_HEADS, groups, B_HEAD_DIM)
    pad = ((0, 0), (BLOCK, BLOCK), (0, 0), (0, 0))
    idx = jnp.arange(nb)[:, None] * BLOCK + jnp.arange(3 * BLOCK)[None, :]
    kw = jnp.pad(k, pad)[:, idx]
    vw = jnp.pad(v, pad)[:, idx]
    qpos = jnp.arange(nb)[:, None] * BLOCK + jnp.arange(BLOCK)[None, :]
    kpos = idx - BLOCK
    valid = ((jnp.abs(qpos[:, :, None] - kpos[:, None, :]) <= WINDOW)
             & (kpos[:, None, :] >= 0) & (kpos[:, None, :] < T))
    scale = B_HEAD_DIM ** -0.5
    s_w = jnp.einsum('bnqkgd,bnjkd->bnkgqj', qb, kw).astype(jnp.float32) * scale
    s_w = jnp.where(valid[None, :, None, None], s_w, NEG_INF)
    s_c = jnp.einsum('bnqkgd,blkd->bnkgql', qb, kc).astype(jnp.float32) * scale
    sc = jnp.broadcast_to(sink.astype(jnp.float32).reshape(1, 1, B_KV_HEADS, groups, 1, 1), s_w.shape[:-1] + (1,))
    p = jax.nn.softmax(jnp.concatenate([s_w, s_c, sc], axis=-1), axis=-1)
    nw = 3 * BLOCK
    o = (jnp.einsum('bnkgqj,bnjkd->bnqkgd', p[..., :nw].astype(v.dtype), vw)
         + jnp.einsum('bnkgql,blkd->bnqkgd', p[..., nw:-1].astype(vc.dtype), vc))
    return o.reshape(Bsz, T, B_WIDTH)


def diff_attention(q, k, v, lam):
    Bsz, T = q.shape[:2]
    d = C_HEAD_DIM
    qb = jnp.moveaxis(q.reshape(Bsz, T // BLOCK, BLOCK, C_HEADS, 2 * d), 1, 0)
    k1, k2 = k[..., :d], k[..., d:]
    scale = d ** -0.5

    def one(qx):
        s1 = jnp.einsum('bqhd,bkhd->bhqk', qx[..., :d], k1).astype(jnp.float32) * scale
        s2 = jnp.einsum('bqhd,bkhd->bhqk', qx[..., d:], k2).astype(jnp.float32) * scale
        p = jax.nn.softmax(s1, axis=-1) - lam * jax.nn.softmax(s2, axis=-1)
        return jnp.einsum('bhqk,bkhe->bqhe', p.astype(v.dtype), v)

    o = lax.map(one, qb)
    return jnp.moveaxis(o, 0, 1).reshape(Bsz, T, C_HEADS, 2 * d)


def diff_lambda(lam_vecs, layer):
    lam_init = 0.8 - 0.6 * math.exp(-0.3 * layer)
    lv = lam_vecs.astype(jnp.float32)
    lam = jnp.exp(jnp.sum(lv[0] * lv[1])) - jnp.exp(jnp.sum(lv[2] * lv[3])) + lam_init
    return lam, lam_init


def diff_output(o, subln_g, lam_init):
    Bsz, T = o.shape[:2]
    of = o.astype(jnp.float32)
    of = of * lax.rsqrt(jnp.mean(of * of, axis=-1, keepdims=True) + 1e-6) * subln_g * (1.0 - lam_init)
    return of.reshape(Bsz, T, C_WIDTH).astype(o.dtype)


def peer_ffn(h, wq, sub_keys, u, v):
    Bsz, T, D = h.shape
    half = PEER_QDIM // 2
    chunks = h.reshape(Bsz * T // BLOCK, BLOCK, D)

    def one(hc):
        q = (hc @ wq).reshape(BLOCK, PEER_HEADS, PEER_QDIM)
        s1 = jnp.einsum('rhd,hnd->rhn', q[..., :half], sub_keys[0]).astype(jnp.float32)
        s2 = jnp.einsum('rhd,hnd->rhn', q[..., half:], sub_keys[1]).astype(jnp.float32)
        t1, i1 = lax.top_k(s1, PEER_TOPK)
        t2, i2 = lax.top_k(s2, PEER_TOPK)
        cand = (t1[..., :, None] + t2[..., None, :]).reshape(BLOCK, PEER_HEADS, PEER_TOPK * PEER_TOPK)
        cand_idx = (i1[..., :, None] * N_KEYS + i2[..., None, :]).reshape(BLOCK, PEER_HEADS, PEER_TOPK * PEER_TOPK)
        top_s, j = lax.top_k(cand, PEER_TOPK)
        experts = jnp.take_along_axis(cand_idx, j, axis=-1)
        gate = jax.nn.softmax(top_s, axis=-1)
        ue = u[experts]
        ve = v[experts]
        act = jax.nn.gelu(jnp.einsum('rhkd,rd->rhk', ue, hc).astype(jnp.float32))
        return jnp.einsum('rhk,rhkd->rd', (gate * act).astype(ve.dtype), ve)

    return lax.map(one, chunks).reshape(Bsz, T, D)


def adaln_params(cond, lp):
    mod = jax.nn.silu(cond) @ lp['w_mod'] + lp['b_mod']
    return jnp.split(mod[:, None, :], 6, axis=-1)


def mixer_inputs(x, shift, scale, lp):
    h = layer_norm(x) * (1.0 + scale) + shift
    z = h @ lp['w_in']
    zA, zB, zC, zG = split_cols(z, [A_COLS, B_WIDTH + 2 * B_KV_WIDTH, 3 * C_WIDTH, 3 * D_MODEL])
    return centred_conv3(zA, lp['a_conv']), zB, zC, zG


def finish_layer(x, yA, yB, yC, zG, gate1, shift2, scale2, gate2, lp):
    gA, gB, gC = jnp.split(jax.nn.sigmoid(zG), 3, axis=-1)
    merged = gA * (yA @ lp['p_a']) + gB * (yB @ lp['p_b']) + gC * (yC @ lp['p_c'])
    x = layer_norm(DEEPNORM_ALPHA * x + gate1 * (merged @ lp['w_out']), lp['ln1_g'], lp['ln1_b'])
    h = layer_norm(x) * (1.0 + scale2) + shift2
    y = peer_ffn(h, lp['peer_wq'], lp['peer_keys'], lp['peer_u'], lp['peer_v'])
    return layer_norm(DEEPNORM_ALPHA * x + gate2 * y, lp['ln2_g'], lp['ln2_b'])


def context_layer(x, cond, lp, layer):
    Bsz, L, _ = x.shape
    shift1, scale1, gate1, shift2, scale2, gate2 = adaln_params(cond, lp)
    zA, zB, zC, zG = mixer_inputs(x, shift1, scale1, lp)
    S0 = jnp.zeros((Bsz, A_HEADS, A_HEAD_DIM, A_HEAD_DIM), jnp.float32)
    yA, S_f, S_b = rwkv7_mixer(zA, S0, S0, lp)
    qB, kB, vB = split_cols(zB, [B_WIDTH, B_KV_WIDTH, B_KV_WIDTH])
    qB = qB.reshape(Bsz, L, B_HEADS, B_HEAD_DIM)
    kB = kB.reshape(Bsz, L, B_KV_HEADS, B_HEAD_DIM)
    vB = vB.reshape(Bsz, L, B_KV_HEADS, B_HEAD_DIM)
    yB = gqa_context_attn(qB, kB, vB, lp['b_sink'])
    qC, kC, vC = [t.reshape(Bsz, L, C_HEADS, 2 * C_HEAD_DIM) for t in split_cols(zC, [C_WIDTH, C_WIDTH, C_WIDTH])]
    lam, lam_init = diff_lambda(lp['c_lam'], layer)
    yC = diff_output(diff_attention(qC, kC, vC, lam), lp['c_subln'], lam_init)
    x_new = finish_layer(x, yA, yB, yC, zG, gate1, shift2, scale2, gate2, lp)
    ctx = (S_f.astype(x.dtype), S_b.astype(x.dtype),
           jnp.transpose(kB, (0, 2, 1, 3)), jnp.transpose(vB, (0, 2, 1, 3)),
           jnp.transpose(kC, (0, 2, 1, 3)), jnp.transpose(vC, (0, 2, 1, 3)))
    return x_new, ctx


def latent_layer(x, cond, lp, layer, S_f0, S_b0, kB_c, vB_c, kC_c, vC_c, rope_b, rope_c):
    Bsz, T, _ = x.shape
    shift1, scale1, gate1, shift2, scale2, gate2 = adaln_params(cond, lp)
    zA, zB, zC, zG = mixer_inputs(x, shift1, scale1, lp)
    yA, _, _ = rwkv7_mixer(zA, S_f0, S_b0, lp)
    qB, kB, vB = split_cols(zB, [B_WIDTH, B_KV_WIDTH, B_KV_WIDTH])
    qB = apply_rope(qB.reshape(Bsz, T, B_HEADS, B_HEAD_DIM), *rope_b)
    kB = apply_rope(kB.reshape(Bsz, T, B_KV_HEADS, B_HEAD_DIM), *rope_b)
    vB = vB.reshape(Bsz, T, B_KV_HEADS, B_HEAD_DIM)
    yB = gqa_window_attn(qB, kB, vB, jnp.transpose(kB_c, (0, 2, 1, 3)), jnp.transpose(vB_c, (0, 2, 1, 3)), lp['b_sink'])
    qC, kC, vC = split_cols(zC, [C_WIDTH, C_WIDTH, C_WIDTH])
    qC = apply_rope(qC.reshape(Bsz, T, 2 * C_HEADS, C_HEAD_DIM), *rope_c).reshape(Bsz, T, C_HEADS, 2 * C_HEAD_DIM)
    kC = apply_rope(kC.reshape(Bsz, T, 2 * C_HEADS, C_HEAD_DIM), *rope_c).reshape(Bsz, T, C_HEADS, 2 * C_HEAD_DIM)
    vC = vC.reshape(Bsz, T, C_HEADS, 2 * C_HEAD_DIM)
    k_all = jnp.concatenate([kC, jnp.transpose(kC_c, (0, 2, 1, 3)).astype(kC.dtype)], axis=1)
    v_all = jnp.concatenate([vC, jnp.transpose(vC_c, (0, 2, 1, 3)).astype(vC.dtype)], axis=1)
    lam, lam_init = diff_lambda(lp['c_lam'], layer)
    yC = diff_output(diff_attention(qC, k_all, v_all, lam), lp['c_subln'], lam_init)
    return finish_layer(x, yA, yB, yC, zG, gate1, shift2, scale2, gate2, lp)


def setup_inputs(seed: int = 0) -> dict:
    key = jax.random.key(seed)
    ks = iter(jax.random.split(key, 48))

    def nrm(shape, scale=1.0):
        return jax.random.normal(next(ks), shape, jnp.float32) * scale

    D = D_MODEL
    inv = D ** -0.5
    beta = DEEPNORM_BETA
    return {
        'x_prompt': nrm((BATCH, SEQ, D)),
        'x_sample': nrm((DEC_BATCH, DEC_SEQ, D)),
        'state_a_fwd': nrm((DEC_BATCH, DEPTH, A_HEADS, A_HEAD_DIM, A_HEAD_DIM), 0.2),
        'state_a_bwd': nrm((DEC_BATCH, DEPTH, A_HEADS, A_HEAD_DIM, A_HEAD_DIM), 0.2),
        'cache_b_k': nrm((DEC_BATCH, DEPTH, B_KV_HEADS, PAST_LEN, B_HEAD_DIM)),
        'cache_b_v': nrm((DEC_BATCH, DEPTH, B_KV_HEADS, PAST_LEN, B_HEAD_DIM)),
        'cache_c_k': nrm((DEC_BATCH, DEPTH, C_HEADS, PAST_LEN, 2 * C_HEAD_DIM)),
        'cache_c_v': nrm((DEC_BATCH, DEPTH, C_HEADS, PAST_LEN, 2 * C_HEAD_DIM)),
        'c': nrm((DEC_BATCH, D)),
        'c_ctx': nrm((D,)),
        'w_mod': nrm((DEPTH, D, 6 * D), 0.5 * inv),
        'b_mod': nrm((DEPTH, 6 * D), 0.02),
        'w_in': nrm((DEPTH, D, IN_COLS), inv),
        'a_conv': nrm((DEPTH, 3, A_COLS), 0.05) + jnp.array([0.2, 0.6, 0.2], jnp.float32)[None, :, None],
        'a_w0': nrm((DEPTH, 2, A_WIDTH), 0.5) - 0.5,
        'a_w_up': nrm((DEPTH, 2, W_LORA, A_WIDTH), 0.5 * W_LORA ** -0.5),
        'a_a0': nrm((DEPTH, 2, A_WIDTH), 0.5),
        'a_a_up': nrm((DEPTH, 2, ICL_LORA, A_WIDTH), 0.5 * ICL_LORA ** -0.5),
        'a_g_up': nrm((DEPTH, G_LORA, A_WIDTH), G_LORA ** -0.5),
        'a_k_k': nrm((DEPTH, A_WIDTH), 0.05) + 0.85,
        'a_k_a': nrm((DEPTH, A_WIDTH), 0.05) + 1.0,
        'a_r_k': nrm((DEPTH, A_HEADS, A_HEAD_DIM), 0.1),
        'a_gn_g': nrm((DEPTH, A_WIDTH), 0.02) + 1.0,
        'a_gn_b': nrm((DEPTH, A_WIDTH), 0.02),
        'b_sink': nrm((DEPTH, B_HEADS), 0.5),
        'c_lam': nrm((DEPTH, 4, C_HEAD_DIM), 0.1),
        'c_subln': nrm((DEPTH, 2 * C_HEAD_DIM), 0.02) + 1.0,
        'p_a': nrm((DEPTH, A_WIDTH, D), beta * A_WIDTH ** -0.5),
        'p_b': nrm((DEPTH, B_WIDTH, D), beta * B_WIDTH ** -0.5),
        'p_c': nrm((DEPTH, C_WIDTH, D), beta * C_WIDTH ** -0.5),
        'w_out': nrm((DEPTH, D, D), beta * inv),
        'ln1_g': nrm((DEPTH, D), 0.02) + 1.0,
        'ln1_b': nrm((DEPTH, D), 0.02),
        'ln2_g': nrm((DEPTH, D), 0.02) + 1.0,
        'ln2_b': nrm((DEPTH, D), 0.02),
        'peer_wq': nrm((DEPTH, D, PEER_HEADS * PEER_QDIM), inv),
        'peer_keys': nrm((DEPTH, 2, PEER_HEADS, N_KEYS, PEER_QDIM // 2), (PEER_QDIM // 2) ** -0.5),
        'peer_u': nrm((DEPTH, N_EXPERTS, D), inv),
        'peer_v': nrm((DEPTH, N_EXPERTS, D), beta * PEER_HEADS ** -0.5),
    }


def reference(x_prompt, x_sample, state_a_fwd, state_a_bwd, cache_b_k, cache_b_v, cache_c_k, cache_c_v,
              c, c_ctx, w_mod, b_mod, w_in, a_conv, a_w0, a_w_up, a_a0, a_a_up, a_g_up, a_k_k, a_k_a, a_r_k,
              a_gn_g, a_gn_b, b_sink, c_lam, c_subln, p_a, p_b, p_c, w_out, ln1_g, ln1_b, ln2_g, ln2_b,
              peer_wq, peer_keys, peer_u, peer_v):
    n_lat = x_sample.shape[1]
    rope_b = axial_rope_tables(n_lat, B_HEAD_DIM)
    rope_c = axial_rope_tables(n_lat, C_HEAD_DIM)
    cond_ctx = jnp.broadcast_to(c_ctx[None, :], (x_prompt.shape[0], D_MODEL))
    y_prompt, y_sample = x_prompt, x_sample
    new = [[], [], [], [], [], []]
    for layer in range(DEPTH):
        lp = {
            'w_mod': w_mod[layer], 'b_mod': b_mod[layer], 'w_in': w_in[layer], 'a_conv': a_conv[layer],
            'a_w0': a_w0[layer], 'a_w_up': a_w_up[layer], 'a_a0': a_a0[layer], 'a_a_up': a_a_up[layer],
            'a_g_up': a_g_up[layer], 'a_k_k': a_k_k[layer], 'a_k_a': a_k_a[layer], 'a_r_k': a_r_k[layer],
            'a_gn_g': a_gn_g[layer], 'a_gn_b': a_gn_b[layer], 'b_sink': b_sink[layer], 'c_lam': c_lam[layer],
            'c_subln': c_subln[layer], 'p_a': p_a[layer], 'p_b': p_b[layer], 'p_c': p_c[layer],
            'w_out': w_out[layer], 'ln1_g': ln1_g[layer], 'ln1_b': ln1_b[layer], 'ln2_g': ln2_g[layer],
            'ln2_b': ln2_b[layer], 'peer_wq': peer_wq[layer], 'peer_keys': peer_keys[layer],
            'peer_u': peer_u[layer], 'peer_v': peer_v[layer],
        }
        y_prompt, ctx = context_layer(y_prompt, cond_ctx, lp, layer)
        for lst, t in zip(new, ctx):
            lst.append(t)
        y_sample = latent_layer(y_sample, c, lp, layer,
                                state_a_fwd[:, layer], state_a_bwd[:, layer],
                                cache_b_k[:, layer], cache_b_v[:, layer],
                                cache_c_k[:, layer], cache_c_v[:, layer], rope_b, rope_c)
    new_state_a_fwd = jnp.stack(new[0], axis=1)
    new_state_a_bwd = jnp.stack(new[1], axis=1)
    new_cache_b_k = jnp.stack(new[2], axis=1)
    new_cache_b_v = jnp.stack(new[3], axis=1)
    new_cache_c_k = jnp.stack(new[4], axis=1)
    new_cache_c_v = jnp.stack(new[5], axis=1)
    return (y_prompt, y_sample, new_state_a_fwd, new_state_a_bwd, new_cache_b_k, new_cache_b_v, new_cache_c_k, new_cache_c_v)
```

```python
import math
from functools import partial

import jax
import jax.numpy as jnp
import numpy as np
from jax import lax
from jax.experimental import pallas as pl
from jax.experimental.pallas import tpu as pltpu

D_MODEL = 2048
DEPTH = 4
GRID_W = 64
BLOCK = 128
A_HEADS = 16
A_HEAD_DIM = 64
A_WIDTH = A_HEADS * A_HEAD_DIM
W_LORA = 64
ICL_LORA = 64
G_LORA = 128
A_COLS = 3 * A_WIDTH + G_LORA + 2 * W_LORA + 2 * ICL_LORA
A_GN_EPS = 64e-5
B_HEADS = 4
B_KV_HEADS = 2
B_HEAD_DIM = 128
B_WIDTH = B_HEADS * B_HEAD_DIM
B_KV_WIDTH = B_KV_HEADS * B_HEAD_DIM
WINDOW = 128
C_HEADS = 4
C_HEAD_DIM = 64
C_WIDTH = C_HEADS * 2 * C_HEAD_DIM
PEER_HEADS = 8
N_KEYS = 128
PEER_QDIM = 256
PEER_TOPK = 16
ROPE_BASE = 10000.0
LN_EPS = 1e-5
NEG_INF = -1e30
DEEPNORM_ALPHA = (2 * DEPTH) ** 0.25

LANE = 128
VMEM_LIMIT_BYTES = 56 * 1024 * 1024


def _mm_kernel(x_ref, w_ref, o_ref):
    o_ref[...] = jnp.dot(x_ref[...].astype(jnp.bfloat16), w_ref[...],
                         preferred_element_type=jnp.float32)


def _pick_tile(n, cap, unit):
    if n <= cap:
        return n
    best = None
    for t in range(unit, cap + 1, unit):
        if n % t == 0:
            best = t
    assert best is not None, (n, cap, unit)
    return best


def mm(x, w):
    M, K = x.shape
    N = w.shape[1]
    tm = _pick_tile(M, 512, 8)
    tn = _pick_tile(N, 1024, LANE)
    return pl.pallas_call(
        _mm_kernel,
        grid=(N // tn, M // tm),
        in_specs=[pl.BlockSpec((tm, K), lambda j, i: (i, 0)),
                  pl.BlockSpec((K, tn), lambda j, i: (0, j))],
        out_specs=pl.BlockSpec((tm, tn), lambda j, i: (i, j)),
        out_shape=jax.ShapeDtypeStruct((M, N), jnp.float32),
        compiler_params=pltpu.CompilerParams(
            dimension_semantics=("arbitrary", "arbitrary"),
            vmem_limit_bytes=VMEM_LIMIT_BYTES),
        name="mm",
    )(x, w)


def mm3(x, w):
    B, T, K = x.shape
    return mm(x.reshape(B * T, K), w).reshape(B, T, w.shape[1])


def split_cols(z, sizes):
    return jnp.split(z, [int(s) for s in np.cumsum(sizes)[:-1]], axis=-1)


def layer_norm(x, g=None, b=None):
    mu = jnp.mean(x, axis=-1, keepdims=True)
    var = jnp.mean(jnp.square(x - mu), axis=-1, keepdims=True)
    y = (x - mu) * lax.rsqrt(var + LN_EPS)
    if g is not None:
        y = y * g + b
    return y


def centred_conv3(x, w):
    xp = jnp.pad(x, ((0, 0), (1, 1), (0, 0)))
    return xp[:, :-2] * w[0] + xp[:, 1:-1] * w[1] + xp[:, 2:] * w[2]


def axial_rope_tables(n_tokens, head_dim):
    rows = n_tokens // GRID_W
    row_pos = jnp.repeat(jnp.arange(rows, dtype=jnp.float32), GRID_W)
    col_pos = jnp.tile(jnp.arange(GRID_W, dtype=jnp.float32), rows)
    n_freq = head_dim // 4
    freqs = ROPE_BASE ** (-jnp.arange(n_freq, dtype=jnp.float32) / n_freq)
    ang = jnp.concatenate([row_pos[:, None] * freqs, col_pos[:, None] * freqs], axis=-1)
    return jnp.cos(ang), jnp.sin(ang)


def apply_rope(x, cos, sin):
    x1, x2 = x[..., 0::2], x[..., 1::2]
    c, s = cos[None, :, None, :], sin[None, :, None, :]
    return jnp.stack([x1 * c - x2 * s, x1 * s + x2 * c], axis=-1).reshape(x.shape)


def rwkv7_scan(S0, r, w, k, v, a, b, reverse):
    def step(S, inp):
        r_t, w_t, k_t, v_t, a_t, b_t = inp
        sa = jnp.einsum('bhvk,bhk->bhv', S, a_t)
        S = S * w_t[:, :, None, :] + sa[..., None] * b_t[:, :, None, :] + v_t[..., None] * k_t[:, :, None, :]
        return S, jnp.einsum('bhvk,bhk->bhv', S, r_t)
    xs = tuple(jnp.moveaxis(t, 1, 0) for t in (r, w, k, v, a, b))
    S_fin, ys = lax.scan(step, S0, xs, reverse=reverse)
    return S_fin, jnp.moveaxis(ys, 0, 1)


def rwkv7_mixer(za, S0_f, S0_b, lp):
    Bsz, T, _ = za.shape
    r, k, v, g_d, wd_f, wd_b, ad_f, ad_b = split_cols(
        za, [A_WIDTH, A_WIDTH, A_WIDTH, G_LORA, W_LORA, W_LORA, ICL_LORA, ICL_LORA])
    heads = lambda t: t.reshape(Bsz, T, A_HEADS, A_HEAD_DIM)
    rh, vh = heads(r), heads(v)
    kk = heads(k * lp['a_k_k'])
    kk = kk * lax.rsqrt(jnp.sum(kk * kk, axis=-1, keepdims=True) + 1e-12)
    ys, states, kds = [], [], []
    for d, (wd, ad, S0) in enumerate(((wd_f, ad_f, S0_f), (wd_b, ad_b, S0_b))):
        wz = lp['a_w0'][d] + mm3(jnp.tanh(wd), lp['a_w_up'][d])
        decay = jnp.exp(-math.exp(-0.5) * jax.nn.sigmoid(wz))
        icl = jax.nn.sigmoid(lp['a_a0'][d] + mm3(ad, lp['a_a_up'][d]))
        kd = heads(k * (1.0 + (icl - 1.0) * lp['a_k_a']))
        S_fin, y = rwkv7_scan(S0, rh, heads(decay), kd, vh, -kk, kk * heads(icl), reverse=(d == 1))
        ys.append(y)
        states.append(S_fin)
        kds.append(kd)
    y = ys[0] + ys[1]
    mu = jnp.mean(y, axis=-1, keepdims=True)
    var = jnp.mean(jnp.square(y - mu), axis=-1, keepdims=True)
    y = ((y - mu) * lax.rsqrt(var + A_GN_EPS)).reshape(Bsz, T, A_WIDTH) * lp['a_gn_g'] + lp['a_gn_b']
    bonus = jnp.sum(rh * (kds[0] + kds[1]) * lp['a_r_k'], axis=-1, keepdims=True) * vh
    g = mm3(jax.nn.sigmoid(g_d), lp['a_g_up'])
    out = (y + bonus.reshape(Bsz, T, A_WIDTH)) * g
    return out, states[0], states[1]


def gqa_context_attn(q, k, v, sink):
    Bsz, T = q.shape[:2]
    groups = B_HEADS // B_KV_HEADS
    qb = jnp.moveaxis(q.reshape(Bsz, T // BLOCK, BLOCK, B_KV_HEADS, groups, B_HEAD_DIM), 1, 0)
    sink_col = sink.reshape(1, B_KV_HEADS, groups, 1, 1)

    def one(qx):
        s = jnp.einsum('bqkgd,blkd->bkgql', qx, k) * (B_HEAD_DIM ** -0.5)
        sc = jnp.broadcast_to(sink_col, s.shape[:-1] + (1,))
        p = jax.nn.softmax(jnp.concatenate([s, sc], axis=-1), axis=-1)[..., :-1]
        return jnp.einsum('bkgql,blkd->bqkgd', p, v)

    o = lax.map(one, qb)
    return jnp.moveaxis(o, 0, 1).reshape(Bsz, T, B_WIDTH)


def gqa_window_attn(q, k, v, kc, vc, sink):
    Bsz, T = q.shape[:2]
    nb = T // BLOCK
    groups = B_HEADS // B_KV_HEADS
    qb = q.reshape(Bsz, nb, BLOCK, B_KV_HEADS, groups, B_HEAD_DIM)
    pad = ((0, 0), (BLOCK, BLOCK), (0, 0), (0, 0))
    idx = jnp.arange(nb)[:, None] * BLOCK + jnp.arange(3 * BLOCK)[None, :]
    kw = jnp.pad(k, pad)[:, idx]
    vw = jnp.pad(v, pad)[:, idx]
    qpos = jnp.arange(nb)[:, None] * BLOCK + jnp.arange(BLOCK)[None, :]
    kpos = idx - BLOCK
    valid = ((jnp.abs(qpos[:, :, None] - kpos[:, None, :]) <= WINDOW)
             & (kpos[:, None, :] >= 0) & (kpos[:, None, :] < T))
    scale = B_HEAD_DIM ** -0.5
    s_w = jnp.einsum('bnqkgd,bnjkd->bnkgqj', qb, kw) * scale
    s_w = jnp.where(valid[None, :, None, None], s_w, NEG_INF)
    s_c = jnp.einsum('bnqkgd,blkd->bnkgql', qb, kc) * scale
    sc = jnp.broadcast_to(sink.reshape(1, 1, B_KV_HEADS, groups, 1, 1), s_w.shape[:-1] + (1,))
    p = jax.nn.softmax(jnp.concatenate([s_w, s_c, sc], axis=-1), axis=-1)
    nw = 3 * BLOCK
    o = (jnp.einsum('bnkgqj,bnjkd->bnqkgd', p[..., :nw], vw)
         + jnp.einsum('bnkgql,blkd->bnqkgd', p[..., nw:-1], vc))
    return o.reshape(Bsz, T, B_WIDTH)


def diff_attention(q, k, v, lam):
    Bsz, T = q.shape[:2]
    d = C_HEAD_DIM
    qb = jnp.moveaxis(q.reshape(Bsz, T // BLOCK, BLOCK, C_HEADS, 2 * d), 1, 0)
    k1, k2 = k[..., :d], k[..., d:]
    scale = d ** -0.5

    def one(qx):
        s1 = jnp.einsum('bqhd,bkhd->bhqk', qx[..., :d], k1) * scale
        s2 = jnp.einsum('bqhd,bkhd->bhqk', qx[..., d:], k2) * scale
        p = jax.nn.softmax(s1, axis=-1) - lam * jax.nn.softmax(s2, axis=-1)
        return jnp.einsum('bhqk,bkhe->bqhe', p, v)

    o = lax.map(one, qb)
    return jnp.moveaxis(o, 0, 1).reshape(Bsz, T, C_HEADS, 2 * d)


def diff_lambda(lv, layer):
    lam_init = 0.8 - 0.6 * math.exp(-0.3 * layer)
    lam = jnp.exp(jnp.sum(lv[0] * lv[1])) - jnp.exp(jnp.sum(lv[2] * lv[3])) + lam_init
    return lam, lam_init


def diff_output(o, subln_g, lam_init):
    Bsz, T = o.shape[:2]
    of = o * lax.rsqrt(jnp.mean(o * o, axis=-1, keepdims=True) + 1e-6) * subln_g * (1.0 - lam_init)
    return of.reshape(Bsz, T, C_WIDTH)


def peer_ffn(h, wq, sub_keys, u, v):
    Bsz, T, D = h.shape
    half = PEER_QDIM // 2
    qall = mm(h.reshape(Bsz * T, D), wq)
    chunks = (h.reshape(Bsz * T // BLOCK, BLOCK, D), qall.reshape(Bsz * T // BLOCK, BLOCK, -1))

    def one(args):
        hc, q = args
        q = q.reshape(BLOCK, PEER_HEADS, PEER_QDIM)
        s1 = jnp.einsum('rhd,hnd->rhn', q[..., :half], sub_keys[0])
        s2 = jnp.einsum('rhd,hnd->rhn', q[..., half:], sub_keys[1])
        t1, i1 = lax.top_k(s1, PEER_TOPK)
        t2, i2 = lax.top_k(s2, PEER_TOPK)
        cand = (t1[..., :, None] + t2[..., None, :]).reshape(BLOCK, PEER_HEADS, PEER_TOPK * PEER_TOPK)
        cand_idx = (i1[..., :, None] * N_KEYS + i2[..., None, :]).reshape(BLOCK, PEER_HEADS, PEER_TOPK * PEER_TOPK)
        top_s, j = lax.top_k(cand, PEER_TOPK)
        experts = jnp.take_along_axis(cand_idx, j, axis=-1)
        gate = jax.nn.softmax(top_s, axis=-1)
        ue = u[experts]
        ve = v[experts]
        act = jax.nn.gelu(jnp.einsum('rhkd,rd->rhk', ue, hc))
        return jnp.einsum('rhk,rhkd->rd', gate * act, ve)

    return lax.map(one, chunks).reshape(Bsz, T, D)


def adaln_params(cond, lp):
    mod = mm(jax.nn.silu(cond), lp['w_mod']) + lp['b_mod']
    return jnp.split(mod[:, None, :], 6, axis=-1)


def mixer_inputs(x, shift, scale, lp):
    h = layer_norm(x) * (1.0 + scale) + shift
    z = mm3(h, lp['w_in'])
    zA, zB, zC, zG = split_cols(z, [A_COLS, B_WIDTH + 2 * B_KV_WIDTH, 3 * C_WIDTH, 3 * D_MODEL])
    return centred_conv3(zA, lp['a_conv']), zB, zC, zG


def finish_layer(x, yA, yB, yC, zG, gate1, shift2, scale2, gate2, lp):
    gA, gB, gC = jnp.split(jax.nn.sigmoid(zG), 3, axis=-1)
    merged = gA * mm3(yA, lp['p_a']) + gB * mm3(yB, lp['p_b']) + gC * mm3(yC, lp['p_c'])
    x = layer_norm(DEEPNORM_ALPHA * x + gate1 * mm3(merged, lp['w_out']), lp['ln1_g'], lp['ln1_b'])
    h = layer_norm(x) * (1.0 + scale2) + shift2
    y = peer_ffn(h, lp['peer_wq'], lp['peer_keys'], lp['peer_u'], lp['peer_v'])
    return layer_norm(DEEPNORM_ALPHA * x + gate2 * y, lp['ln2_g'], lp['ln2_b'])


def context_layer(x, cond, lp, layer):
    Bsz, L, _ = x.shape
    shift1, scale1, gate1, shift2, scale2, gate2 = adaln_params(cond, lp)
    zA, zB, zC, zG = mixer_inputs(x, shift1, scale1, lp)
    S0 = jnp.zeros((Bsz, A_HEADS, A_HEAD_DIM, A_HEAD_DIM), jnp.float32)
    yA, S_f, S_b = rwkv7_mixer(zA, S0, S0, lp)
    qB, kB, vB = split_cols(zB, [B_WIDTH, B_KV_WIDTH, B_KV_WIDTH])
    qB = qB.reshape(Bsz, L, B_HEADS, B_HEAD_DIM)
    kB = kB.reshape(Bsz, L, B_KV_HEADS, B_HEAD_DIM)
    vB = vB.reshape(Bsz, L, B_KV_HEADS, B_HEAD_DIM)
    yB = gqa_context_attn(qB, kB, vB, lp['b_sink'])
    qC, kC, vC = [t.reshape(Bsz, L, C_HEADS, 2 * C_HEAD_DIM) for t in split_cols(zC, [C_WIDTH, C_WIDTH, C_WIDTH])]
    lam, lam_init = diff_lambda(lp['c_lam'], layer)
    yC = diff_output(diff_attention(qC, kC, vC, lam), lp['c_subln'], lam_init)
    x_new = finish_layer(x, yA, yB, yC, zG, gate1, shift2, scale2, gate2, lp)
    ctx = (S_f, S_b,
           jnp.transpose(kB, (0, 2, 1, 3)), jnp.transpose(vB, (0, 2, 1, 3)),
           jnp.transpose(kC, (0, 2, 1, 3)), jnp.transpose(vC, (0, 2, 1, 3)))
    return x_new, ctx


def latent_layer(x, cond, lp, layer, S_f0, S_b0, kB_c, vB_c, kC_c, vC_c, rope_b, rope_c):
    Bsz, T, _ = x.shape
    shift1, scale1, gate1, shift2, scale2, gate2 = adaln_params(cond, lp)
    zA, zB, zC, zG = mixer_inputs(x, shift1, scale1, lp)
    yA, _, _ = rwkv7_mixer(zA, S_f0, S_b0, lp)
    qB, kB, vB = split_cols(zB, [B_WIDTH, B_KV_WIDTH, B_KV_WIDTH])
    qB = apply_rope(qB.reshape(Bsz, T, B_HEADS, B_HEAD_DIM), *rope_b)
    kB = apply_rope(kB.reshape(Bsz, T, B_KV_HEADS, B_HEAD_DIM), *rope_b)
    vB = vB.reshape(Bsz, T, B_KV_HEADS, B_HEAD_DIM)
    yB = gqa_window_attn(qB, kB, vB, jnp.transpose(kB_c, (0, 2, 1, 3)), jnp.transpose(vB_c, (0, 2, 1, 3)), lp['b_sink'])
    qC, kC, vC = split_cols(zC, [C_WIDTH, C_WIDTH, C_WIDTH])
    qC = apply_rope(qC.reshape(Bsz, T, 2 * C_HEADS, C_HEAD_DIM), *rope_c).reshape(Bsz, T, C_HEADS, 2 * C_HEAD_DIM)
    kC = apply_rope(kC.reshape(Bsz, T, 2 * C_HEADS, C_HEAD_DIM), *rope_c).reshape(Bsz, T, C_HEADS, 2 * C_HEAD_DIM)
    vC = vC.reshape(Bsz, T, C_HEADS, 2 * C_HEAD_DIM)
    k_all = jnp.concatenate([kC, jnp.transpose(kC_c, (0, 2, 1, 3))], axis=1)
    v_all = jnp.concatenate([vC, jnp.transpose(vC_c, (0, 2, 1, 3))], axis=1)
    lam, lam_init = diff_lambda(lp['c_lam'], layer)
    yC = diff_output(diff_attention(qC, k_all, v_all, lam), lp['c_subln'], lam_init)
    return finish_layer(x, yA, yB, yC, zG, gate1, shift2, scale2, gate2, lp)


_BF16_WEIGHTS = ('w_mod', 'w_in', 'a_w_up', 'a_a_up', 'a_g_up', 'p_a', 'p_b', 'p_c', 'w_out', 'peer_wq')


def kernel(x_prompt, x_sample, state_a_fwd, state_a_bwd, cache_b_k, cache_b_v, cache_c_k, cache_c_v, c, c_ctx, w_mod, b_mod, w_in, a_conv, a_w0, a_w_up, a_a0, a_a_up, a_g_up, a_k_k, a_k_a, a_r_k, a_gn_g, a_gn_b, b_sink, c_lam, c_subln, p_a, p_b, p_c, w_out, ln1_g, ln1_b, ln2_g, ln2_b, peer_wq, peer_keys, peer_u, peer_v):
    params = dict(w_mod=w_mod, b_mod=b_mod, w_in=w_in, a_conv=a_conv, a_w0=a_w0, a_w_up=a_w_up, a_a0=a_a0,
                  a_a_up=a_a_up, a_g_up=a_g_up, a_k_k=a_k_k, a_k_a=a_k_a, a_r_k=a_r_k, a_gn_g=a_gn_g,
                  a_gn_b=a_gn_b, b_sink=b_sink, c_lam=c_lam, c_subln=c_subln, p_a=p_a, p_b=p_b, p_c=p_c,
                  w_out=w_out, ln1_g=ln1_g, ln1_b=ln1_b, ln2_g=ln2_g, ln2_b=ln2_b, peer_wq=peer_wq,
                  peer_keys=peer_keys, peer_u=peer_u, peer_v=peer_v)
    for name in _BF16_WEIGHTS:
        params[name] = params[name].astype(jnp.bfloat16)
    n_lat = x_sample.shape[1]
    rope_b = axial_rope_tables(n_lat, B_HEAD_DIM)
    rope_c = axial_rope_tables(n_lat, C_HEAD_DIM)
    cond_ctx = jnp.broadcast_to(c_ctx[None, :], (x_prompt.shape[0], D_MODEL))
    y_prompt, y_sample = x_prompt, x_sample
    new = [[], [], [], [], [], []]
    for layer in range(DEPTH):
        lp = {name: val[layer] for name, val in params.items()}
        y_prompt, ctx = context_layer(y_prompt, cond_ctx, lp, layer)
        for lst, t in zip(new, ctx):
            lst.append(t)
        y_sample = latent_layer(y_sample, c, lp, layer,
                                state_a_fwd[:, layer], state_a_bwd[:, layer],
                                cache_b_k[:, layer], cache_b_v[:, layer],
                                cache_c_k[:, layer], cache_c_v[:, layer], rope_b, rope_c)
    return (y_prompt, y_sample) + tuple(jnp.stack(lst, axis=1) for lst in new)
```

```python
import math
from functools import partial

import jax
import jax.numpy as jnp
import numpy as np
from jax import lax
from jax.experimental import pallas as pl
from jax.experimental.pallas import tpu as pltpu

D_MODEL = 2048
DEPTH = 4
GRID_W = 64
BLOCK = 128
A_HEADS = 16
A_HEAD_DIM = 64
A_WIDTH = A_HEADS * A_HEAD_DIM
W_LORA = 64
ICL_LORA = 64
G_LORA = 128
A_COLS = 3 * A_WIDTH + G_LORA + 2 * W_LORA + 2 * ICL_LORA
A_GN_EPS = 64e-5
B_HEADS = 4
B_KV_HEADS = 2
B_HEAD_DIM = 128
B_WIDTH = B_HEADS * B_HEAD_DIM
B_KV_WIDTH = B_KV_HEADS * B_HEAD_DIM
WINDOW = 128
C_HEADS = 4
C_HEAD_DIM = 64
C_WIDTH = C_HEADS * 2 * C_HEAD_DIM
PEER_HEADS = 8
N_KEYS = 128
PEER_QDIM = 256
PEER_TOPK = 16
ROPE_BASE = 10000.0
LN_EPS = 1e-5
NEG_INF = -1e30
DEEPNORM_ALPHA = (2 * DEPTH) ** 0.25

LANE = 128
VMEM_LIMIT_BYTES = 56 * 1024 * 1024


def _mm_kernel(x_ref, w_ref, o_ref):
    o_ref[...] = jnp.dot(x_ref[...].astype(jnp.bfloat16), w_ref[...],
                         preferred_element_type=jnp.float32)


def _pick_tile(n, cap, unit):
    if n <= cap:
        return n
    best = None
    for t in range(unit, cap + 1, unit):
        if n % t == 0:
            best = t
    assert best is not None, (n, cap, unit)
    return best


def mm(x, w):
    M, K = x.shape
    N = w.shape[1]
    tm = _pick_tile(M, 512, 8)
    tn = _pick_tile(N, 1024, LANE)
    return pl.pallas_call(
        _mm_kernel,
        grid=(N // tn, M // tm),
        in_specs=[pl.BlockSpec((tm, K), lambda j, i: (i, 0)),
                  pl.BlockSpec((K, tn), lambda j, i: (0, j))],
        out_specs=pl.BlockSpec((tm, tn), lambda j, i: (i, j)),
        out_shape=jax.ShapeDtypeStruct((M, N), jnp.float32),
        compiler_params=pltpu.CompilerParams(
            dimension_semantics=("arbitrary", "arbitrary"),
            vmem_limit_bytes=VMEM_LIMIT_BYTES),
        name="mm",
    )(x, w)


def mm3(x, w):
    B, T, K = x.shape
    return mm(x.reshape(B * T, K), w).reshape(B, T, w.shape[1])


RWKV_CHUNK = 64
RWKV_HEADS_PER_STEP = 8

_NN = (((2,), (1,)), ((0,), (0,)))
_NT = (((2,), (2,)), ((0,), (0,)))
_TN = (((1,), (1,)), ((0,), (0,)))


def _split2(x):
    hi = x.astype(jnp.bfloat16)
    lo = (x - hi.astype(jnp.float32)).astype(jnp.bfloat16)
    return hi, lo


def _dot3(a, b, dims=_NN):
    f = lambda x, y: lax.dot_general(x, y, dims, preferred_element_type=jnp.float32)
    return f(a[0], b[0]) + (f(a[0], b[1]) + f(a[1], b[0]))


def _rwkv_chunk_kernel(r_ref, v_ref, a_ref, lw_ref, k_ref, b_ref, s0_ref, y_ref, sfin_ref, st_ref):
    d = pl.program_id(0)
    c = pl.program_id(3)
    n_chunks = pl.num_programs(3)
    HB, C, _ = r_ref.shape

    @pl.when(c == 0)
    def _():
        st_ref[...] = s0_ref[...]

    ti = lax.broadcasted_iota(jnp.int32, (HB, C, C), 1)
    si = lax.broadcasted_iota(jnp.int32, (HB, C, C), 2)
    diff = (si - ti) * (1 - 2 * d)
    m_incl = diff <= 0
    m_strict = diff < 0
    m_incl_bf = jnp.where(m_incl, 1.0, 0.0).astype(jnp.bfloat16)
    eye = jnp.where(diff == 0, 1.0, 0.0).astype(jnp.float32)

    r, v, a = r_ref[...], v_ref[...], a_ref[...]
    lw, k, b = lw_ref[...], k_ref[...], b_ref[...]
    st = st_ref[...]

    lw_hi, lw_lo = _split2(lw)
    lw_lo2 = (lw - lw_hi.astype(jnp.float32) - lw_lo.astype(jnp.float32)).astype(jnp.bfloat16)
    f = lambda y: lax.dot_general(m_incl_bf, y, _NN, preferred_element_type=jnp.float32)
    cum = f(lw_hi) + (f(lw_lo) + f(lw_lo2))
    total = jnp.sum(lw, axis=1, keepdims=True)
    e_cum = jnp.exp(cum)
    e_inv = jnp.exp(-cum)
    e_tot = jnp.exp(total)
    at = _split2(a * jnp.exp(cum - lw))
    rt = _split2(r * e_cum)
    bt_f = b * e_inv
    kt_f = k * e_inv
    bt, kt = _split2(bt_f), _split2(kt_f)
    bh, kh = _split2(bt_f * e_tot), _split2(kt_f * e_tot)
    vs = _split2(v)

    l_ab = jnp.where(m_strict, _dot3(at, bt, _NT), 0.0)
    l_ak = jnp.where(m_strict, _dot3(at, kt, _NT), 0.0)
    m_rb = jnp.where(m_incl, _dot3(rt, bt, _NT), 0.0)
    m_rk = jnp.where(m_incl, _dot3(rt, kt, _NT), 0.0)

    tm = eye + l_ab
    lp = _split2(l_ab)
    for _ in range(int(math.log2(C)) - 1):
        lp = _split2(_dot3(lp, lp))
        tm = tm + _dot3(_split2(tm), lp)
    tms = _split2(tm)
    p = _split2(_dot3(tms, at))
    q = _dot3(tms, _split2(_dot3(_split2(l_ak), vs)))

    sts = _split2(st)
    u = _split2(_dot3(p, sts) + q)
    y_ref[...] = _dot3(rt, sts) + _dot3(_split2(m_rb), u) + _dot3(_split2(m_rk), vs)
    st_ref[...] = jnp.swapaxes(e_tot, 1, 2) * st + _dot3(bh, u, _TN) + _dot3(kh, vs, _TN)

    @pl.when(c == n_chunks - 1)
    def _():
        sfin_ref[...] = st_ref[...]


def rwkv7_chunked(r, v, a, lw, k, b, s0t):
    D, B, H, T, N = lw.shape
    C, HB = RWKV_CHUNK, RWKV_HEADS_PER_STEP
    assert T % C == 0 and H % HB == 0
    n_chunks = T // C
    chunk = lambda d, c: c + d * (n_chunks - 1 - 2 * c)
    shared = pl.BlockSpec((None, HB, C, N), lambda d, bb, g, c: (bb, g, chunk(d, c), 0))
    per_dir = pl.BlockSpec((None, None, HB, C, N), lambda d, bb, g, c: (d, bb, g, chunk(d, c), 0))
    state = pl.BlockSpec((None, None, HB, N, N), lambda d, bb, g, c: (d, bb, g, 0, 0))
    return pl.pallas_call(
        _rwkv_chunk_kernel,
        grid=(D, B, H // HB, n_chunks),
        in_specs=[shared, shared, shared, per_dir, per_dir, per_dir, state],
        out_specs=[per_dir, state],
        out_shape=[jax.ShapeDtypeStruct((D, B, H, T, N), jnp.float32),
                   jax.ShapeDtypeStruct((D, B, H, N, N), jnp.float32)],
        scratch_shapes=[pltpu.VMEM((HB, N, N), jnp.float32)],
        compiler_params=pltpu.CompilerParams(
            dimension_semantics=("arbitrary", "arbitrary", "arbitrary", "arbitrary"),
            vmem_limit_bytes=VMEM_LIMIT_BYTES),
        name="rwkv7_chunked",
    )(r, v, a, lw, k, b, s0t)


def split_cols(z, sizes):
    return jnp.split(z, [int(s) for s in np.cumsum(sizes)[:-1]], axis=-1)


def layer_norm(x, g=None, b=None):
    mu = jnp.mean(x, axis=-1, keepdims=True)
    var = jnp.mean(jnp.square(x - mu), axis=-1, keepdims=True)
    y = (x - mu) * lax.rsqrt(var + LN_EPS)
    if g is not None:
        y = y * g + b
    return y


def centred_conv3(x, w):
    xp = jnp.pad(x, ((0, 0), (1, 1), (0, 0)))
    return xp[:, :-2] * w[0] + xp[:, 1:-1] * w[1] + xp[:, 2:] * w[2]


def axial_rope_tables(n_tokens, head_dim):
    rows = n_tokens // GRID_W
    row_pos = jnp.repeat(jnp.arange(rows, dtype=jnp.float32), GRID_W)
    col_pos = jnp.tile(jnp.arange(GRID_W, dtype=jnp.float32), rows)
    n_freq = head_dim // 4
    freqs = ROPE_BASE ** (-jnp.arange(n_freq, dtype=jnp.float32) / n_freq)
    ang = jnp.concatenate([row_pos[:, None] * freqs, col_pos[:, None] * freqs], axis=-1)
    return jnp.cos(ang), jnp.sin(ang)


def apply_rope(x, cos, sin):
    x1, x2 = x[..., 0::2], x[..., 1::2]
    c, s = cos[None, :, None, :], sin[None, :, None, :]
    return jnp.stack([x1 * c - x2 * s, x1 * s + x2 * c], axis=-1).reshape(x.shape)


def rwkv7_mixer(za, S0_f, S0_b, lp):
    Bsz, T, _ = za.shape
    r, k, v, g_d, wd_f, wd_b, ad_f, ad_b = split_cols(
        za, [A_WIDTH, A_WIDTH, A_WIDTH, G_LORA, W_LORA, W_LORA, ICL_LORA, ICL_LORA])
    heads = lambda t: t.reshape(Bsz, T, A_HEADS, A_HEAD_DIM)
    by_head = lambda t: jnp.transpose(t, (0, 2, 1, 3))
    rh, vh = heads(r), heads(v)
    kk = heads(k * lp['a_k_k'])
    kk = kk * lax.rsqrt(jnp.sum(kk * kk, axis=-1, keepdims=True) + 1e-12)
    lws, kds, bs = [], [], []
    for d, (wd, ad) in enumerate(((wd_f, ad_f), (wd_b, ad_b))):
        wz = lp['a_w0'][d] + mm3(jnp.tanh(wd), lp['a_w_up'][d])
        lws.append(heads(-math.exp(-0.5) * jax.nn.sigmoid(wz)))
        icl = jax.nn.sigmoid(lp['a_a0'][d] + mm3(ad, lp['a_a_up'][d]))
        kds.append(heads(k * (1.0 + (icl - 1.0) * lp['a_k_a'])))
        bs.append(kk * heads(icl))
    stack = lambda ts: jnp.stack([by_head(t) for t in ts], axis=0)
    s0t = jnp.swapaxes(jnp.stack([S0_f, S0_b], axis=0), -1, -2)
    ys, sfin_t = rwkv7_chunked(by_head(rh), by_head(vh), by_head(-kk), stack(lws), stack(kds), stack(bs), s0t)
    states = jnp.swapaxes(sfin_t, -1, -2)
    y = jnp.transpose(ys[0] + ys[1], (0, 2, 1, 3))
    mu = jnp.mean(y, axis=-1, keepdims=True)
    var = jnp.mean(jnp.square(y - mu), axis=-1, keepdims=True)
    y = ((y - mu) * lax.rsqrt(var + A_GN_EPS)).reshape(Bsz, T, A_WIDTH) * lp['a_gn_g'] + lp['a_gn_b']
    bonus = jnp.sum(rh * (kds[0] + kds[1]) * lp['a_r_k'], axis=-1, keepdims=True) * vh
    g = mm3(jax.nn.sigmoid(g_d), lp['a_g_up'])
    out = (y + bonus.reshape(Bsz, T, A_WIDTH)) * g
    return out, states[0], states[1]


def gqa_context_attn(q, k, v, sink):
    Bsz, T = q.shape[:2]
    groups = B_HEADS // B_KV_HEADS
    qb = jnp.moveaxis(q.reshape(Bsz, T // BLOCK, BLOCK, B_KV_HEADS, groups, B_HEAD_DIM), 1, 0)
    sink_col = sink.reshape(1, B_KV_HEADS, groups, 1, 1)

    def one(qx):
        s = jnp.einsum('bqkgd,blkd->bkgql', qx, k) * (B_HEAD_DIM ** -0.5)
        sc = jnp.broadcast_to(sink_col, s.shape[:-1] + (1,))
        p = jax.nn.softmax(jnp.concatenate([s, sc], axis=-1), axis=-1)[..., :-1]
        return jnp.einsum('bkgql,blkd->bqkgd', p, v)

    o = lax.map(one, qb)
    return jnp.moveaxis(o, 0, 1).reshape(Bsz, T, B_WIDTH)


def gqa_window_attn(q, k, v, kc, vc, sink):
    Bsz, T = q.shape[:2]
    nb = T // BLOCK
    groups = B_HEADS // B_KV_HEADS
    qb = q.reshape(Bsz, nb, BLOCK, B_KV_HEADS, groups, B_HEAD_DIM)
    pad = ((0, 0), (BLOCK, BLOCK), (0, 0), (0, 0))
    idx = jnp.arange(nb)[:, None] * BLOCK + jnp.arange(3 * BLOCK)[None, :]
    kw = jnp.pad(k, pad)[:, idx]
    vw = jnp.pad(v, pad)[:, idx]
    qpos = jnp.arange(nb)[:, None] * BLOCK + jnp.arange(BLOCK)[None, :]
    kpos = idx - BLOCK
    valid = ((jnp.abs(qpos[:, :, None] - kpos[:, None, :]) <= WINDOW)
             & (kpos[:, None, :] >= 0) & (kpos[:, None, :] < T))
    scale = B_HEAD_DIM ** -0.5
    s_w = jnp.einsum('bnqkgd,bnjkd->bnkgqj', qb, kw) * scale
    s_w = jnp.where(valid[None, :, None, None], s_w, NEG_INF)
    s_c = jnp.einsum('bnqkgd,blkd->bnkgql', qb, kc) * scale
    sc = jnp.broadcast_to(sink.reshape(1, 1, B_KV_HEADS, groups, 1, 1), s_w.shape[:-1] + (1,))
    p = jax.nn.softmax(jnp.concatenate([s_w, s_c, sc], axis=-1), axis=-1)
    nw = 3 * BLOCK
    o = (jnp.einsum('bnkgqj,bnjkd->bnqkgd', p[..., :nw], vw)
         + jnp.einsum('bnkgql,blkd->bnqkgd', p[..., nw:-1], vc))
    return o.reshape(Bsz, T, B_WIDTH)


def diff_attention(q, k, v, lam):
    Bsz, T = q.shape[:2]
    d = C_HEAD_DIM
    qb = jnp.moveaxis(q.reshape(Bsz, T // BLOCK, BLOCK, C_HEADS, 2 * d), 1, 0)
    k1, k2 = k[..., :d], k[..., d:]
    scale = d ** -0.5

    def one(qx):
        s1 = jnp.einsum('bqhd,bkhd->bhqk', qx[..., :d], k1) * scale
        s2 = jnp.einsum('bqhd,bkhd->bhqk', qx[..., d:], k2) * scale
        p = jax.nn.softmax(s1, axis=-1) - lam * jax.nn.softmax(s2, axis=-1)
        return jnp.einsum('bhqk,bkhe->bqhe', p, v)

    o = lax.map(one, qb)
    return jnp.moveaxis(o, 0, 1).reshape(Bsz, T, C_HEADS, 2 * d)


def diff_lambda(lv, layer):
    lam_init = 0.8 - 0.6 * math.exp(-0.3 * layer)
    lam = jnp.exp(jnp.sum(lv[0] * lv[1])) - jnp.exp(jnp.sum(lv[2] * lv[3])) + lam_init
    return lam, lam_init


def diff_output(o, subln_g, lam_init):
    Bsz, T = o.shape[:2]
    of = o * lax.rsqrt(jnp.mean(o * o, axis=-1, keepdims=True) + 1e-6) * subln_g * (1.0 - lam_init)
    return of.reshape(Bsz, T, C_WIDTH)


def peer_ffn(h, wq, sub_keys, u, v):
    Bsz, T, D = h.shape
    half = PEER_QDIM // 2
    qall = mm(h.reshape(Bsz * T, D), wq)
    chunks = (h.reshape(Bsz * T // BLOCK, BLOCK, D), qall.reshape(Bsz * T // BLOCK, BLOCK, -1))

    def one(args):
        hc, q = args
        q = q.reshape(BLOCK, PEER_HEADS, PEER_QDIM)
        s1 = jnp.einsum('rhd,hnd->rhn', q[..., :half], sub_keys[0])
        s2 = jnp.einsum('rhd,hnd->rhn', q[..., half:], sub_keys[1])
        t1, i1 = lax.top_k(s1, PEER_TOPK)
        t2, i2 = lax.top_k(s2, PEER_TOPK)
        cand = (t1[..., :, None] + t2[..., None, :]).reshape(BLOCK, PEER_HEADS, PEER_TOPK * PEER_TOPK)
        cand_idx = (i1[..., :, None] * N_KEYS + i2[..., None, :]).reshape(BLOCK, PEER_HEADS, PEER_TOPK * PEER_TOPK)
        top_s, j = lax.top_k(cand, PEER_TOPK)
        experts = jnp.take_along_axis(cand_idx, j, axis=-1)
        gate = jax.nn.softmax(top_s, axis=-1)
        ue = u[experts]
        ve = v[experts]
        act = jax.nn.gelu(jnp.einsum('rhkd,rd->rhk', ue, hc))
        return jnp.einsum('rhk,rhkd->rd', gate * act, ve)

    return lax.map(one, chunks).reshape(Bsz, T, D)


def adaln_params(cond, lp):
    mod = mm(jax.nn.silu(cond), lp['w_mod']) + lp['b_mod']
    return jnp.split(mod[:, None, :], 6, axis=-1)


def mixer_inputs(x, shift, scale, lp):
    h = layer_norm(x) * (1.0 + scale) + shift
    z = mm3(h, lp['w_in'])
    zA, zB, zC, zG = split_cols(z, [A_COLS, B_WIDTH + 2 * B_KV_WIDTH, 3 * C_WIDTH, 3 * D_MODEL])
    return centred_conv3(zA, lp['a_conv']), zB, zC, zG


def finish_layer(x, yA, yB, yC, zG, gate1, shift2, scale2, gate2, lp):
    gA, gB, gC = jnp.split(jax.nn.sigmoid(zG), 3, axis=-1)
    merged = gA * mm3(yA, lp['p_a']) + gB * mm3(yB, lp['p_b']) + gC * mm3(yC, lp['p_c'])
    x = layer_norm(DEEPNORM_ALPHA * x + gate1 * mm3(merged, lp['w_out']), lp['ln1_g'], lp['ln1_b'])
    h = layer_norm(x) * (1.0 + scale2) + shift2
    y = peer_ffn(h, lp['peer_wq'], lp['peer_keys'], lp['peer_u'], lp['peer_v'])
    return layer_norm(DEEPNORM_ALPHA * x + gate2 * y, lp['ln2_g'], lp['ln2_b'])


def context_layer(x, cond, lp, layer):
    Bsz, L, _ = x.shape
    shift1, scale1, gate1, shift2, scale2, gate2 = adaln_params(cond, lp)
    zA, zB, zC, zG = mixer_inputs(x, shift1, scale1, lp)
    S0 = jnp.zeros((Bsz, A_HEADS, A_HEAD_DIM, A_HEAD_DIM), jnp.float32)
    yA, S_f, S_b = rwkv7_mixer(zA, S0, S0, lp)
    qB, kB, vB = split_cols(zB, [B_WIDTH, B_KV_WIDTH, B_KV_WIDTH])
    qB = qB.reshape(Bsz, L, B_HEADS, B_HEAD_DIM)
    kB = kB.reshape(Bsz, L, B_KV_HEADS, B_HEAD_DIM)
    vB = vB.reshape(Bsz, L, B_KV_HEADS, B_HEAD_DIM)
    yB = gqa_context_attn(qB, kB, vB, lp['b_sink'])
    qC, kC, vC = [t.reshape(Bsz, L, C_HEADS, 2 * C_HEAD_DIM) for t in split_cols(zC, [C_WIDTH, C_WIDTH, C_WIDTH])]
    lam, lam_init = diff_lambda(lp['c_lam'], layer)
    yC = diff_output(diff_attention(qC, kC, vC, lam), lp['c_subln'], lam_init)
    x_new = finish_layer(x, yA, yB, yC, zG, gate1, shift2, scale2, gate2, lp)
    ctx = (S_f, S_b,
           jnp.transpose(kB, (0, 2, 1, 3)), jnp.transpose(vB, (0, 2, 1, 3)),
           jnp.transpose(kC, (0, 2, 1, 3)), jnp.transpose(vC, (0, 2, 1, 3)))
    return x_new, ctx


def latent_layer(x, cond, lp, layer, S_f0, S_b0, kB_c, vB_c, kC_c, vC_c, rope_b, rope_c):
    Bsz, T, _ = x.shape
    shift1, scale1, gate1, shift2, scale2, gate2 = adaln_params(cond, lp)
    zA, zB, zC, zG = mixer_inputs(x, shift1, scale1, lp)
    yA, _, _ = rwkv7_mixer(zA, S_f0, S_b0, lp)
    qB, kB, vB = split_cols(zB, [B_WIDTH, B_KV_WIDTH, B_KV_WIDTH])
    qB = apply_rope(qB.reshape(Bsz, T, B_HEADS, B_HEAD_DIM), *rope_b)
    kB = apply_rope(kB.reshape(Bsz, T, B_KV_HEADS, B_HEAD_DIM), *rope_b)
    vB = vB.reshape(Bsz, T, B_KV_HEADS, B_HEAD_DIM)
    yB = gqa_window_attn(qB, kB, vB, jnp.transpose(kB_c, (0, 2, 1, 3)), jnp.transpose(vB_c, (0, 2, 1, 3)), lp['b_sink'])
    qC, kC, vC = split_cols(zC, [C_WIDTH, C_WIDTH, C_WIDTH])
    qC = apply_rope(qC.reshape(Bsz, T, 2 * C_HEADS, C_HEAD_DIM), *rope_c).reshape(Bsz, T, C_HEADS, 2 * C_HEAD_DIM)
    kC = apply_rope(kC.reshape(Bsz, T, 2 * C_HEADS, C_HEAD_DIM), *rope_c).reshape(Bsz, T, C_HEADS, 2 * C_HEAD_DIM)
    vC = vC.reshape(Bsz, T, C_HEADS, 2 * C_HEAD_DIM)
    k_all = jnp.concatenate([kC, jnp.transpose(kC_c, (0, 2, 1, 3))], axis=1)
    v_all = jnp.concatenate([vC, jnp.transpose(vC_c, (0, 2, 1, 3))], axis=1)
    lam, lam_init = diff_lambda(lp['c_lam'], layer)
    yC = diff_output(diff_attention(qC, k_all, v_all, lam), lp['c_subln'], lam_init)
    return finish_layer(x, yA, yB, yC, zG, gate1, shift2, scale2, gate2, lp)


_BF16_WEIGHTS = ('w_mod', 'w_in', 'a_w_up', 'a_a_up', 'a_g_up', 'p_a', 'p_b', 'p_c', 'w_out', 'peer_wq')


def kernel(x_prompt, x_sample, state_a_fwd, state_a_bwd, cache_b_k, cache_b_v, cache_c_k, cache_c_v, c, c_ctx, w_mod, b_mod, w_in, a_conv, a_w0, a_w_up, a_a0, a_a_up, a_g_up, a_k_k, a_k_a, a_r_k, a_gn_g, a_gn_b, b_sink, c_lam, c_subln, p_a, p_b, p_c, w_out, ln1_g, ln1_b, ln2_g, ln2_b, peer_wq, peer_keys, peer_u, peer_v):
    params = dict(w_mod=w_mod, b_mod=b_mod, w_in=w_in, a_conv=a_conv, a_w0=a_w0, a_w_up=a_w_up, a_a0=a_a0,
                  a_a_up=a_a_up, a_g_up=a_g_up, a_k_k=a_k_k, a_k_a=a_k_a, a_r_k=a_r_k, a_gn_g=a_gn_g,
                  a_gn_b=a_gn_b, b_sink=b_sink, c_lam=c_lam, c_subln=c_subln, p_a=p_a, p_b=p_b, p_c=p_c,
                  w_out=w_out, ln1_g=ln1_g, ln1_b=ln1_b, ln2_g=ln2_g, ln2_b=ln2_b, peer_wq=peer_wq,
                  peer_keys=peer_keys, peer_u=peer_u, peer_v=peer_v)
    for name in _BF16_WEIGHTS:
        params[name] = params[name].astype(jnp.bfloat16)
    n_lat = x_sample.shape[1]
    rope_b = axial_rope_tables(n_lat, B_HEAD_DIM)
    rope_c = axial_rope_tables(n_lat, C_HEAD_DIM)
    cond_ctx = jnp.broadcast_to(c_ctx[None, :], (x_prompt.shape[0], D_MODEL))
    y_prompt, y_sample = x_prompt, x_sample
    new = [[], [], [], [], [], []]
    for layer in range(DEPTH):
        lp = {name: val[layer] for name, val in params.items()}
        y_prompt, ctx = context_layer(y_prompt, cond_ctx, lp, layer)
        for lst, t in zip(new, ctx):
            lst.append(t)
        y_sample = latent_layer(y_sample, c, lp, layer,
                                state_a_fwd[:, layer], state_a_bwd[:, layer],
                                cache_b_k[:, layer], cache_b_v[:, layer],
                                cache_c_k[:, layer], cache_c_v[:, layer], rope_b, rope_c)
    return (y_prompt, y_sample) + tuple(jnp.stack(lst, axis=1) for lst in new)
```

```python
import math
from functools import partial

import jax
import jax.numpy as jnp
import numpy as np
from jax import lax
from jax.experimental import pallas as pl
from jax.experimental.pallas import tpu as pltpu

D_MODEL = 2048
DEPTH = 4
GRID_W = 64
BLOCK = 128
A_HEADS = 16
A_HEAD_DIM = 64
A_WIDTH = A_HEADS * A_HEAD_DIM
W_LORA = 64
ICL_LORA = 64
G_LORA = 128
A_COLS = 3 * A_WIDTH + G_LORA + 2 * W_LORA + 2 * ICL_LORA
A_GN_EPS = 64e-5
B_HEADS = 4
B_KV_HEADS = 2
B_HEAD_DIM = 128
B_WIDTH = B_HEADS * B_HEAD_DIM
B_KV_WIDTH = B_KV_HEADS * B_HEAD_DIM
WINDOW = 128
C_HEADS = 4
C_HEAD_DIM = 64
C_WIDTH = C_HEADS * 2 * C_HEAD_DIM
PEER_HEADS = 8
N_KEYS = 128
PEER_QDIM = 256
PEER_TOPK = 16
ROPE_BASE = 10000.0
LN_EPS = 1e-5
NEG_INF = -1e30
DEEPNORM_ALPHA = (2 * DEPTH) ** 0.25

LANE = 128
VMEM_LIMIT_BYTES = 56 * 1024 * 1024


def _mm_kernel(x_ref, w_ref, o_ref):
    o_ref[...] = jnp.dot(x_ref[...].astype(jnp.bfloat16), w_ref[...],
                         preferred_element_type=jnp.float32)


def _pick_tile(n, cap, unit):
    if n <= cap:
        return n
    best = None
    for t in range(unit, cap + 1, unit):
        if n % t == 0:
            best = t
    assert best is not None, (n, cap, unit)
    return best


def mm(x, w):
    M, K = x.shape
    N = w.shape[1]
    tm = _pick_tile(M, 512, 8)
    tn = _pick_tile(N, 1024, LANE)
    return pl.pallas_call(
        _mm_kernel,
        grid=(N // tn, M // tm),
        in_specs=[pl.BlockSpec((tm, K), lambda j, i: (i, 0)),
                  pl.BlockSpec((K, tn), lambda j, i: (0, j))],
        out_specs=pl.BlockSpec((tm, tn), lambda j, i: (i, j)),
        out_shape=jax.ShapeDtypeStruct((M, N), jnp.float32),
        compiler_params=pltpu.CompilerParams(
            dimension_semantics=("arbitrary", "arbitrary"),
            vmem_limit_bytes=VMEM_LIMIT_BYTES),
        name="mm",
    )(x, w)


def mm3(x, w):
    B, T, K = x.shape
    return mm(x.reshape(B * T, K), w).reshape(B, T, w.shape[1])


RWKV_CHUNK = 64
RWKV_HEADS_PER_STEP = 8

_NN = (((2,), (1,)), ((0,), (0,)))
_NT = (((2,), (2,)), ((0,), (0,)))
_TN = (((1,), (1,)), ((0,), (0,)))


def _split2(x):
    hi = x.astype(jnp.bfloat16)
    lo = (x - hi.astype(jnp.float32)).astype(jnp.bfloat16)
    return hi, lo


def _dot3(a, b, dims=_NN):
    f = lambda x, y: lax.dot_general(x, y, dims, preferred_element_type=jnp.float32)
    return f(a[0], b[0]) + (f(a[0], b[1]) + f(a[1], b[0]))


def _rwkv_chunk_kernel(r_ref, v_ref, a_ref, lw_ref, k_ref, b_ref, s0_ref, y_ref, sfin_ref, st_ref):
    d = pl.program_id(0)
    c = pl.program_id(3)
    n_chunks = pl.num_programs(3)
    HB, C, _ = r_ref.shape

    @pl.when(c == 0)
    def _():
        st_ref[...] = s0_ref[...]

    ti = lax.broadcasted_iota(jnp.int32, (HB, C, C), 1)
    si = lax.broadcasted_iota(jnp.int32, (HB, C, C), 2)
    diff = (si - ti) * (1 - 2 * d)
    m_incl = diff <= 0
    m_strict = diff < 0
    m_incl_bf = jnp.where(m_incl, 1.0, 0.0).astype(jnp.bfloat16)
    eye = jnp.where(diff == 0, 1.0, 0.0).astype(jnp.float32)

    r, v, a = r_ref[...], v_ref[...], a_ref[...]
    lw, k, b = lw_ref[...], k_ref[...], b_ref[...]
    st = st_ref[...]

    lw_hi, lw_lo = _split2(lw)
    lw_lo2 = (lw - lw_hi.astype(jnp.float32) - lw_lo.astype(jnp.float32)).astype(jnp.bfloat16)
    f = lambda y: lax.dot_general(m_incl_bf, y, _NN, preferred_element_type=jnp.float32)
    cum = f(lw_hi) + (f(lw_lo) + f(lw_lo2))
    total = jnp.sum(lw, axis=1, keepdims=True)
    e_cum = jnp.exp(cum)
    e_inv = jnp.exp(-cum)
    e_tot = jnp.exp(total)
    at = _split2(a * jnp.exp(cum - lw))
    rt = _split2(r * e_cum)
    bt_f = b * e_inv
    kt_f = k * e_inv
    bt, kt = _split2(bt_f), _split2(kt_f)
    bh, kh = _split2(bt_f * e_tot), _split2(kt_f * e_tot)
    vs = _split2(v)

    l_ab = jnp.where(m_strict, _dot3(at, bt, _NT), 0.0)
    l_ak = jnp.where(m_strict, _dot3(at, kt, _NT), 0.0)
    m_rb = jnp.where(m_incl, _dot3(rt, bt, _NT), 0.0)
    m_rk = jnp.where(m_incl, _dot3(rt, kt, _NT), 0.0)

    tm = eye + l_ab
    lp = _split2(l_ab)
    for _ in range(int(math.log2(C)) - 1):
        lp = _split2(_dot3(lp, lp))
        tm = tm + _dot3(_split2(tm), lp)
    tms = _split2(tm)
    p = _split2(_dot3(tms, at))
    q = _dot3(tms, _split2(_dot3(_split2(l_ak), vs)))

    sts = _split2(st)
    u = _split2(_dot3(p, sts) + q)
    y_ref[...] = _dot3(rt, sts) + _dot3(_split2(m_rb), u) + _dot3(_split2(m_rk), vs)
    st_ref[...] = jnp.swapaxes(e_tot, 1, 2) * st + _dot3(bh, u, _TN) + _dot3(kh, vs, _TN)

    @pl.when(c == n_chunks - 1)
    def _():
        sfin_ref[...] = st_ref[...]


def rwkv7_chunked(r, v, a, lw, k, b, s0t):
    D, B, H, T, N = lw.shape
    C, HB = RWKV_CHUNK, RWKV_HEADS_PER_STEP
    assert T % C == 0 and H % HB == 0
    n_chunks = T // C
    chunk = lambda d, c: c + d * (n_chunks - 1 - 2 * c)
    shared = pl.BlockSpec((None, HB, C, N), lambda d, bb, g, c: (bb, g, chunk(d, c), 0))
    per_dir = pl.BlockSpec((None, None, HB, C, N), lambda d, bb, g, c: (d, bb, g, chunk(d, c), 0))
    state = pl.BlockSpec((None, None, HB, N, N), lambda d, bb, g, c: (d, bb, g, 0, 0))
    return pl.pallas_call(
        _rwkv_chunk_kernel,
        grid=(D, B, H // HB, n_chunks),
        in_specs=[shared, shared, shared, per_dir, per_dir, per_dir, state],
        out_specs=[per_dir, state],
        out_shape=[jax.ShapeDtypeStruct((D, B, H, T, N), jnp.float32),
                   jax.ShapeDtypeStruct((D, B, H, N, N), jnp.float32)],
        scratch_shapes=[pltpu.VMEM((HB, N, N), jnp.float32)],
        compiler_params=pltpu.CompilerParams(
            dimension_semantics=("arbitrary", "arbitrary", "arbitrary", "arbitrary"),
            vmem_limit_bytes=VMEM_LIMIT_BYTES),
        name="rwkv7_chunked",
    )(r, v, a, lw, k, b, s0t)


PEER_ROUTE_TOKENS = 256
PEER_TOKENS = 512
PEER_E1_PER_STEP = 8


def _top_rows(s_ref, n_rows, k, emit):
    for a in range(k):
        s = s_ref[0:n_rows, :]
        mx = jnp.max(s, axis=0, keepdims=True)
        emit(a, mx)
        if a + 1 < k:
            s_ref[0:n_rows, :] = jnp.where(s == mx, NEG_INF, s)


def _peer_route_kernel(ht_ref, wqt_ref, k1_ref, k2_ref, s1_ref, s2_ref, g1_ref, e2_ref, th_ref,
                       work_ref, t1_ref, t2_ref, cand_ref):
    half = PEER_QDIM // 2
    qt = jnp.dot(wqt_ref[...], ht_ref[...], preferred_element_type=jnp.float32)
    for h in range(PEER_HEADS):
        for which, (k_ref, s_out, t_ref) in enumerate(((k1_ref, s1_ref, t1_ref), (k2_ref, s2_ref, t2_ref))):
            q = qt[h * PEER_QDIM + which * half: h * PEER_QDIM + (which + 1) * half, :]
            q_hi, q_lo = _split2(q)
            k_hi, k_lo = _split2(k_ref[h])
            f = lambda x, y: jnp.dot(x, y, preferred_element_type=jnp.float32)
            s = f(k_hi, q_hi) + (f(k_hi, q_lo) + f(k_lo, q_hi))
            s_out[h] = s
            work_ref[...] = s

            def emit(a, mx, t_ref=t_ref):
                t_ref[a:a + 1, :] = mx
            _top_rows(work_ref, N_KEYS, PEER_TOPK, emit)
        t1 = t1_ref[...]
        t2 = t2_ref[...]
        m1 = t1[0:1, :]
        m2 = t2[0:1, :]
        for a in range(PEER_TOPK):
            cand_ref[a * PEER_TOPK:(a + 1) * PEER_TOPK, :] = t1[a:a + 1, :] + t2
        acc = {}

        def emit_c(a, mx):
            e = jnp.exp(mx - (m1 + m2))
            acc['z'] = e if a == 0 else acc['z'] + e
            acc['th'] = mx
        _top_rows(cand_ref, PEER_TOPK * PEER_TOPK, PEER_TOPK, emit_c)
        th_ref[h:h + 1, :] = acc['th']
        g1_ref[h] = jnp.exp(s1_ref[h] - m1) / acc['z']
        e2_ref[h] = jnp.exp(s2_ref[h] - m2)


def peer_route(ht, wqt, keys):
    D, N = ht.shape
    tm = PEER_ROUTE_TOKENS
    assert N % tm == 0
    big = jax.ShapeDtypeStruct((PEER_HEADS, N_KEYS, N), jnp.float32)
    big_spec = pl.BlockSpec((PEER_HEADS, N_KEYS, tm), lambda i: (0, 0, i))
    key_spec = pl.BlockSpec((PEER_HEADS, N_KEYS, PEER_QDIM // 2), lambda i: (0, 0, 0))
    return pl.pallas_call(
        _peer_route_kernel,
        grid=(N // tm,),
        in_specs=[pl.BlockSpec((D, tm), lambda i: (0, i)),
                  pl.BlockSpec((PEER_HEADS * PEER_QDIM, D), lambda i: (0, 0)),
                  key_spec, key_spec],
        out_specs=[big_spec, big_spec, big_spec, big_spec, pl.BlockSpec((PEER_HEADS, tm), lambda i: (0, i))],
        out_shape=[big, big, big, big, jax.ShapeDtypeStruct((PEER_HEADS, N), jnp.float32)],
        scratch_shapes=[pltpu.VMEM((N_KEYS, tm), jnp.float32),
                        pltpu.VMEM((PEER_TOPK, tm), jnp.float32),
                        pltpu.VMEM((PEER_TOPK, tm), jnp.float32),
                        pltpu.VMEM((PEER_TOPK * PEER_TOPK, tm), jnp.float32)],
        compiler_params=pltpu.CompilerParams(
            dimension_semantics=("arbitrary",), vmem_limit_bytes=VMEM_LIMIT_BYTES),
        name="peer_route",
    )(ht, wqt, keys[0], keys[1])


def _gelu_tanh(x):
    return 0.5 * x * (1.0 + jnp.tanh(math.sqrt(2.0 / math.pi) * (x + 0.044715 * (x * x * x))))


def _peer_expert_kernel(ht_ref, u_ref, vt_ref, s1_ref, s2_ref, g1_ref, e2_ref, th_ref, o_ref, wg_ref):
    j = pl.program_id(1)
    tm = ht_ref.shape[1]

    @pl.when(j == 0)
    def _():
        o_ref[...] = jnp.zeros_like(o_ref)

    act = _gelu_tanh(jnp.dot(u_ref[...], ht_ref[...], preferred_element_type=jnp.float32))
    e1_rows = pl.ds(pl.multiple_of(j * PEER_E1_PER_STEP, PEER_E1_PER_STEP), PEER_E1_PER_STEP)
    for tc in range(tm // LANE):
        cols = slice(tc * LANE, (tc + 1) * LANE)
        s1_rows = [s1_ref[h, e1_rows, cols] for h in range(PEER_HEADS)]
        g1_rows = [g1_ref[h, e1_rows, cols] for h in range(PEER_HEADS)]
        for e in range(PEER_E1_PER_STEP):
            w = None
            for h in range(PEER_HEADS):
                score = s2_ref[h, :, cols] + s1_rows[h][e:e + 1, :]
                gate = e2_ref[h, :, cols] * g1_rows[h][e:e + 1, :]
                term = jnp.where(score >= th_ref[h:h + 1, cols], gate, 0.0)
                w = term if w is None else w + term
            rows = slice(e * N_KEYS, (e + 1) * N_KEYS)
            wg_ref[rows, cols] = (w * act[rows, cols]).astype(jnp.bfloat16)
    o_ref[...] += jnp.dot(vt_ref[...], wg_ref[...], preferred_element_type=jnp.float32)


def peer_experts(ht, u, vt, s1, s2, g1, e2, th):
    D, N = ht.shape
    E = u.shape[0]
    tm = PEER_TOKENS
    te = PEER_E1_PER_STEP * N_KEYS
    assert N % tm == 0 and E % te == 0
    big_spec = pl.BlockSpec((PEER_HEADS, N_KEYS, tm), lambda i, j: (0, 0, i))
    return pl.pallas_call(
        _peer_expert_kernel,
        grid=(N // tm, E // te),
        in_specs=[pl.BlockSpec((D, tm), lambda i, j: (0, i)),
                  pl.BlockSpec((te, D), lambda i, j: (j, 0)),
                  pl.BlockSpec((D, te), lambda i, j: (0, j)),
                  big_spec, big_spec, big_spec, big_spec,
                  pl.BlockSpec((PEER_HEADS, tm), lambda i, j: (0, i))],
        out_specs=pl.BlockSpec((D, tm), lambda i, j: (0, i)),
        out_shape=jax.ShapeDtypeStruct((D, N), jnp.float32),
        scratch_shapes=[pltpu.VMEM((te, tm), jnp.bfloat16)],
        compiler_params=pltpu.CompilerParams(
            dimension_semantics=("arbitrary", "arbitrary"), vmem_limit_bytes=VMEM_LIMIT_BYTES),
        name="peer_experts",
    )(ht, u, vt, s1, s2, g1, e2, th)


def peer_ffn_tokens(h, wqt, keys, u, vt):
    ht = jnp.transpose(h).astype(jnp.bfloat16)
    s1, s2, g1, e2, th = peer_route(ht, wqt, keys)
    return jnp.transpose(peer_experts(ht, u, vt, s1, s2, g1, e2, th))


def split_cols(z, sizes):
    return jnp.split(z, [int(s) for s in np.cumsum(sizes)[:-1]], axis=-1)


def layer_norm(x, g=None, b=None):
    mu = jnp.mean(x, axis=-1, keepdims=True)
    var = jnp.mean(jnp.square(x - mu), axis=-1, keepdims=True)
    y = (x - mu) * lax.rsqrt(var + LN_EPS)
    if g is not None:
        y = y * g + b
    return y


def centred_conv3(x, w):
    xp = jnp.pad(x, ((0, 0), (1, 1), (0, 0)))
    return xp[:, :-2] * w[0] + xp[:, 1:-1] * w[1] + xp[:, 2:] * w[2]


def axial_rope_tables(n_tokens, head_dim):
    rows = n_tokens // GRID_W
    row_pos = jnp.repeat(jnp.arange(rows, dtype=jnp.float32), GRID_W)
    col_pos = jnp.tile(jnp.arange(GRID_W, dtype=jnp.float32), rows)
    n_freq = head_dim // 4
    freqs = ROPE_BASE ** (-jnp.arange(n_freq, dtype=jnp.float32) / n_freq)
    ang = jnp.concatenate([row_pos[:, None] * freqs, col_pos[:, None] * freqs], axis=-1)
    return jnp.cos(ang), jnp.sin(ang)


def apply_rope(x, cos, sin):
    x1, x2 = x[..., 0::2], x[..., 1::2]
    c, s = cos[None, :, None, :], sin[None, :, None, :]
    return jnp.stack([x1 * c - x2 * s, x1 * s + x2 * c], axis=-1).reshape(x.shape)


def rwkv7_mixer(za, S0_f, S0_b, lp):
    Bsz, T, _ = za.shape
    r, k, v, g_d, wd_f, wd_b, ad_f, ad_b = split_cols(
        za, [A_WIDTH, A_WIDTH, A_WIDTH, G_LORA, W_LORA, W_LORA, ICL_LORA, ICL_LORA])
    heads = lambda t: t.reshape(Bsz, T, A_HEADS, A_HEAD_DIM)
    by_head = lambda t: jnp.transpose(t, (0, 2, 1, 3))
    rh, vh = heads(r), heads(v)
    kk = heads(k * lp['a_k_k'])
    kk = kk * lax.rsqrt(jnp.sum(kk * kk, axis=-1, keepdims=True) + 1e-12)
    lws, kds, bs = [], [], []
    for d, (wd, ad) in enumerate(((wd_f, ad_f), (wd_b, ad_b))):
        wz = lp['a_w0'][d] + mm3(jnp.tanh(wd), lp['a_w_up'][d])
        lws.append(heads(-math.exp(-0.5) * jax.nn.sigmoid(wz)))
        icl = jax.nn.sigmoid(lp['a_a0'][d] + mm3(ad, lp['a_a_up'][d]))
        kds.append(heads(k * (1.0 + (icl - 1.0) * lp['a_k_a'])))
        bs.append(kk * heads(icl))
    stack = lambda ts: jnp.stack([by_head(t) for t in ts], axis=0)
    s0t = jnp.swapaxes(jnp.stack([S0_f, S0_b], axis=0), -1, -2)
    ys, sfin_t = rwkv7_chunked(by_head(rh), by_head(vh), by_head(-kk), stack(lws), stack(kds), stack(bs), s0t)
    states = jnp.swapaxes(sfin_t, -1, -2)
    y = jnp.transpose(ys[0] + ys[1], (0, 2, 1, 3))
    mu = jnp.mean(y, axis=-1, keepdims=True)
    var = jnp.mean(jnp.square(y - mu), axis=-1, keepdims=True)
    y = ((y - mu) * lax.rsqrt(var + A_GN_EPS)).reshape(Bsz, T, A_WIDTH) * lp['a_gn_g'] + lp['a_gn_b']
    bonus = jnp.sum(rh * (kds[0] + kds[1]) * lp['a_r_k'], axis=-1, keepdims=True) * vh
    g = mm3(jax.nn.sigmoid(g_d), lp['a_g_up'])
    out = (y + bonus.reshape(Bsz, T, A_WIDTH)) * g
    return out, states[0], states[1]


def gqa_context_attn(q, k, v, sink):
    Bsz, T = q.shape[:2]
    groups = B_HEADS // B_KV_HEADS
    qb = jnp.moveaxis(q.reshape(Bsz, T // BLOCK, BLOCK, B_KV_HEADS, groups, B_HEAD_DIM), 1, 0)
    sink_col = sink.reshape(1, B_KV_HEADS, groups, 1, 1)

    def one(qx):
        s = jnp.einsum('bqkgd,blkd->bkgql', qx, k) * (B_HEAD_DIM ** -0.5)
        sc = jnp.broadcast_to(sink_col, s.shape[:-1] + (1,))
        p = jax.nn.softmax(jnp.concatenate([s, sc], axis=-1), axis=-1)[..., :-1]
        return jnp.einsum('bkgql,blkd->bqkgd', p, v)

    o = lax.map(one, qb)
    return jnp.moveaxis(o, 0, 1).reshape(Bsz, T, B_WIDTH)


def gqa_window_attn(q, k, v, kc, vc, sink):
    Bsz, T = q.shape[:2]
    nb = T // BLOCK
    groups = B_HEADS // B_KV_HEADS
    qb = q.reshape(Bsz, nb, BLOCK, B_KV_HEADS, groups, B_HEAD_DIM)
    pad = ((0, 0), (BLOCK, BLOCK), (0, 0), (0, 0))
    idx = jnp.arange(nb)[:, None] * BLOCK + jnp.arange(3 * BLOCK)[None, :]
    kw = jnp.pad(k, pad)[:, idx]
    vw = jnp.pad(v, pad)[:, idx]
    qpos = jnp.arange(nb)[:, None] * BLOCK + jnp.arange(BLOCK)[None, :]
    kpos = idx - BLOCK
    valid = ((jnp.abs(qpos[:, :, None] - kpos[:, None, :]) <= WINDOW)
             & (kpos[:, None, :] >= 0) & (kpos[:, None, :] < T))
    scale = B_HEAD_DIM ** -0.5
    s_w = jnp.einsum('bnqkgd,bnjkd->bnkgqj', qb, kw) * scale
    s_w = jnp.where(valid[None, :, None, None], s_w, NEG_INF)
    s_c = jnp.einsum('bnqkgd,blkd->bnkgql', qb, kc) * scale
    sc = jnp.broadcast_to(sink.reshape(1, 1, B_KV_HEADS, groups, 1, 1), s_w.shape[:-1] + (1,))
    p = jax.nn.softmax(jnp.concatenate([s_w, s_c, sc], axis=-1), axis=-1)
    nw = 3 * BLOCK
    o = (jnp.einsum('bnkgqj,bnjkd->bnqkgd', p[..., :nw], vw)
         + jnp.einsum('bnkgql,blkd->bnqkgd', p[..., nw:-1], vc))
    return o.reshape(Bsz, T, B_WIDTH)


def diff_attention(q, k, v, lam):
    Bsz, T = q.shape[:2]
    d = C_HEAD_DIM
    qb = jnp.moveaxis(q.reshape(Bsz, T // BLOCK, BLOCK, C_HEADS, 2 * d), 1, 0)
    k1, k2 = k[..., :d], k[..., d:]
    scale = d ** -0.5

    def one(qx):
        s1 = jnp.einsum('bqhd,bkhd->bhqk', qx[..., :d], k1) * scale
        s2 = jnp.einsum('bqhd,bkhd->bhqk', qx[..., d:], k2) * scale
        p = jax.nn.softmax(s1, axis=-1) - lam * jax.nn.softmax(s2, axis=-1)
        return jnp.einsum('bhqk,bkhe->bqhe', p, v)

    o = lax.map(one, qb)
    return jnp.moveaxis(o, 0, 1).reshape(Bsz, T, C_HEADS, 2 * d)


def diff_lambda(lv, layer):
    lam_init = 0.8 - 0.6 * math.exp(-0.3 * layer)
    lam = jnp.exp(jnp.sum(lv[0] * lv[1])) - jnp.exp(jnp.sum(lv[2] * lv[3])) + lam_init
    return lam, lam_init


def diff_output(o, subln_g, lam_init):
    Bsz, T = o.shape[:2]
    of = o * lax.rsqrt(jnp.mean(o * o, axis=-1, keepdims=True) + 1e-6) * subln_g * (1.0 - lam_init)
    return of.reshape(Bsz, T, C_WIDTH)


def adaln_params(cond, lp):
    mod = mm(jax.nn.silu(cond), lp['w_mod']) + lp['b_mod']
    return jnp.split(mod[:, None, :], 6, axis=-1)


def mixer_inputs(x, shift, scale, lp):
    h = layer_norm(x) * (1.0 + scale) + shift
    z = mm3(h, lp['w_in'])
    zA, zB, zC, zG = split_cols(z, [A_COLS, B_WIDTH + 2 * B_KV_WIDTH, 3 * C_WIDTH, 3 * D_MODEL])
    return centred_conv3(zA, lp['a_conv']), zB, zC, zG


def merge_mixers(x, yA, yB, yC, zG, gate1, shift2, scale2, lp):
    gA, gB, gC = jnp.split(jax.nn.sigmoid(zG), 3, axis=-1)
    merged = gA * mm3(yA, lp['p_a']) + gB * mm3(yB, lp['p_b']) + gC * mm3(yC, lp['p_c'])
    x = layer_norm(DEEPNORM_ALPHA * x + gate1 * mm3(merged, lp['w_out']), lp['ln1_g'], lp['ln1_b'])
    h = layer_norm(x) * (1.0 + scale2) + shift2
    return x, h


def peer_and_norm(parts, lp):
    sizes = [x.shape[0] * x.shape[1] for x, _, _ in parts]
    h_all = jnp.concatenate([h.reshape(-1, D_MODEL) for _, h, _ in parts], axis=0)
    y_all = peer_ffn_tokens(h_all, lp['peer_wqt'], lp['peer_keys'], lp['peer_u'], lp['peer_vt'])
    outs, start = [], 0
    for (x, _, gate2), n in zip(parts, sizes):
        y = y_all[start:start + n].reshape(x.shape)
        start += n
        outs.append(layer_norm(DEEPNORM_ALPHA * x + gate2 * y, lp['ln2_g'], lp['ln2_b']))
    return outs


def context_layer(x, cond, lp, layer):
    Bsz, L, _ = x.shape
    shift1, scale1, gate1, shift2, scale2, gate2 = adaln_params(cond, lp)
    zA, zB, zC, zG = mixer_inputs(x, shift1, scale1, lp)
    S0 = jnp.zeros((Bsz, A_HEADS, A_HEAD_DIM, A_HEAD_DIM), jnp.float32)
    yA, S_f, S_b = rwkv7_mixer(zA, S0, S0, lp)
    qB, kB, vB = split_cols(zB, [B_WIDTH, B_KV_WIDTH, B_KV_WIDTH])
    qB = qB.reshape(Bsz, L, B_HEADS, B_HEAD_DIM)
    kB = kB.reshape(Bsz, L, B_KV_HEADS, B_HEAD_DIM)
    vB = vB.reshape(Bsz, L, B_KV_HEADS, B_HEAD_DIM)
    yB = gqa_context_attn(qB, kB, vB, lp['b_sink'])
    qC, kC, vC = [t.reshape(Bsz, L, C_HEADS, 2 * C_HEAD_DIM) for t in split_cols(zC, [C_WIDTH, C_WIDTH, C_WIDTH])]
    lam, lam_init = diff_lambda(lp['c_lam'], layer)
    yC = diff_output(diff_attention(qC, kC, vC, lam), lp['c_subln'], lam_init)
    x_mid, h_mid = merge_mixers(x, yA, yB, yC, zG, gate1, shift2, scale2, lp)
    ctx = (S_f, S_b,
           jnp.transpose(kB, (0, 2, 1, 3)), jnp.transpose(vB, (0, 2, 1, 3)),
           jnp.transpose(kC, (0, 2, 1, 3)), jnp.transpose(vC, (0, 2, 1, 3)))
    return (x_mid, h_mid, gate2), ctx


def latent_layer(x, cond, lp, layer, S_f0, S_b0, kB_c, vB_c, kC_c, vC_c, rope_b, rope_c):
    Bsz, T, _ = x.shape
    shift1, scale1, gate1, shift2, scale2, gate2 = adaln_params(cond, lp)
    zA, zB, zC, zG = mixer_inputs(x, shift1, scale1, lp)
    yA, _, _ = rwkv7_mixer(zA, S_f0, S_b0, lp)
    qB, kB, vB = split_cols(zB, [B_WIDTH, B_KV_WIDTH, B_KV_WIDTH])
    qB = apply_rope(qB.reshape(Bsz, T, B_HEADS, B_HEAD_DIM), *rope_b)
    kB = apply_rope(kB.reshape(Bsz, T, B_KV_HEADS, B_HEAD_DIM), *rope_b)
    vB = vB.reshape(Bsz, T, B_KV_HEADS, B_HEAD_DIM)
    yB = gqa_window_attn(qB, kB, vB, jnp.transpose(kB_c, (0, 2, 1, 3)), jnp.transpose(vB_c, (0, 2, 1, 3)), lp['b_sink'])
    qC, kC, vC = split_cols(zC, [C_WIDTH, C_WIDTH, C_WIDTH])
    qC = apply_rope(qC.reshape(Bsz, T, 2 * C_HEADS, C_HEAD_DIM), *rope_c).reshape(Bsz, T, C_HEADS, 2 * C_HEAD_DIM)
    kC = apply_rope(kC.reshape(Bsz, T, 2 * C_HEADS, C_HEAD_DIM), *rope_c).reshape(Bsz, T, C_HEADS, 2 * C_HEAD_DIM)
    vC = vC.reshape(Bsz, T, C_HEADS, 2 * C_HEAD_DIM)
    k_all = jnp.concatenate([kC, jnp.transpose(kC_c, (0, 2, 1, 3))], axis=1)
    v_all = jnp.concatenate([vC, jnp.transpose(vC_c, (0, 2, 1, 3))], axis=1)
    lam, lam_init = diff_lambda(lp['c_lam'], layer)
    yC = diff_output(diff_attention(qC, k_all, v_all, lam), lp['c_subln'], lam_init)
    x_mid, h_mid = merge_mixers(x, yA, yB, yC, zG, gate1, shift2, scale2, lp)
    return x_mid, h_mid, gate2


_BF16_WEIGHTS = ('w_mod', 'w_in', 'a_w_up', 'a_a_up', 'a_g_up', 'p_a', 'p_b', 'p_c', 'w_out', 'peer_u')


def kernel(x_prompt, x_sample, state_a_fwd, state_a_bwd, cache_b_k, cache_b_v, cache_c_k, cache_c_v, c, c_ctx, w_mod, b_mod, w_in, a_conv, a_w0, a_w_up, a_a0, a_a_up, a_g_up, a_k_k, a_k_a, a_r_k, a_gn_g, a_gn_b, b_sink, c_lam, c_subln, p_a, p_b, p_c, w_out, ln1_g, ln1_b, ln2_g, ln2_b, peer_wq, peer_keys, peer_u, peer_v):
    params = dict(w_mod=w_mod, b_mod=b_mod, w_in=w_in, a_conv=a_conv, a_w0=a_w0, a_w_up=a_w_up, a_a0=a_a0,
                  a_a_up=a_a_up, a_g_up=a_g_up, a_k_k=a_k_k, a_k_a=a_k_a, a_r_k=a_r_k, a_gn_g=a_gn_g,
                  a_gn_b=a_gn_b, b_sink=b_sink, c_lam=c_lam, c_subln=c_subln, p_a=p_a, p_b=p_b, p_c=p_c,
                  w_out=w_out, ln1_g=ln1_g, ln1_b=ln1_b, ln2_g=ln2_g, ln2_b=ln2_b, peer_wq=peer_wq,
                  peer_keys=peer_keys, peer_u=peer_u, peer_v=peer_v)
    for name in _BF16_WEIGHTS:
        params[name] = params[name].astype(jnp.bfloat16)
    params['peer_wqt'] = jnp.swapaxes(params.pop('peer_wq'), 1, 2).astype(jnp.bfloat16)
    params['peer_vt'] = jnp.swapaxes(params.pop('peer_v'), 1, 2).astype(jnp.bfloat16)
    n_lat = x_sample.shape[1]
    rope_b = axial_rope_tables(n_lat, B_HEAD_DIM)
    rope_c = axial_rope_tables(n_lat, C_HEAD_DIM)
    cond_ctx = jnp.broadcast_to(c_ctx[None, :], (x_prompt.shape[0], D_MODEL))
    y_prompt, y_sample = x_prompt, x_sample
    new = [[], [], [], [], [], []]
    for layer in range(DEPTH):
        lp = {name: val[layer] for name, val in params.items()}
        part_ctx, ctx = context_layer(y_prompt, cond_ctx, lp, layer)
        for lst, t in zip(new, ctx):
            lst.append(t)
        part_lat = latent_layer(y_sample, c, lp, layer,
                                state_a_fwd[:, layer], state_a_bwd[:, layer],
                                cache_b_k[:, layer], cache_b_v[:, layer],
                                cache_c_k[:, layer], cache_c_v[:, layer], rope_b, rope_c)
        y_prompt, y_sample = peer_and_norm([part_ctx, part_lat], lp)
    return (y_prompt, y_sample) + tuple(jnp.stack(lst, axis=1) for lst in new)
```

```python
import math
from functools import partial

import jax
import jax.numpy as jnp
import numpy as np
from jax import lax
from jax.experimental import pallas as pl
from jax.experimental.pallas import tpu as pltpu

D_MODEL = 2048
DEPTH = 4
GRID_W = 64
BLOCK = 128
A_HEADS = 16
A_HEAD_DIM = 64
A_WIDTH = A_HEADS * A_HEAD_DIM
W_LORA = 64
ICL_LORA = 64
G_LORA = 128
A_COLS = 3 * A_WIDTH + G_LORA + 2 * W_LORA + 2 * ICL_LORA
A_GN_EPS = 64e-5
B_HEADS = 4
B_KV_HEADS = 2
B_HEAD_DIM = 128
B_WIDTH = B_HEADS * B_HEAD_DIM
B_KV_WIDTH = B_KV_HEADS * B_HEAD_DIM
WINDOW = 128
C_HEADS = 4
C_HEAD_DIM = 64
C_WIDTH = C_HEADS * 2 * C_HEAD_DIM
IN_COLS = A_COLS + B_WIDTH + 2 * B_KV_WIDTH + 3 * C_WIDTH + 3 * D_MODEL
PEER_HEADS = 8
N_KEYS = 128
PEER_QDIM = 256
PEER_TOPK = 16
ROPE_BASE = 10000.0
LN_EPS = 1e-5
NEG_INF = -1e30
DEEPNORM_ALPHA = (2 * DEPTH) ** 0.25

LANE = 128
VMEM_LIMIT_BYTES = 56 * 1024 * 1024


def _mm_kernel(x_ref, w_ref, o_ref):
    o_ref[...] = jnp.dot(x_ref[...].astype(jnp.bfloat16), w_ref[...],
                         preferred_element_type=jnp.float32)


def _pick_tile(n, cap, unit):
    if n <= cap:
        return n
    best = None
    for t in range(unit, cap + 1, unit):
        if n % t == 0:
            best = t
    assert best is not None, (n, cap, unit)
    return best


def mm(x, w):
    M, K = x.shape
    N = w.shape[1]
    tm = _pick_tile(M, 512, 8)
    tn = _pick_tile(N, 1024, LANE)
    return pl.pallas_call(
        _mm_kernel,
        grid=(N // tn, M // tm),
        in_specs=[pl.BlockSpec((tm, K), lambda j, i: (i, 0)),
                  pl.BlockSpec((K, tn), lambda j, i: (0, j))],
        out_specs=pl.BlockSpec((tm, tn), lambda j, i: (i, j)),
        out_shape=jax.ShapeDtypeStruct((M, N), jnp.float32),
        compiler_params=pltpu.CompilerParams(
            dimension_semantics=("arbitrary", "arbitrary"),
            vmem_limit_bytes=VMEM_LIMIT_BYTES),
        name="mm",
    )(x, w)


def mm3(x, w):
    B, T, K = x.shape
    return mm(x.reshape(B * T, K), w).reshape(B, T, w.shape[1])


RWKV_CHUNK = 64
RWKV_HEADS_PER_STEP = 8

_NN = (((2,), (1,)), ((0,), (0,)))
_NT = (((2,), (2,)), ((0,), (0,)))
_TN = (((1,), (1,)), ((0,), (0,)))


def _split2(x):
    hi = x.astype(jnp.bfloat16)
    lo = (x - hi.astype(jnp.float32)).astype(jnp.bfloat16)
    return hi, lo


def _dot3(a, b, dims=_NN):
    f = lambda x, y: lax.dot_general(x, y, dims, preferred_element_type=jnp.float32)
    return f(a[0], b[0]) + (f(a[0], b[1]) + f(a[1], b[0]))


def _rwkv_chunk_kernel(r_ref, v_ref, a_ref, lw_ref, k_ref, b_ref, s0_ref, y_ref, sfin_ref, st_ref):
    d = pl.program_id(0)
    c = pl.program_id(3)
    n_chunks = pl.num_programs(3)
    HB, C, _ = r_ref.shape

    @pl.when(c == 0)
    def _():
        st_ref[...] = s0_ref[...]

    ti = lax.broadcasted_iota(jnp.int32, (HB, C, C), 1)
    si = lax.broadcasted_iota(jnp.int32, (HB, C, C), 2)
    diff = (si - ti) * (1 - 2 * d)
    m_incl = diff <= 0
    m_strict = diff < 0
    m_incl_bf = jnp.where(m_incl, 1.0, 0.0).astype(jnp.bfloat16)
    eye = jnp.where(diff == 0, 1.0, 0.0).astype(jnp.float32)

    r, v, a = r_ref[...], v_ref[...], a_ref[...]
    lw, k, b = lw_ref[...], k_ref[...], b_ref[...]
    st = st_ref[...]

    lw_hi, lw_lo = _split2(lw)
    lw_lo2 = (lw - lw_hi.astype(jnp.float32) - lw_lo.astype(jnp.float32)).astype(jnp.bfloat16)
    f = lambda y: lax.dot_general(m_incl_bf, y, _NN, preferred_element_type=jnp.float32)
    cum = f(lw_hi) + (f(lw_lo) + f(lw_lo2))
    total = jnp.sum(lw, axis=1, keepdims=True)
    e_cum = jnp.exp(cum)
    e_inv = jnp.exp(-cum)
    e_tot = jnp.exp(total)
    at = _split2(a * jnp.exp(cum - lw))
    rt = _split2(r * e_cum)
    bt_f = b * e_inv
    kt_f = k * e_inv
    bt, kt = _split2(bt_f), _split2(kt_f)
    bh, kh = _split2(bt_f * e_tot), _split2(kt_f * e_tot)
    vs = _split2(v)

    l_ab = jnp.where(m_strict, _dot3(at, bt, _NT), 0.0)
    l_ak = jnp.where(m_strict, _dot3(at, kt, _NT), 0.0)
    m_rb = jnp.where(m_incl, _dot3(rt, bt, _NT), 0.0)
    m_rk = jnp.where(m_incl, _dot3(rt, kt, _NT), 0.0)

    tm = eye + l_ab
    lp = _split2(l_ab)
    for _ in range(int(math.log2(C)) - 1):
        lp = _split2(_dot3(lp, lp))
        tm = tm + _dot3(_split2(tm), lp)
    tms = _split2(tm)
    p = _split2(_dot3(tms, at))
    q = _dot3(tms, _split2(_dot3(_split2(l_ak), vs)))

    sts = _split2(st)
    u = _split2(_dot3(p, sts) + q)
    y_ref[...] = _dot3(rt, sts) + _dot3(_split2(m_rb), u) + _dot3(_split2(m_rk), vs)
    st_ref[...] = jnp.swapaxes(e_tot, 1, 2) * st + _dot3(bh, u, _TN) + _dot3(kh, vs, _TN)

    @pl.when(c == n_chunks - 1)
    def _():
        sfin_ref[...] = st_ref[...]


def rwkv7_chunked(r, v, a, lw, k, b, s0t):
    D, B, H, T, N = lw.shape
    C, HB = RWKV_CHUNK, RWKV_HEADS_PER_STEP
    assert T % C == 0 and H % HB == 0
    n_chunks = T // C
    chunk = lambda d, c: c + d * (n_chunks - 1 - 2 * c)
    shared = pl.BlockSpec((None, HB, C, N), lambda d, bb, g, c: (bb, g, chunk(d, c), 0))
    per_dir = pl.BlockSpec((None, None, HB, C, N), lambda d, bb, g, c: (d, bb, g, chunk(d, c), 0))
    state = pl.BlockSpec((None, None, HB, N, N), lambda d, bb, g, c: (d, bb, g, 0, 0))
    return pl.pallas_call(
        _rwkv_chunk_kernel,
        grid=(D, B, H // HB, n_chunks),
        in_specs=[shared, shared, shared, per_dir, per_dir, per_dir, state],
        out_specs=[per_dir, state],
        out_shape=[jax.ShapeDtypeStruct((D, B, H, T, N), jnp.float32),
                   jax.ShapeDtypeStruct((D, B, H, N, N), jnp.float32)],
        scratch_shapes=[pltpu.VMEM((HB, N, N), jnp.float32)],
        compiler_params=pltpu.CompilerParams(
            dimension_semantics=("arbitrary", "arbitrary", "arbitrary", "arbitrary"),
            vmem_limit_bytes=VMEM_LIMIT_BYTES),
        name="rwkv7_chunked",
    )(r, v, a, lw, k, b, s0t)


PEER_ROUTE_TOKENS = 256
PEER_TOKENS = 512
PEER_E1_PER_STEP = 8


def _top_rows(s_ref, n_rows, k, emit):
    for a in range(k):
        s = s_ref[0:n_rows, :]
        mx = jnp.max(s, axis=0, keepdims=True)
        emit(a, mx)
        if a + 1 < k:
            s_ref[0:n_rows, :] = jnp.where(s == mx, NEG_INF, s)


def _peer_route_kernel(ht_ref, wqt_ref, k1_ref, k2_ref, s1_ref, s2_ref, g1_ref, e2_ref, th_ref,
                       work_ref, t1_ref, t2_ref, cand_ref):
    half = PEER_QDIM // 2
    qt = jnp.dot(wqt_ref[...], ht_ref[...], preferred_element_type=jnp.float32)
    for h in range(PEER_HEADS):
        for which, (k_ref, s_out, t_ref) in enumerate(((k1_ref, s1_ref, t1_ref), (k2_ref, s2_ref, t2_ref))):
            q = qt[h * PEER_QDIM + which * half: h * PEER_QDIM + (which + 1) * half, :]
            q_hi, q_lo = _split2(q)
            k_hi, k_lo = _split2(k_ref[h])
            f = lambda x, y: jnp.dot(x, y, preferred_element_type=jnp.float32)
            s = f(k_hi, q_hi) + (f(k_hi, q_lo) + f(k_lo, q_hi))
            s_out[h] = s
            work_ref[...] = s

            def emit(a, mx, t_ref=t_ref):
                t_ref[a:a + 1, :] = mx
            _top_rows(work_ref, N_KEYS, PEER_TOPK, emit)
        t1 = t1_ref[...]
        t2 = t2_ref[...]
        m1 = t1[0:1, :]
        m2 = t2[0:1, :]
        for a in range(PEER_TOPK):
            cand_ref[a * PEER_TOPK:(a + 1) * PEER_TOPK, :] = t1[a:a + 1, :] + t2
        acc = {}

        def emit_c(a, mx):
            e = jnp.exp(mx - (m1 + m2))
            acc['z'] = e if a == 0 else acc['z'] + e
            acc['th'] = mx
        _top_rows(cand_ref, PEER_TOPK * PEER_TOPK, PEER_TOPK, emit_c)
        th_ref[h:h + 1, :] = acc['th']
        g1_ref[h] = jnp.exp(s1_ref[h] - m1) / acc['z']
        e2_ref[h] = jnp.exp(s2_ref[h] - m2)


def peer_route(ht, wqt, keys):
    D, N = ht.shape
    tm = PEER_ROUTE_TOKENS
    assert N % tm == 0
    big = jax.ShapeDtypeStruct((PEER_HEADS, N_KEYS, N), jnp.float32)
    big_spec = pl.BlockSpec((PEER_HEADS, N_KEYS, tm), lambda i: (0, 0, i))
    key_spec = pl.BlockSpec((PEER_HEADS, N_KEYS, PEER_QDIM // 2), lambda i: (0, 0, 0))
    return pl.pallas_call(
        _peer_route_kernel,
        grid=(N // tm,),
        in_specs=[pl.BlockSpec((D, tm), lambda i: (0, i)),
                  pl.BlockSpec((PEER_HEADS * PEER_QDIM, D), lambda i: (0, 0)),
                  key_spec, key_spec],
        out_specs=[big_spec, big_spec, big_spec, big_spec, pl.BlockSpec((PEER_HEADS, tm), lambda i: (0, i))],
        out_shape=[big, big, big, big, jax.ShapeDtypeStruct((PEER_HEADS, N), jnp.float32)],
        scratch_shapes=[pltpu.VMEM((N_KEYS, tm), jnp.float32),
                        pltpu.VMEM((PEER_TOPK, tm), jnp.float32),
                        pltpu.VMEM((PEER_TOPK, tm), jnp.float32),
                        pltpu.VMEM((PEER_TOPK * PEER_TOPK, tm), jnp.float32)],
        compiler_params=pltpu.CompilerParams(
            dimension_semantics=("arbitrary",), vmem_limit_bytes=VMEM_LIMIT_BYTES),
        name="peer_route",
    )(ht, wqt, keys[0], keys[1])


def _gelu_tanh(x):
    return 0.5 * x * (1.0 + jnp.tanh(math.sqrt(2.0 / math.pi) * (x + 0.044715 * (x * x * x))))


def _peer_expert_kernel(ht_ref, u_ref, vt_ref, s1_ref, s2_ref, g1_ref, e2_ref, th_ref, o_ref, wg_ref):
    j = pl.program_id(1)
    tm = ht_ref.shape[1]

    @pl.when(j == 0)
    def _():
        o_ref[...] = jnp.zeros_like(o_ref)

    act = _gelu_tanh(jnp.dot(u_ref[...], ht_ref[...], preferred_element_type=jnp.float32))
    e1_rows = pl.ds(pl.multiple_of(j * PEER_E1_PER_STEP, PEER_E1_PER_STEP), PEER_E1_PER_STEP)
    for tc in range(tm // LANE):
        cols = slice(tc * LANE, (tc + 1) * LANE)
        s1_rows = [s1_ref[h, e1_rows, cols] for h in range(PEER_HEADS)]
        g1_rows = [g1_ref[h, e1_rows, cols] for h in range(PEER_HEADS)]
        for e in range(PEER_E1_PER_STEP):
            w = None
            for h in range(PEER_HEADS):
                score = s2_ref[h, :, cols] + s1_rows[h][e:e + 1, :]
                gate = e2_ref[h, :, cols] * g1_rows[h][e:e + 1, :]
                term = jnp.where(score >= th_ref[h:h + 1, cols], gate, 0.0)
                w = term if w is None else w + term
            rows = slice(e * N_KEYS, (e + 1) * N_KEYS)
            wg_ref[rows, cols] = (w * act[rows, cols]).astype(jnp.bfloat16)
    o_ref[...] += jnp.dot(vt_ref[...], wg_ref[...], preferred_element_type=jnp.float32)


def peer_experts(ht, u, vt, s1, s2, g1, e2, th):
    D, N = ht.shape
    E = u.shape[0]
    tm = PEER_TOKENS
    te = PEER_E1_PER_STEP * N_KEYS
    assert N % tm == 0 and E % te == 0
    big_spec = pl.BlockSpec((PEER_HEADS, N_KEYS, tm), lambda i, j: (0, 0, i))
    return pl.pallas_call(
        _peer_expert_kernel,
        grid=(N // tm, E // te),
        in_specs=[pl.BlockSpec((D, tm), lambda i, j: (0, i)),
                  pl.BlockSpec((te, D), lambda i, j: (j, 0)),
                  pl.BlockSpec((D, te), lambda i, j: (0, j)),
                  big_spec, big_spec, big_spec, big_spec,
                  pl.BlockSpec((PEER_HEADS, tm), lambda i, j: (0, i))],
        out_specs=pl.BlockSpec((D, tm), lambda i, j: (0, i)),
        out_shape=jax.ShapeDtypeStruct((D, N), jnp.float32),
        scratch_shapes=[pltpu.VMEM((te, tm), jnp.bfloat16)],
        compiler_params=pltpu.CompilerParams(
            dimension_semantics=("arbitrary", "arbitrary"), vmem_limit_bytes=VMEM_LIMIT_BYTES),
        name="peer_experts",
    )(ht, u, vt, s1, s2, g1, e2, th)


def peer_ffn_tokens(h, wqt, keys, u, vt):
    ht = jnp.transpose(h).astype(jnp.bfloat16)
    s1, s2, g1, e2, th = peer_route(ht, wqt, keys)
    return jnp.transpose(peer_experts(ht, u, vt, s1, s2, g1, e2, th))


ZB_Q = A_COLS // LANE
ZB_K = ZB_Q + B_HEADS
ZB_V = ZB_K + B_KV_HEADS
ZC_Q = ZB_V + B_KV_HEADS
ZC_K = ZC_Q + C_HEADS
ZC_V = ZC_K + C_HEADS
DIFF_Q_ROWS = 256

_DOT_NT = (((1,), (1,)), ((), ()))


def rope_lane_tables(n_tokens, head_dim):
    rows = n_tokens // GRID_W
    row_pos = jnp.repeat(jnp.arange(rows, dtype=jnp.float32), GRID_W)
    col_pos = jnp.tile(jnp.arange(GRID_W, dtype=jnp.float32), rows)
    n_freq = head_dim // 4
    freqs = ROPE_BASE ** (-jnp.arange(n_freq, dtype=jnp.float32) / n_freq)
    ang = jnp.concatenate([row_pos[:, None] * freqs, col_pos[:, None] * freqs], axis=-1)
    cos = jnp.repeat(jnp.cos(ang), 2, axis=-1)
    sin = jnp.repeat(jnp.sin(ang), 2, axis=-1) * jnp.tile(jnp.array([-1.0, 1.0], jnp.float32), head_dim // 2)
    reps = LANE // head_dim
    return jnp.tile(cos, (1, reps)), jnp.tile(sin, (1, reps))


def _rope(x, cos, sin_signed):
    lane = lax.broadcasted_iota(jnp.int32, x.shape, 1)
    partner = jnp.where(lane % 2 == 0, pltpu.roll(x, LANE - 1, axis=1), pltpu.roll(x, 1, axis=1))
    return x * cos + partner * sin_signed


def _bf(x):
    return x.astype(jnp.bfloat16)


def _diff_attn_kernel(lam_ref, q_ref, k_ref, v_ref, *rest, latent, lam_init):
    if latent:
        kc_ref, vc_ref, cq_ref, sq_ref, ck_ref, sk_ref, g_ref, o_ref, kr_ref = rest
    else:
        g_ref, o_ref, kr_ref = rest
    i = pl.program_id(2)

    @pl.when(i == 0)
    def _():
        k = k_ref[...]
        kr_ref[...] = _bf(_rope(k, ck_ref[...], sk_ref[...]) if latent else k)

    q = q_ref[...]
    if latent:
        q = _rope(q, cq_ref[...], sq_ref[...])
    lane = lax.broadcasted_iota(jnp.int32, q.shape, 1)
    scale = C_HEAD_DIM ** -0.5
    lam = lam_ref[0]
    kr = kr_ref[...]
    vb = _bf(v_ref[...])
    if latent:
        kcb, vcb = _bf(kc_ref[...]), _bf(vc_ref[...])

    def softmax_parts(qh):
        s = lax.dot_general(qh, kr, _DOT_NT, preferred_element_type=jnp.float32) * scale
        m = jnp.max(s, axis=-1, keepdims=True)
        if latent:
            sc = lax.dot_general(qh, kcb, _DOT_NT, preferred_element_type=jnp.float32) * scale
            m = jnp.maximum(m, jnp.max(sc, axis=-1, keepdims=True))
            ec = jnp.exp(sc - m)
        e = jnp.exp(s - m)
        z = jnp.sum(e, axis=-1, keepdims=True)
        if latent:
            z = z + jnp.sum(ec, axis=-1, keepdims=True)
            return e, ec, 1.0 / z
        return e, None, 1.0 / z

    e1, ec1, r1 = softmax_parts(_bf(jnp.where(lane < C_HEAD_DIM, q, 0.0)))
    e2, ec2, r2 = softmax_parts(_bf(jnp.where(lane >= C_HEAD_DIM, q, 0.0)))
    r2 = lam * r2
    o = jnp.dot(_bf(e1 * r1 - e2 * r2), vb, preferred_element_type=jnp.float32)
    if latent:
        o = o + jnp.dot(_bf(ec1 * r1 - ec2 * r2), vcb, preferred_element_type=jnp.float32)
    o = o * lax.rsqrt(jnp.mean(o * o, axis=-1, keepdims=True) + 1e-6) * g_ref[...] * (1.0 - lam_init)
    o_ref[...] = o


def diff_attention_mixer(z, lam, lam_init, subln_g, cache=None, layer=None, rope=None):
    B, T, _ = z.shape
    latent = cache is not None
    tq = min(DIFF_Q_ROWS, T)
    head = lambda base: pl.BlockSpec((None, tq, LANE), lambda b, h, i: (b, i, base + h))
    whole = lambda base: pl.BlockSpec((None, T, LANE), lambda b, h, i: (b, 0, base + h))
    in_specs = [pl.BlockSpec(memory_space=pltpu.SMEM), head(ZC_Q), whole(ZC_K), whole(ZC_V)]
    args = [lam.reshape(1), z, z, z]
    if latent:
        P = cache[0].shape[3]
        cspec = pl.BlockSpec((None, None, None, P, LANE), lambda b, h, i: (b, layer, h, 0, 0))
        tq_spec = pl.BlockSpec((tq, LANE), lambda b, h, i: (i, 0))
        tk_spec = pl.BlockSpec((T, LANE), lambda b, h, i: (0, 0))
        in_specs += [cspec, cspec, tq_spec, tq_spec, tk_spec, tk_spec]
        args += [cache[0], cache[1], rope[0], rope[1], rope[0], rope[1]]
    in_specs.append(pl.BlockSpec((1, LANE), lambda b, h, i: (0, 0)))
    args.append(subln_g.reshape(1, LANE))
    return pl.pallas_call(
        partial(_diff_attn_kernel, latent=latent, lam_init=lam_init),
        grid=(B, C_HEADS, T // tq),
        in_specs=in_specs,
        out_specs=pl.BlockSpec((None, tq, LANE), lambda b, h, i: (b, i, h)),
        out_shape=jax.ShapeDtypeStruct((B, T, C_HEADS * LANE), jnp.float32),
        scratch_shapes=[pltpu.VMEM((T, LANE), jnp.bfloat16)],
        compiler_params=pltpu.CompilerParams(
            dimension_semantics=("arbitrary", "arbitrary", "arbitrary"), vmem_limit_bytes=VMEM_LIMIT_BYTES),
        name="diff_attention",
    )(*args)


def _gqa_kernel(sink_ref, q0_ref, q1_ref, *rest, latent, n_tokens):
    if latent:
        (kp_ref, kn_ref, kx_ref, vp_ref, vn_ref, vx_ref, kc_ref, vc_ref,
         cq_ref, sq_ref, cp_ref, sp_ref, cx_ref, sx_ref, o_ref) = rest
    else:
        kc_ref, vc_ref, o_ref = rest
    kvh = pl.program_id(1)
    n = pl.program_id(2)
    scale = B_HEAD_DIM ** -0.5
    q0, q1 = q0_ref[...], q1_ref[...]
    if latent:
        q0 = _rope(q0, cq_ref[...], sq_ref[...])
        q1 = _rope(q1, cq_ref[...], sq_ref[...])
    q = _bf(jnp.concatenate([q0, q1], axis=0))
    row = lax.broadcasted_iota(jnp.int32, (2 * BLOCK, 1), 0)
    sink = jnp.where(row < BLOCK, sink_ref[2 * kvh], sink_ref[2 * kvh + 1])

    s_c = lax.dot_general(q, _bf(kc_ref[...]), _DOT_NT, preferred_element_type=jnp.float32) * scale
    m = jnp.maximum(jnp.max(s_c, axis=-1, keepdims=True), sink)
    if latent:
        kw = jnp.concatenate([_rope(kp_ref[...], cp_ref[...], sp_ref[...]),
                              _rope(kn_ref[...], cq_ref[...], sq_ref[...]),
                              _rope(kx_ref[...], cx_ref[...], sx_ref[...])], axis=0)
        vw = jnp.concatenate([vp_ref[...], vn_ref[...], vx_ref[...]], axis=0)
        s_w = lax.dot_general(q, _bf(kw), _DOT_NT, preferred_element_type=jnp.float32) * scale
        qpos = n * BLOCK + lax.broadcasted_iota(jnp.int32, s_w.shape, 0) % BLOCK
        kpos = (n - 1) * BLOCK + lax.broadcasted_iota(jnp.int32, s_w.shape, 1)
        valid = (jnp.abs(qpos - kpos) <= WINDOW) & (kpos >= 0) & (kpos < n_tokens)
        s_w = jnp.where(valid, s_w, NEG_INF)
        m = jnp.maximum(m, jnp.max(s_w, axis=-1, keepdims=True))
        e_w = jnp.exp(s_w - m)
    e_c = jnp.exp(s_c - m)
    zsum = jnp.sum(e_c, axis=-1, keepdims=True) + jnp.exp(sink - m)
    o = jnp.dot(_bf(e_c), _bf(vc_ref[...]), preferred_element_type=jnp.float32)
    if latent:
        zsum = zsum + jnp.sum(e_w, axis=-1, keepdims=True)
        o = o + jnp.dot(_bf(e_w), _bf(vw), preferred_element_type=jnp.float32)
    o = o / zsum
    o_ref[:, 0:LANE] = o[0:BLOCK]
    o_ref[:, LANE:2 * LANE] = o[BLOCK:2 * BLOCK]


def gqa_mixer(z, sink, cache=None, layer=None, rope=None):
    B, T, _ = z.shape
    latent = cache is not None
    nb = T // BLOCK
    blk = lambda base, off: pl.BlockSpec(
        (None, BLOCK, LANE), lambda b, kvh, n: (b, jnp.clip(n + off, 0, nb - 1), base + kvh))
    qspec = lambda g: pl.BlockSpec((None, BLOCK, LANE), lambda b, kvh, n: (b, n, ZB_Q + 2 * kvh + g))
    in_specs = [pl.BlockSpec(memory_space=pltpu.SMEM), qspec(0), qspec(1)]
    args = [sink, z, z]
    if latent:
        P = cache[0].shape[3]
        cspec = pl.BlockSpec((None, None, None, P, LANE), lambda b, kvh, n: (b, layer, kvh, 0, 0))
        tab = lambda off: pl.BlockSpec((BLOCK, LANE), lambda b, kvh, n: (jnp.clip(n + off, 0, nb - 1), 0))
        in_specs += [blk(ZB_K, -1), blk(ZB_K, 0), blk(ZB_K, 1), blk(ZB_V, -1), blk(ZB_V, 0), blk(ZB_V, 1),
                     cspec, cspec, tab(0), tab(0), tab(-1), tab(-1), tab(1), tab(1)]
        args += [z] * 6 + [cache[0], cache[1]] + [rope[0], rope[1]] * 3
    else:
        whole = lambda base: pl.BlockSpec((None, T, LANE), lambda b, kvh, n: (b, 0, base + kvh))
        in_specs += [whole(ZB_K), whole(ZB_V)]
        args += [z, z]
    return pl.pallas_call(
        partial(_gqa_kernel, latent=latent, n_tokens=T),
        grid=(B, B_KV_HEADS, nb),
        in_specs=in_specs,
        out_specs=pl.BlockSpec((None, BLOCK, 2 * LANE), lambda b, kvh, n: (b, n, kvh)),
        out_shape=jax.ShapeDtypeStruct((B, T, B_HEADS * LANE), jnp.float32),
        compiler_params=pltpu.CompilerParams(
            dimension_semantics=("arbitrary", "arbitrary", "arbitrary"), vmem_limit_bytes=VMEM_LIMIT_BYTES),
        name="gqa_attention",
    )(*args)


def split_cols(z, sizes):
    return jnp.split(z, [int(s) for s in np.cumsum(sizes)[:-1]], axis=-1)


def layer_norm(x, g=None, b=None):
    mu = jnp.mean(x, axis=-1, keepdims=True)
    var = jnp.mean(jnp.square(x - mu), axis=-1, keepdims=True)
    y = (x - mu) * lax.rsqrt(var + LN_EPS)
    if g is not None:
        y = y * g + b
    return y


def centred_conv3(x, w):
    xp = jnp.pad(x, ((0, 0), (1, 1), (0, 0)))
    return xp[:, :-2] * w[0] + xp[:, 1:-1] * w[1] + xp[:, 2:] * w[2]


def rwkv7_mixer(za, S0_f, S0_b, lp):
    Bsz, T, _ = za.shape
    r, k, v, g_d, wd_f, wd_b, ad_f, ad_b = split_cols(
        za, [A_WIDTH, A_WIDTH, A_WIDTH, G_LORA, W_LORA, W_LORA, ICL_LORA, ICL_LORA])
    heads = lambda t: t.reshape(Bsz, T, A_HEADS, A_HEAD_DIM)
    by_head = lambda t: jnp.transpose(t, (0, 2, 1, 3))
    rh, vh = heads(r), heads(v)
    kk = heads(k * lp['a_k_k'])
    kk = kk * lax.rsqrt(jnp.sum(kk * kk, axis=-1, keepdims=True) + 1e-12)
    lws, kds, bs = [], [], []
    for d, (wd, ad) in enumerate(((wd_f, ad_f), (wd_b, ad_b))):
        wz = lp['a_w0'][d] + mm3(jnp.tanh(wd), lp['a_w_up'][d])
        lws.append(heads(-math.exp(-0.5) * jax.nn.sigmoid(wz)))
        icl = jax.nn.sigmoid(lp['a_a0'][d] + mm3(ad, lp['a_a_up'][d]))
        kds.append(heads(k * (1.0 + (icl - 1.0) * lp['a_k_a'])))
        bs.append(kk * heads(icl))
    stack = lambda ts: jnp.stack([by_head(t) for t in ts], axis=0)
    s0t = jnp.swapaxes(jnp.stack([S0_f, S0_b], axis=0), -1, -2)
    ys, sfin_t = rwkv7_chunked(by_head(rh), by_head(vh), by_head(-kk), stack(lws), stack(kds), stack(bs), s0t)
    states = jnp.swapaxes(sfin_t, -1, -2)
    y = jnp.transpose(ys[0] + ys[1], (0, 2, 1, 3))
    mu = jnp.mean(y, axis=-1, keepdims=True)
    var = jnp.mean(jnp.square(y - mu), axis=-1, keepdims=True)
    y = ((y - mu) * lax.rsqrt(var + A_GN_EPS)).reshape(Bsz, T, A_WIDTH) * lp['a_gn_g'] + lp['a_gn_b']
    bonus = jnp.sum(rh * (kds[0] + kds[1]) * lp['a_r_k'], axis=-1, keepdims=True) * vh
    g = mm3(jax.nn.sigmoid(g_d), lp['a_g_up'])
    out = (y + bonus.reshape(Bsz, T, A_WIDTH)) * g
    return out, states[0], states[1]


def diff_lambda(lv, layer):
    lam_init = 0.8 - 0.6 * math.exp(-0.3 * layer)
    lam = jnp.exp(jnp.sum(lv[0] * lv[1])) - jnp.exp(jnp.sum(lv[2] * lv[3])) + lam_init
    return lam, lam_init


def adaln_params(cond, lp):
    mod = mm(jax.nn.silu(cond), lp['w_mod']) + lp['b_mod']
    return jnp.split(mod[:, None, :], 6, axis=-1)


def mixer_inputs(x, shift, scale, lp):
    h = layer_norm(x) * (1.0 + scale) + shift
    z = mm3(h, lp['w_in'])
    zA = z[..., :A_COLS]
    zG = z[..., IN_COLS - 3 * D_MODEL:]
    return z, centred_conv3(zA, lp['a_conv']), zG


def merge_mixers(x, yA, yB, yC, zG, gate1, shift2, scale2, lp):
    gA, gB, gC = jnp.split(jax.nn.sigmoid(zG), 3, axis=-1)
    merged = gA * mm3(yA, lp['p_a']) + gB * mm3(yB, lp['p_b']) + gC * mm3(yC, lp['p_c'])
    x = layer_norm(DEEPNORM_ALPHA * x + gate1 * mm3(merged, lp['w_out']), lp['ln1_g'], lp['ln1_b'])
    h = layer_norm(x) * (1.0 + scale2) + shift2
    return x, h


def peer_and_norm(parts, lp):
    sizes = [x.shape[0] * x.shape[1] for x, _, _ in parts]
    h_all = jnp.concatenate([h.reshape(-1, D_MODEL) for _, h, _ in parts], axis=0)
    y_all = peer_ffn_tokens(h_all, lp['peer_wqt'], lp['peer_keys'], lp['peer_u'], lp['peer_vt'])
    outs, start = [], 0
    for (x, _, gate2), n in zip(parts, sizes):
        y = y_all[start:start + n].reshape(x.shape)
        start += n
        outs.append(layer_norm(DEEPNORM_ALPHA * x + gate2 * y, lp['ln2_g'], lp['ln2_b']))
    return outs


def context_layer(x, cond, lp, layer):
    Bsz, L, _ = x.shape
    shift1, scale1, gate1, shift2, scale2, gate2 = adaln_params(cond, lp)
    z, zA, zG = mixer_inputs(x, shift1, scale1, lp)
    S0 = jnp.zeros((Bsz, A_HEADS, A_HEAD_DIM, A_HEAD_DIM), jnp.float32)
    yA, S_f, S_b = rwkv7_mixer(zA, S0, S0, lp)
    yB = gqa_mixer(z, lp['b_sink'])
    lam, lam_init = diff_lambda(lp['c_lam'], layer)
    yC = diff_attention_mixer(z, lam, lam_init, lp['c_subln'])
    x_mid, h_mid = merge_mixers(x, yA, yB, yC, zG, gate1, shift2, scale2, lp)

    def cache_layout(first_block, n_heads):
        t = z[..., first_block * LANE:(first_block + n_heads) * LANE].reshape(Bsz, L, n_heads, LANE)
        return jnp.transpose(t, (0, 2, 1, 3))
    ctx = (S_f, S_b, cache_layout(ZB_K, B_KV_HEADS), cache_layout(ZB_V, B_KV_HEADS),
           cache_layout(ZC_K, C_HEADS), cache_layout(ZC_V, C_HEADS))
    return (x_mid, h_mid, gate2), ctx


def latent_layer(x, cond, lp, layer, S_f0, S_b0, cache_b, cache_c, rope_b, rope_c):
    shift1, scale1, gate1, shift2, scale2, gate2 = adaln_params(cond, lp)
    z, zA, zG = mixer_inputs(x, shift1, scale1, lp)
    yA, _, _ = rwkv7_mixer(zA, S_f0, S_b0, lp)
    yB = gqa_mixer(z, lp['b_sink'], cache=cache_b, layer=layer, rope=rope_b)
    lam, lam_init = diff_lambda(lp['c_lam'], layer)
    yC = diff_attention_mixer(z, lam, lam_init, lp['c_subln'], cache=cache_c, layer=layer, rope=rope_c)
    x_mid, h_mid = merge_mixers(x, yA, yB, yC, zG, gate1, shift2, scale2, lp)
    return x_mid, h_mid, gate2


_BF16_WEIGHTS = ('w_mod', 'w_in', 'a_w_up', 'a_a_up', 'a_g_up', 'p_a', 'p_b', 'p_c', 'w_out', 'peer_u')


def kernel(x_prompt, x_sample, state_a_fwd, state_a_bwd, cache_b_k, cache_b_v, cache_c_k, cache_c_v, c, c_ctx, w_mod, b_mod, w_in, a_conv, a_w0, a_w_up, a_a0, a_a_up, a_g_up, a_k_k, a_k_a, a_r_k, a_gn_g, a_gn_b, b_sink, c_lam, c_subln, p_a, p_b, p_c, w_out, ln1_g, ln1_b, ln2_g, ln2_b, peer_wq, peer_keys, peer_u, peer_v):
    params = dict(w_mod=w_mod, b_mod=b_mod, w_in=w_in, a_conv=a_conv, a_w0=a_w0, a_w_up=a_w_up, a_a0=a_a0,
                  a_a_up=a_a_up, a_g_up=a_g_up, a_k_k=a_k_k, a_k_a=a_k_a, a_r_k=a_r_k, a_gn_g=a_gn_g,
                  a_gn_b=a_gn_b, b_sink=b_sink, c_lam=c_lam, c_subln=c_subln, p_a=p_a, p_b=p_b, p_c=p_c,
                  w_out=w_out, ln1_g=ln1_g, ln1_b=ln1_b, ln2_g=ln2_g, ln2_b=ln2_b, peer_wq=peer_wq,
                  peer_keys=peer_keys, peer_u=peer_u, peer_v=peer_v)
    for name in _BF16_WEIGHTS:
        params[name] = params[name].astype(jnp.bfloat16)
    params['peer_wqt'] = jnp.swapaxes(params.pop('peer_wq'), 1, 2).astype(jnp.bfloat16)
    params['peer_vt'] = jnp.swapaxes(params.pop('peer_v'), 1, 2).astype(jnp.bfloat16)
    n_lat = x_sample.shape[1]
    rope_b = rope_lane_tables(n_lat, B_HEAD_DIM)
    rope_c = rope_lane_tables(n_lat, C_HEAD_DIM)
    cond_ctx = jnp.broadcast_to(c_ctx[None, :], (x_prompt.shape[0], D_MODEL))
    y_prompt, y_sample = x_prompt, x_sample
    new = [[], [], [], [], [], []]
    for layer in range(DEPTH):
        lp = {name: val[layer] for name, val in params.items()}
        part_ctx, ctx = context_layer(y_prompt, cond_ctx, lp, layer)
        for lst, t in zip(new, ctx):
            lst.append(t)
        part_lat = latent_layer(y_sample, c, lp, layer,
                                state_a_fwd[:, layer], state_a_bwd[:, layer],
                                (cache_b_k, cache_b_v), (cache_c_k, cache_c_v), rope_b, rope_c)
        y_prompt, y_sample = peer_and_norm([part_ctx, part_lat], lp)
    return (y_prompt, y_sample) + tuple(jnp.stack(lst, axis=1) for lst in new)
```

```python
import math
from functools import partial

import jax
import jax.numpy as jnp
import numpy as np
from jax import lax
from jax.experimental import pallas as pl
from jax.experimental.pallas import tpu as pltpu

D_MODEL = 2048
DEPTH = 4
GRID_W = 64
BLOCK = 128
A_HEADS = 16
A_HEAD_DIM = 64
A_WIDTH = A_HEADS * A_HEAD_DIM
W_LORA = 64
ICL_LORA = 64
G_LORA = 128
A_COLS = 3 * A_WIDTH + G_LORA + 2 * W_LORA + 2 * ICL_LORA
A_GN_EPS = 64e-5
B_HEADS = 4
B_KV_HEADS = 2
B_HEAD_DIM = 128
B_WIDTH = B_HEADS * B_HEAD_DIM
B_KV_WIDTH = B_KV_HEADS * B_HEAD_DIM
WINDOW = 128
C_HEADS = 4
C_HEAD_DIM = 64
C_WIDTH = C_HEADS * 2 * C_HEAD_DIM
IN_COLS = A_COLS + B_WIDTH + 2 * B_KV_WIDTH + 3 * C_WIDTH + 3 * D_MODEL
PEER_HEADS = 8
N_KEYS = 128
PEER_QDIM = 256
PEER_TOPK = 16
ROPE_BASE = 10000.0
LN_EPS = 1e-5
NEG_INF = -1e30
DEEPNORM_ALPHA = (2 * DEPTH) ** 0.25

LANE = 128
VMEM_LIMIT_BYTES = 56 * 1024 * 1024


def _mm_kernel(x_ref, w_ref, o_ref):
    o_ref[...] = jnp.dot(x_ref[...].astype(jnp.bfloat16), w_ref[...],
                         preferred_element_type=jnp.float32)


def _pick_tile(n, cap, unit):
    if n <= cap:
        return n
    best = None
    for t in range(unit, cap + 1, unit):
        if n % t == 0:
            best = t
    assert best is not None, (n, cap, unit)
    return best


def mm(x, w):
    M, K = x.shape
    N = w.shape[1]
    tm = _pick_tile(M, 512, 8)
    tn = _pick_tile(N, 1024, LANE)
    return pl.pallas_call(
        _mm_kernel,
        grid=(N // tn, M // tm),
        in_specs=[pl.BlockSpec((tm, K), lambda j, i: (i, 0)),
                  pl.BlockSpec((K, tn), lambda j, i: (0, j))],
        out_specs=pl.BlockSpec((tm, tn), lambda j, i: (i, j)),
        out_shape=jax.ShapeDtypeStruct((M, N), jnp.float32),
        compiler_params=pltpu.CompilerParams(
            dimension_semantics=("arbitrary", "arbitrary"),
            vmem_limit_bytes=VMEM_LIMIT_BYTES),
        name="mm",
    )(x, w)


def mm3(x, w):
    B, T, K = x.shape
    return mm(x.reshape(B * T, K), w).reshape(B, T, w.shape[1])


RWKV_CHUNK = 64
RWKV_HEADS_PER_STEP = 8

_NN = (((2,), (1,)), ((0,), (0,)))
_NT = (((2,), (2,)), ((0,), (0,)))
_TN = (((1,), (1,)), ((0,), (0,)))


def _split2(x):
    hi = x.astype(jnp.bfloat16)
    lo = (x - hi.astype(jnp.float32)).astype(jnp.bfloat16)
    return hi, lo


def _dot3(a, b, dims=_NN):
    f = lambda x, y: lax.dot_general(x, y, dims, preferred_element_type=jnp.float32)
    return f(a[0], b[0]) + (f(a[0], b[1]) + f(a[1], b[0]))


def _dot1(a, b, dims=_NN):
    return lax.dot_general(a[0], b[0], dims, preferred_element_type=jnp.float32)


def _split_heads(x):
    n = A_HEAD_DIM
    return jnp.stack([x[:, h * n:(h + 1) * n] for h in range(x.shape[1] // n)], axis=0)


def _rwkv_chunk_kernel(zr_ref, zk_ref, zv_ref, wz_ref, az_ref, w0_ref, a0_ref, kk_ref, ka_ref, s0_ref,
                       y_ref, sfin_ref, st_ref):
    d = pl.program_id(0)
    c = pl.program_id(3)
    n_chunks = pl.num_programs(3)
    HB = st_ref.shape[0]
    C = zr_ref.shape[0]

    @pl.when(c == 0)
    def _():
        st_ref[...] = s0_ref[...]

    k_all = zk_ref[...]
    lw_all = -math.exp(-0.5) * jax.nn.sigmoid(wz_ref[...] + w0_ref[...])
    icl_all = jax.nn.sigmoid(az_ref[...] + a0_ref[...])
    r, v = _split_heads(zr_ref[...]), _split_heads(zv_ref[...])
    lw = _split_heads(lw_all)
    k = _split_heads(k_all * (1.0 + (icl_all - 1.0) * ka_ref[...]))
    kk = _split_heads(k_all * kk_ref[...])
    kk = kk * lax.rsqrt(jnp.sum(kk * kk, axis=-1, keepdims=True) + 1e-12)
    a = -kk
    b = kk * _split_heads(icl_all)

    ti = lax.broadcasted_iota(jnp.int32, (HB, C, C), 1)
    si = lax.broadcasted_iota(jnp.int32, (HB, C, C), 2)
    diff = (si - ti) * (1 - 2 * d)
    m_incl = diff <= 0
    m_strict = diff < 0
    m_incl_bf = jnp.where(m_incl, 1.0, 0.0).astype(jnp.bfloat16)
    eye = jnp.where(diff == 0, 1.0, 0.0).astype(jnp.float32)

    st = st_ref[...]

    lw_hi, lw_lo = _split2(lw)
    lw_lo2 = (lw - lw_hi.astype(jnp.float32) - lw_lo.astype(jnp.float32)).astype(jnp.bfloat16)
    f = lambda y: lax.dot_general(m_incl_bf, y, _NN, preferred_element_type=jnp.float32)
    cum = f(lw_hi) + (f(lw_lo) + f(lw_lo2))
    total = jnp.sum(lw, axis=1, keepdims=True)
    e_cum = jnp.exp(cum)
    e_inv = jnp.exp(-cum)
    e_tot = jnp.exp(total)
    hi = lambda x: (x.astype(jnp.bfloat16),)
    at = hi(a * jnp.exp(cum - lw))
    rt = _split2(r * e_cum)
    bt_f = b * e_inv
    kt_f = k * e_inv
    bt, kt = hi(bt_f), hi(kt_f)
    bh, kh = _split2(bt_f * e_tot), _split2(kt_f * e_tot)
    vs = _split2(v)

    l_ab = jnp.where(m_strict, _dot1(at, bt, _NT), 0.0)
    l_ak = jnp.where(m_strict, _dot1(at, kt, _NT), 0.0)
    m_rb = jnp.where(m_incl, _dot1(rt, bt, _NT), 0.0)
    m_rk = jnp.where(m_incl, _dot1(rt, kt, _NT), 0.0)

    tm = eye + l_ab
    lp = hi(l_ab)
    for _ in range(int(math.log2(C)) - 1):
        lp = hi(_dot1(lp, lp))
        tm = tm + _dot1(hi(tm), lp)
    tms = hi(tm)
    p = _split2(_dot1(tms, at))
    q = _dot1(tms, hi(_dot1(hi(l_ak), vs)))

    sts = _split2(st)
    u = _split2(_dot3(p, sts) + q)
    y = _dot3(rt, sts) + _dot3(_split2(m_rb), u) + _dot3(_split2(m_rk), vs)
    y_ref[...] = jnp.concatenate([y[h] for h in range(HB)], axis=-1)
    st_ref[...] = jnp.swapaxes(e_tot, 1, 2) * st + _dot3(bh, u, _TN) + _dot3(kh, vs, _TN)

    @pl.when(c == n_chunks - 1)
    def _():
        sfin_ref[...] = st_ref[...]


def rwkv7_chunked(zc, wz, az, w0, a0, k_k, k_a, s0t):
    D, B, T, W = wz.shape
    N, H = A_HEAD_DIM, A_HEADS
    C, HB = RWKV_CHUNK, RWKV_HEADS_PER_STEP
    G = H // HB
    lanes = HB * N
    assert T % C == 0 and H % HB == 0 and W == H * N
    n_chunks = T // C
    chunk = lambda d, c: c + d * (n_chunks - 1 - 2 * c)
    zcol = lambda base: pl.BlockSpec((None, C, lanes), lambda d, bb, g, c: (bb, chunk(d, c), base * G + g))
    per_dir = pl.BlockSpec((None, None, C, lanes), lambda d, bb, g, c: (d, bb, chunk(d, c), g))
    dir_row = pl.BlockSpec((None, 1, lanes), lambda d, bb, g, c: (d, 0, g))
    row = pl.BlockSpec((1, lanes), lambda d, bb, g, c: (0, g))
    state = pl.BlockSpec((None, None, HB, N, N), lambda d, bb, g, c: (d, bb, g, 0, 0))
    return pl.pallas_call(
        _rwkv_chunk_kernel,
        grid=(D, B, G, n_chunks),
        in_specs=[zcol(0), zcol(1), zcol(2), per_dir, per_dir, dir_row, dir_row, row, row, state],
        out_specs=[per_dir, state],
        out_shape=[jax.ShapeDtypeStruct((D, B, T, W), jnp.float32),
                   jax.ShapeDtypeStruct((D, B, H, N, N), jnp.float32)],
        scratch_shapes=[pltpu.VMEM((HB, N, N), jnp.float32)],
        compiler_params=pltpu.CompilerParams(
            dimension_semantics=("arbitrary", "arbitrary", "arbitrary", "arbitrary"),
            vmem_limit_bytes=VMEM_LIMIT_BYTES),
        name="rwkv7_chunked",
    )(zc, zc, zc, wz, az, w0.reshape(D, 1, W), a0.reshape(D, 1, W), k_k.reshape(1, W), k_a.reshape(1, W), s0t)


PEER_ROUTE_TOKENS = 256
PEER_TOKENS = 512
PEER_E1_PER_STEP = 8
PEER_E1_PER_PART = 2


def _top_rows(s_ref, n_rows, k, emit):
    for a in range(k):
        s = s_ref[0:n_rows, :]
        mx = jnp.max(s, axis=0, keepdims=True)
        emit(a, mx)
        if a + 1 < k:
            s_ref[0:n_rows, :] = jnp.where(s == mx, NEG_INF, s)


def _peer_route_kernel(ht_ref, wqt_ref, k1_ref, k2_ref, s1_ref, s2_ref, g1_ref, e2_ref, th_ref,
                       work_ref, t1_ref, t2_ref, cand_ref):
    half = PEER_QDIM // 2
    qt = jnp.dot(wqt_ref[...], ht_ref[...], preferred_element_type=jnp.float32)
    for h in range(PEER_HEADS):
        for which, (k_ref, s_out, t_ref) in enumerate(((k1_ref, s1_ref, t1_ref), (k2_ref, s2_ref, t2_ref))):
            q = qt[h * PEER_QDIM + which * half: h * PEER_QDIM + (which + 1) * half, :]
            q_hi, q_lo = _split2(q)
            k_hi, k_lo = _split2(k_ref[h])
            f = lambda x, y: jnp.dot(x, y, preferred_element_type=jnp.float32)
            s = f(k_hi, q_hi) + (f(k_hi, q_lo) + f(k_lo, q_hi))
            s_out[h] = s
            work_ref[...] = s

            def emit(a, mx, t_ref=t_ref):
                t_ref[a:a + 1, :] = mx
            _top_rows(work_ref, N_KEYS, PEER_TOPK, emit)
        t1 = t1_ref[...]
        t2 = t2_ref[...]
        m1 = t1[0:1, :]
        m2 = t2[0:1, :]
        for a in range(PEER_TOPK):
            cand_ref[a * PEER_TOPK:(a + 1) * PEER_TOPK, :] = t1[a:a + 1, :] + t2
        acc = {}

        def emit_c(a, mx):
            e = jnp.exp(mx - (m1 + m2))
            acc['z'] = e if a == 0 else acc['z'] + e
            acc['th'] = mx
        _top_rows(cand_ref, PEER_TOPK * PEER_TOPK, PEER_TOPK, emit_c)
        th_ref[h:h + 1, :] = acc['th']
        g1_ref[h] = jnp.exp(s1_ref[h] - m1) / acc['z']
        e2_ref[h] = jnp.exp(s2_ref[h] - m2)


def peer_route(ht, wqt, keys):
    D, N = ht.shape
    tm = PEER_ROUTE_TOKENS
    assert N % tm == 0
    big = jax.ShapeDtypeStruct((PEER_HEADS, N_KEYS, N), jnp.float32)
    big_spec = pl.BlockSpec((PEER_HEADS, N_KEYS, tm), lambda i: (0, 0, i))
    key_spec = pl.BlockSpec((PEER_HEADS, N_KEYS, PEER_QDIM // 2), lambda i: (0, 0, 0))
    return pl.pallas_call(
        _peer_route_kernel,
        grid=(N // tm,),
        in_specs=[pl.BlockSpec((D, tm), lambda i: (0, i)),
                  pl.BlockSpec((PEER_HEADS * PEER_QDIM, D), lambda i: (0, 0)),
                  key_spec, key_spec],
        out_specs=[big_spec, big_spec, big_spec, big_spec, pl.BlockSpec((PEER_HEADS, tm), lambda i: (0, i))],
        out_shape=[big, big, big, big, jax.ShapeDtypeStruct((PEER_HEADS, N), jnp.float32)],
        scratch_shapes=[pltpu.VMEM((N_KEYS, tm), jnp.float32),
                        pltpu.VMEM((PEER_TOPK, tm), jnp.float32),
                        pltpu.VMEM((PEER_TOPK, tm), jnp.float32),
                        pltpu.VMEM((PEER_TOPK * PEER_TOPK, tm), jnp.float32)],
        compiler_params=pltpu.CompilerParams(
            dimension_semantics=("arbitrary",), vmem_limit_bytes=VMEM_LIMIT_BYTES),
        name="peer_route",
    )(ht, wqt, keys[0], keys[1])


def _gelu_tanh(x):
    return 0.5 * x * (1.0 + jnp.tanh(math.sqrt(2.0 / math.pi) * (x + 0.044715 * (x * x * x))))


def _peer_expert_kernel(ht_ref, u_ref, vt_ref, s1_ref, s2_ref, g1_ref, e2_ref, th_ref, o_ref, wg_ref):
    j = pl.program_id(1)
    tm = ht_ref.shape[1]

    @pl.when(j == 0)
    def _():
        o_ref[...] = jnp.zeros_like(o_ref)

    e1_rows = pl.ds(pl.multiple_of(j * PEER_E1_PER_STEP, PEER_E1_PER_STEP), PEER_E1_PER_STEP)
    ht = ht_ref[...]
    n_parts = PEER_E1_PER_STEP // PEER_E1_PER_PART
    part_slice = lambda p: slice(p * PEER_E1_PER_PART * N_KEYS, (p + 1) * PEER_E1_PER_PART * N_KEYS)
    activation = lambda p: _gelu_tanh(jnp.dot(u_ref[part_slice(p), :], ht, preferred_element_type=jnp.float32))
    act_next = activation(0)
    for part in range(n_parts):
        act = act_next
        if part + 1 < n_parts:
            act_next = activation(part + 1)
        for tc in range(tm // LANE):
            cols = slice(tc * LANE, (tc + 1) * LANE)
            s1_rows = [s1_ref[h, e1_rows, cols] for h in range(PEER_HEADS)]
            g1_rows = [g1_ref[h, e1_rows, cols] for h in range(PEER_HEADS)]
            for ee in range(PEER_E1_PER_PART):
                e = part * PEER_E1_PER_PART + ee
                w = None
                for h in range(PEER_HEADS):
                    score = s2_ref[h, :, cols] + s1_rows[h][e:e + 1, :]
                    gate = e2_ref[h, :, cols] * g1_rows[h][e:e + 1, :]
                    term = jnp.where(score >= th_ref[h:h + 1, cols], gate, 0.0)
                    w = term if w is None else w + term
                rows = slice(ee * N_KEYS, (ee + 1) * N_KEYS)
                wg_ref[part, rows, cols] = (w * act[rows, cols]).astype(jnp.bfloat16)
        if part >= 1:
            o_ref[...] += jnp.dot(vt_ref[:, part_slice(part - 1)], wg_ref[part - 1],
                                  preferred_element_type=jnp.float32)
    o_ref[...] += jnp.dot(vt_ref[:, part_slice(n_parts - 1)], wg_ref[n_parts - 1],
                          preferred_element_type=jnp.float32)


def peer_experts(ht, u, vt, s1, s2, g1, e2, th):
    D, N = ht.shape
    E = u.shape[0]
    tm = PEER_TOKENS
    te = PEER_E1_PER_STEP * N_KEYS
    assert N % tm == 0 and E % te == 0
    big_spec = pl.BlockSpec((PEER_HEADS, N_KEYS, tm), lambda i, j: (0, 0, i))
    return pl.pallas_call(
        _peer_expert_kernel,
        grid=(N // tm, E // te),
        in_specs=[pl.BlockSpec((D, tm), lambda i, j: (0, i)),
                  pl.BlockSpec((te, D), lambda i, j: (j, 0)),
                  pl.BlockSpec((D, te), lambda i, j: (0, j)),
                  big_spec, big_spec, big_spec, big_spec,
                  pl.BlockSpec((PEER_HEADS, tm), lambda i, j: (0, i))],
        out_specs=pl.BlockSpec((D, tm), lambda i, j: (0, i)),
        out_shape=jax.ShapeDtypeStruct((D, N), jnp.float32),
        scratch_shapes=[pltpu.VMEM((PEER_E1_PER_STEP // PEER_E1_PER_PART, PEER_E1_PER_PART * N_KEYS, tm), jnp.bfloat16)],
        compiler_params=pltpu.CompilerParams(
            dimension_semantics=("arbitrary", "arbitrary"), vmem_limit_bytes=VMEM_LIMIT_BYTES),
        name="peer_experts",
    )(ht, u, vt, s1, s2, g1, e2, th)


def peer_ffn_tokens(h, wqt, keys, u, vt):
    ht = jnp.transpose(h).astype(jnp.bfloat16)
    s1, s2, g1, e2, th = peer_route(ht, wqt, keys)
    return jnp.transpose(peer_experts(ht, u, vt, s1, s2, g1, e2, th))


ZB_Q = A_COLS // LANE
ZB_K = ZB_Q + B_HEADS
ZB_V = ZB_K + B_KV_HEADS
ZC_Q = ZB_V + B_KV_HEADS
ZC_K = ZC_Q + C_HEADS
ZC_V = ZC_K + C_HEADS
DIFF_Q_ROWS = 256

_DOT_NT = (((1,), (1,)), ((), ()))


def rope_lane_tables(n_tokens, head_dim):
    rows = n_tokens // GRID_W
    row_pos = jnp.repeat(jnp.arange(rows, dtype=jnp.float32), GRID_W)
    col_pos = jnp.tile(jnp.arange(GRID_W, dtype=jnp.float32), rows)
    n_freq = head_dim // 4
    freqs = ROPE_BASE ** (-jnp.arange(n_freq, dtype=jnp.float32) / n_freq)
    ang = jnp.concatenate([row_pos[:, None] * freqs, col_pos[:, None] * freqs], axis=-1)
    cos = jnp.repeat(jnp.cos(ang), 2, axis=-1)
    sin = jnp.repeat(jnp.sin(ang), 2, axis=-1) * jnp.tile(jnp.array([-1.0, 1.0], jnp.float32), head_dim // 2)
    reps = LANE // head_dim
    return jnp.tile(cos, (1, reps)), jnp.tile(sin, (1, reps))


def _rope(x, cos, sin_signed):
    lane = lax.broadcasted_iota(jnp.int32, x.shape, 1)
    partner = jnp.where(lane % 2 == 0, pltpu.roll(x, LANE - 1, axis=1), pltpu.roll(x, 1, axis=1))
    return x * cos + partner * sin_signed


def _bf(x):
    return x.astype(jnp.bfloat16)


def _diff_attn_kernel(lam_ref, q_ref, k_ref, v_ref, *rest, latent, lam_init):
    if latent:
        kc_ref, vc_ref, cq_ref, sq_ref, ck_ref, sk_ref, g_ref, o_ref, kr_ref = rest
    else:
        g_ref, o_ref, kr_ref = rest
    i = pl.program_id(2)

    @pl.when(i == 0)
    def _():
        k = k_ref[...]
        kr_ref[...] = _bf(_rope(k, ck_ref[...], sk_ref[...]) if latent else k)

    q = q_ref[...]
    if latent:
        q = _rope(q, cq_ref[...], sq_ref[...])
    lane = lax.broadcasted_iota(jnp.int32, q.shape, 1)
    scale = C_HEAD_DIM ** -0.5
    lam = lam_ref[0]
    kr = kr_ref[...]
    vb = _bf(v_ref[...])
    if latent:
        kcb, vcb = _bf(kc_ref[...]), _bf(vc_ref[...])

    def softmax_parts(qh):
        s = lax.dot_general(qh, kr, _DOT_NT, preferred_element_type=jnp.float32) * scale
        m = jnp.max(s, axis=-1, keepdims=True)
        if latent:
            sc = lax.dot_general(qh, kcb, _DOT_NT, preferred_element_type=jnp.float32) * scale
            m = jnp.maximum(m, jnp.max(sc, axis=-1, keepdims=True))
            ec = jnp.exp(sc - m)
        e = jnp.exp(s - m)
        z = jnp.sum(e, axis=-1, keepdims=True)
        if latent:
            z = z + jnp.sum(ec, axis=-1, keepdims=True)
            return e, ec, 1.0 / z
        return e, None, 1.0 / z

    e1, ec1, r1 = softmax_parts(_bf(jnp.where(lane < C_HEAD_DIM, q, 0.0)))
    e2, ec2, r2 = softmax_parts(_bf(jnp.where(lane >= C_HEAD_DIM, q, 0.0)))
    r2 = lam * r2
    o = jnp.dot(_bf(e1 * r1 - e2 * r2), vb, preferred_element_type=jnp.float32)
    if latent:
        o = o + jnp.dot(_bf(ec1 * r1 - ec2 * r2), vcb, preferred_element_type=jnp.float32)
    o = o * lax.rsqrt(jnp.mean(o * o, axis=-1, keepdims=True) + 1e-6) * g_ref[...] * (1.0 - lam_init)
    o_ref[...] = o


def diff_attention_mixer(z, lam, lam_init, subln_g, cache=None, layer=None, rope=None):
    B, T, _ = z.shape
    latent = cache is not None
    tq = min(DIFF_Q_ROWS, T)
    head = lambda base: pl.BlockSpec((None, tq, LANE), lambda b, h, i: (b, i, base + h))
    whole = lambda base: pl.BlockSpec((None, T, LANE), lambda b, h, i: (b, 0, base + h))
    in_specs = [pl.BlockSpec(memory_space=pltpu.SMEM), head(ZC_Q), whole(ZC_K), whole(ZC_V)]
    args = [lam.reshape(1), z, z, z]
    if latent:
        P = cache[0].shape[3]
        cspec = pl.BlockSpec((None, None, None, P, LANE), lambda b, h, i: (b, layer, h, 0, 0))
        tq_spec = pl.BlockSpec((tq, LANE), lambda b, h, i: (i, 0))
        tk_spec = pl.BlockSpec((T, LANE), lambda b, h, i: (0, 0))
        in_specs += [cspec, cspec, tq_spec, tq_spec, tk_spec, tk_spec]
        args += [cache[0], cache[1], rope[0], rope[1], rope[0], rope[1]]
    in_specs.append(pl.BlockSpec((1, LANE), lambda b, h, i: (0, 0)))
    args.append(subln_g.reshape(1, LANE))
    return pl.pallas_call(
        partial(_diff_attn_kernel, latent=latent, lam_init=lam_init),
        grid=(B, C_HEADS, T // tq),
        in_specs=in_specs,
        out_specs=pl.BlockSpec((None, tq, LANE), lambda b, h, i: (b, i, h)),
        out_shape=jax.ShapeDtypeStruct((B, T, C_HEADS * LANE), jnp.float32),
        scratch_shapes=[pltpu.VMEM((T, LANE), jnp.bfloat16)],
        compiler_params=pltpu.CompilerParams(
            dimension_semantics=("arbitrary", "arbitrary", "arbitrary"), vmem_limit_bytes=VMEM_LIMIT_BYTES),
        name="diff_attention",
    )(*args)


def _gqa_kernel(sink_ref, q0_ref, q1_ref, *rest, latent, n_tokens):
    if latent:
        (kp_ref, kn_ref, kx_ref, vp_ref, vn_ref, vx_ref, kc_ref, vc_ref,
         cq_ref, sq_ref, cp_ref, sp_ref, cx_ref, sx_ref, o_ref) = rest
    else:
        kc_ref, vc_ref, o_ref = rest
    kvh = pl.program_id(1)
    n = pl.program_id(2)
    scale = B_HEAD_DIM ** -0.5
    q0, q1 = q0_ref[...], q1_ref[...]
    if latent:
        q0 = _rope(q0, cq_ref[...], sq_ref[...])
        q1 = _rope(q1, cq_ref[...], sq_ref[...])
    q = _bf(jnp.concatenate([q0, q1], axis=0))
    row = lax.broadcasted_iota(jnp.int32, (2 * BLOCK, 1), 0)
    sink = jnp.where(row < BLOCK, sink_ref[2 * kvh], sink_ref[2 * kvh + 1])

    s_c = lax.dot_general(q, _bf(kc_ref[...]), _DOT_NT, preferred_element_type=jnp.float32) * scale
    m = jnp.maximum(jnp.max(s_c, axis=-1, keepdims=True), sink)
    if latent:
        kw = jnp.concatenate([_rope(kp_ref[...], cp_ref[...], sp_ref[...]),
                              _rope(kn_ref[...], cq_ref[...], sq_ref[...]),
                              _rope(kx_ref[...], cx_ref[...], sx_ref[...])], axis=0)
        vw = jnp.concatenate([vp_ref[...], vn_ref[...], vx_ref[...]], axis=0)
        s_w = lax.dot_general(q, _bf(kw), _DOT_NT, preferred_element_type=jnp.float32) * scale
        qpos = n * BLOCK + lax.broadcasted_iota(jnp.int32, s_w.shape, 0) % BLOCK
        kpos = (n - 1) * BLOCK + lax.broadcasted_iota(jnp.int32, s_w.shape, 1)
        valid = (jnp.abs(qpos - kpos) <= WINDOW) & (kpos >= 0) & (kpos < n_tokens)
        s_w = jnp.where(valid, s_w, NEG_INF)
        m = jnp.maximum(m, jnp.max(s_w, axis=-1, keepdims=True))
        e_w = jnp.exp(s_w - m)
    e_c = jnp.exp(s_c - m)
    zsum = jnp.sum(e_c, axis=-1, keepdims=True) + jnp.exp(sink - m)
    o = jnp.dot(_bf(e_c), _bf(vc_ref[...]), preferred_element_type=jnp.float32)
    if latent:
        zsum = zsum + jnp.sum(e_w, axis=-1, keepdims=True)
        o = o + jnp.dot(_bf(e_w), _bf(vw), preferred_element_type=jnp.float32)
    o = o / zsum
    o_ref[:, 0:LANE] = o[0:BLOCK]
    o_ref[:, LANE:2 * LANE] = o[BLOCK:2 * BLOCK]


def gqa_mixer(z, sink, cache=None, layer=None, rope=None):
    B, T, _ = z.shape
    latent = cache is not None
    nb = T // BLOCK
    blk = lambda base, off: pl.BlockSpec(
        (None, BLOCK, LANE), lambda b, kvh, n: (b, jnp.clip(n + off, 0, nb - 1), base + kvh))
    qspec = lambda g: pl.BlockSpec((None, BLOCK, LANE), lambda b, kvh, n: (b, n, ZB_Q + 2 * kvh + g))
    in_specs = [pl.BlockSpec(memory_space=pltpu.SMEM), qspec(0), qspec(1)]
    args = [sink, z, z]
    if latent:
        P = cache[0].shape[3]
        cspec = pl.BlockSpec((None, None, None, P, LANE), lambda b, kvh, n: (b, layer, kvh, 0, 0))
        tab = lambda off: pl.BlockSpec((BLOCK, LANE), lambda b, kvh, n: (jnp.clip(n + off, 0, nb - 1), 0))
        in_specs += [blk(ZB_K, -1), blk(ZB_K, 0), blk(ZB_K, 1), blk(ZB_V, -1), blk(ZB_V, 0), blk(ZB_V, 1),
                     cspec, cspec, tab(0), tab(0), tab(-1), tab(-1), tab(1), tab(1)]
        args += [z] * 6 + [cache[0], cache[1]] + [rope[0], rope[1]] * 3
    else:
        whole = lambda base: pl.BlockSpec((None, T, LANE), lambda b, kvh, n: (b, 0, base + kvh))
        in_specs += [whole(ZB_K), whole(ZB_V)]
        args += [z, z]
    return pl.pallas_call(
        partial(_gqa_kernel, latent=latent, n_tokens=T),
        grid=(B, B_KV_HEADS, nb),
        in_specs=in_specs,
        out_specs=pl.BlockSpec((None, BLOCK, 2 * LANE), lambda b, kvh, n: (b, n, kvh)),
        out_shape=jax.ShapeDtypeStruct((B, T, B_HEADS * LANE), jnp.float32),
        compiler_params=pltpu.CompilerParams(
            dimension_semantics=("arbitrary", "arbitrary", "arbitrary"), vmem_limit_bytes=VMEM_LIMIT_BYTES),
        name="gqa_attention",
    )(*args)


def split_cols(z, sizes):
    return jnp.split(z, [int(s) for s in np.cumsum(sizes)[:-1]], axis=-1)


def layer_norm(x, g=None, b=None):
    mu = jnp.mean(x, axis=-1, keepdims=True)
    var = jnp.mean(jnp.square(x - mu), axis=-1, keepdims=True)
    y = (x - mu) * lax.rsqrt(var + LN_EPS)
    if g is not None:
        y = y * g + b
    return y


def centred_conv3(x, w):
    xp = jnp.pad(x, ((0, 0), (1, 1), (0, 0)))
    return xp[:, :-2] * w[0] + xp[:, 1:-1] * w[1] + xp[:, 2:] * w[2]


def rwkv7_mixer(za, S0_f, S0_b, lp):
    Bsz, T, _ = za.shape
    r, k, v, g_d, wd_f, wd_b, ad_f, ad_b = split_cols(
        za, [A_WIDTH, A_WIDTH, A_WIDTH, G_LORA, W_LORA, W_LORA, ICL_LORA, ICL_LORA])
    heads = lambda t: t.reshape(Bsz, T, A_HEADS, A_HEAD_DIM)
    wz = jnp.stack([mm3(jnp.tanh(wd), lp['a_w_up'][d]) for d, wd in enumerate((wd_f, wd_b))], axis=0)
    az = jnp.stack([mm3(ad, lp['a_a_up'][d]) for d, ad in enumerate((ad_f, ad_b))], axis=0)
    s0t = jnp.swapaxes(jnp.stack([S0_f, S0_b], axis=0), -1, -2)
    ys, sfin_t = rwkv7_chunked(za, wz, az, lp['a_w0'], lp['a_a0'], lp['a_k_k'], lp['a_k_a'], s0t)
    states = jnp.swapaxes(sfin_t, -1, -2)
    y = heads(ys[0] + ys[1])
    mu = jnp.mean(y, axis=-1, keepdims=True)
    var = jnp.mean(jnp.square(y - mu), axis=-1, keepdims=True)
    y = ((y - mu) * lax.rsqrt(var + A_GN_EPS)).reshape(Bsz, T, A_WIDTH) * lp['a_gn_g'] + lp['a_gn_b']
    icl = jax.nn.sigmoid(az + lp['a_a0'][:, None, None, :])
    kd_sum = k * (2.0 + (icl[0] + icl[1] - 2.0) * lp['a_k_a'])
    bonus = jnp.sum(heads(r) * heads(kd_sum) * lp['a_r_k'], axis=-1, keepdims=True) * heads(v)
    g = mm3(jax.nn.sigmoid(g_d), lp['a_g_up'])
    out = (y + bonus.reshape(Bsz, T, A_WIDTH)) * g
    return out, states[0], states[1]


def diff_lambda(lv, layer):
    lam_init = 0.8 - 0.6 * math.exp(-0.3 * layer)
    lam = jnp.exp(jnp.sum(lv[0] * lv[1])) - jnp.exp(jnp.sum(lv[2] * lv[3])) + lam_init
    return lam, lam_init


def adaln_params(cond, lp):
    mod = mm(jax.nn.silu(cond), lp['w_mod']) + lp['b_mod']
    return jnp.split(mod[:, None, :], 6, axis=-1)


def mixer_inputs(x, shift, scale, lp):
    h = layer_norm(x) * (1.0 + scale) + shift
    z = mm3(h, lp['w_in'])
    zA = z[..., :A_COLS]
    zG = z[..., IN_COLS - 3 * D_MODEL:]
    return z, centred_conv3(zA, lp['a_conv']), zG


def merge_mixers(x, yA, yB, yC, zG, gate1, shift2, scale2, lp):
    gA, gB, gC = jnp.split(jax.nn.sigmoid(zG), 3, axis=-1)
    merged = gA * mm3(yA, lp['p_a']) + gB * mm3(yB, lp['p_b']) + gC * mm3(yC, lp['p_c'])
    x = layer_norm(DEEPNORM_ALPHA * x + gate1 * mm3(merged, lp['w_out']), lp['ln1_g'], lp['ln1_b'])
    h = layer_norm(x) * (1.0 + scale2) + shift2
    return x, h


def peer_and_norm(parts, lp):
    sizes = [x.shape[0] * x.shape[1] for x, _, _ in parts]
    h_all = jnp.concatenate([h.reshape(-1, D_MODEL) for _, h, _ in parts], axis=0)
    y_all = peer_ffn_tokens(h_all, lp['peer_wqt'], lp['peer_keys'], lp['peer_u'], lp['peer_vt'])
    outs, start = [], 0
    for (x, _, gate2), n in zip(parts, sizes):
        y = y_all[start:start + n].reshape(x.shape)
        start += n
        outs.append(layer_norm(DEEPNORM_ALPHA * x + gate2 * y, lp['ln2_g'], lp['ln2_b']))
    return outs


def context_layer(x, cond, lp, layer):
    Bsz, L, _ = x.shape
    shift1, scale1, gate1, shift2, scale2, gate2 = adaln_params(cond, lp)
    z, zA, zG = mixer_inputs(x, shift1, scale1, lp)
    S0 = jnp.zeros((Bsz, A_HEADS, A_HEAD_DIM, A_HEAD_DIM), jnp.float32)
    yA, S_f, S_b = rwkv7_mixer(zA, S0, S0, lp)
    yB = gqa_mixer(z, lp['b_sink'])
    lam, lam_init = diff_lambda(lp['c_lam'], layer)
    yC = diff_attention_mixer(z, lam, lam_init, lp['c_subln'])
    x_mid, h_mid = merge_mixers(x, yA, yB, yC, zG, gate1, shift2, scale2, lp)

    def cache_layout(first_block, n_heads):
        t = z[..., first_block * LANE:(first_block + n_heads) * LANE].reshape(Bsz, L, n_heads, LANE)
        return jnp.transpose(t, (0, 2, 1, 3))
    ctx = (S_f, S_b, cache_layout(ZB_K, B_KV_HEADS), cache_layout(ZB_V, B_KV_HEADS),
           cache_layout(ZC_K, C_HEADS), cache_layout(ZC_V, C_HEADS))
    return (x_mid, h_mid, gate2), ctx


def latent_layer(x, cond, lp, layer, S_f0, S_b0, cache_b, cache_c, rope_b, rope_c):
    shift1, scale1, gate1, shift2, scale2, gate2 = adaln_params(cond, lp)
    z, zA, zG = mixer_inputs(x, shift1, scale1, lp)
    yA, _, _ = rwkv7_mixer(zA, S_f0, S_b0, lp)
    yB = gqa_mixer(z, lp['b_sink'], cache=cache_b, layer=layer, rope=rope_b)
    lam, lam_init = diff_lambda(lp['c_lam'], layer)
    yC = diff_attention_mixer(z, lam, lam_init, lp['c_subln'], cache=cache_c, layer=layer, rope=rope_c)
    x_mid, h_mid = merge_mixers(x, yA, yB, yC, zG, gate1, shift2, scale2, lp)
    return x_mid, h_mid, gate2


_BF16_WEIGHTS = ('w_mod', 'w_in', 'a_w_up', 'a_a_up', 'a_g_up', 'p_a', 'p_b', 'p_c', 'w_out', 'peer_u')


def kernel(x_prompt, x_sample, state_a_fwd, state_a_bwd, cache_b_k, cache_b_v, cache_c_k, cache_c_v, c, c_ctx, w_mod, b_mod, w_in, a_conv, a_w0, a_w_up, a_a0, a_a_up, a_g_up, a_k_k, a_k_a, a_r_k, a_gn_g, a_gn_b, b_sink, c_lam, c_subln, p_a, p_b, p_c, w_out, ln1_g, ln1_b, ln2_g, ln2_b, peer_wq, peer_keys, peer_u, peer_v):
    params = dict(w_mod=w_mod, b_mod=b_mod, w_in=w_in, a_conv=a_conv, a_w0=a_w0, a_w_up=a_w_up, a_a0=a_a0,
                  a_a_up=a_a_up, a_g_up=a_g_up, a_k_k=a_k_k, a_k_a=a_k_a, a_r_k=a_r_k, a_gn_g=a_gn_g,
                  a_gn_b=a_gn_b, b_sink=b_sink, c_lam=c_lam, c_subln=c_subln, p_a=p_a, p_b=p_b, p_c=p_c,
                  w_out=w_out, ln1_g=ln1_g, ln1_b=ln1_b, ln2_g=ln2_g, ln2_b=ln2_b, peer_wq=peer_wq,
                  peer_keys=peer_keys, peer_u=peer_u, peer_v=peer_v)
    for name in _BF16_WEIGHTS:
        params[name] = params[name].astype(jnp.bfloat16)
    params['peer_wqt'] = jnp.swapaxes(params.pop('peer_wq'), 1, 2).astype(jnp.bfloat16)
    params['peer_vt'] = jnp.swapaxes(params.pop('peer_v'), 1, 2).astype(jnp.bfloat16)
    n_lat = x_sample.shape[1]
    rope_b = rope_lane_tables(n_lat, B_HEAD_DIM)
    rope_c = rope_lane_tables(n_lat, C_HEAD_DIM)
    cond_ctx = jnp.broadcast_to(c_ctx[None, :], (x_prompt.shape[0], D_MODEL))
    y_prompt, y_sample = x_prompt, x_sample
    new = [[], [], [], [], [], []]
    for layer in range(DEPTH):
        lp = {name: val[layer] for name, val in params.items()}
        part_ctx, ctx = context_layer(y_prompt, cond_ctx, lp, layer)
        for lst, t in zip(new, ctx):
            lst.append(t)
        part_lat = latent_layer(y_sample, c, lp, layer,
                                state_a_fwd[:, layer], state_a_bwd[:, layer],
                                (cache_b_k, cache_b_v), (cache_c_k, cache_c_v), rope_b, rope_c)
        y_prompt, y_sample = peer_and_norm([part_ctx, part_lat], lp)
    return (y_prompt, y_sample) + tuple(jnp.stack(lst, axis=1) for lst in new)
```

```python
import math
from functools import partial

import jax
import jax.numpy as jnp
import numpy as np
from jax import lax
from jax.experimental import pallas as pl
from jax.experimental.pallas import tpu as pltpu

D_MODEL = 2048
DEPTH = 4
GRID_W = 64
BLOCK = 128
A_HEADS = 16
A_HEAD_DIM = 64
A_WIDTH = A_HEADS * A_HEAD_DIM
W_LORA = 64
ICL_LORA = 64
G_LORA = 128
A_COLS = 3 * A_WIDTH + G_LORA + 2 * W_LORA + 2 * ICL_LORA
A_GN_EPS = 64e-5
B_HEADS = 4
B_KV_HEADS = 2
B_HEAD_DIM = 128
B_WIDTH = B_HEADS * B_HEAD_DIM
B_KV_WIDTH = B_KV_HEADS * B_HEAD_DIM
WINDOW = 128
C_HEADS = 4
C_HEAD_DIM = 64
C_WIDTH = C_HEADS * 2 * C_HEAD_DIM
IN_COLS = A_COLS + B_WIDTH + 2 * B_KV_WIDTH + 3 * C_WIDTH + 3 * D_MODEL
PEER_HEADS = 8
N_KEYS = 128
PEER_QDIM = 256
PEER_TOPK = 16
ROPE_BASE = 10000.0
LN_EPS = 1e-5
NEG_INF = -1e30
DEEPNORM_ALPHA = (2 * DEPTH) ** 0.25

LANE = 128
VMEM_LIMIT_BYTES = 56 * 1024 * 1024


def _mm_kernel(x_ref, w_ref, o_ref):
    o_ref[...] = jnp.dot(x_ref[...].astype(jnp.bfloat16), w_ref[...],
                         preferred_element_type=jnp.float32)


def _pick_tile(n, cap, unit):
    if n <= cap:
        return n
    best = None
    for t in range(unit, cap + 1, unit):
        if n % t == 0:
            best = t
    assert best is not None, (n, cap, unit)
    return best


def mm(x, w):
    M, K = x.shape
    N = w.shape[1]
    tm = _pick_tile(M, 512, 8)
    tn = _pick_tile(N, 1024, LANE)
    return pl.pallas_call(
        _mm_kernel,
        grid=(N // tn, M // tm),
        in_specs=[pl.BlockSpec((tm, K), lambda j, i: (i, 0)),
                  pl.BlockSpec((K, tn), lambda j, i: (0, j))],
        out_specs=pl.BlockSpec((tm, tn), lambda j, i: (i, j)),
        out_shape=jax.ShapeDtypeStruct((M, N), jnp.float32),
        compiler_params=pltpu.CompilerParams(
            dimension_semantics=("arbitrary", "arbitrary"),
            vmem_limit_bytes=VMEM_LIMIT_BYTES),
        name="mm",
    )(x, w)


def mm3(x, w):
    B, T, K = x.shape
    return mm(x.reshape(B * T, K), w).reshape(B, T, w.shape[1])


ROW_TILE = 1024
NORM_ROW_TILE = 512
Z_GATES = 0
Z_A = 3 * D_MODEL
Z_B = Z_A + A_COLS
Z_C = Z_B + B_WIDTH + 2 * B_KV_WIDTH


def _row_blocks(n_batch, n_time, rows):
    tt = min(n_time, rows)
    bt = rows // tt
    assert n_time % tt == 0 and n_batch % bt == 0, (n_batch, n_time, rows)
    return bt, tt


def _layer_norm_rows(x):
    mu = jnp.mean(x, axis=-1, keepdims=True)
    xc = x - mu
    return xc * lax.rsqrt(jnp.mean(xc * xc, axis=-1, keepdims=True) + LN_EPS)


def _ln_mod_mm_kernel(x_ref, scale_ref, shift_ref, w_ref, o_ref, h_ref):
    bt, tt, K = x_ref.shape

    @pl.when(pl.program_id(2) == 0)
    def _():
        h = _layer_norm_rows(x_ref[...]) * (1.0 + scale_ref[...]) + shift_ref[...]
        h_ref[...] = h.reshape(bt * tt, K).astype(jnp.bfloat16)

    o = jnp.dot(h_ref[...], w_ref[...], preferred_element_type=jnp.float32)
    o_ref[...] = o.reshape(bt, tt, o.shape[-1])


def ln_mod_matmul(x, scale, shift, w):
    B, T, K = x.shape
    N = w.shape[1]
    bt, tt = _row_blocks(B, T, ROW_TILE)
    tn = _pick_tile(N, 1024, LANE)
    vec = pl.BlockSpec((bt, 1, K), lambda b, t, j: (b, 0, 0))
    return pl.pallas_call(
        _ln_mod_mm_kernel,
        grid=(B // bt, T // tt, N // tn),
        in_specs=[pl.BlockSpec((bt, tt, K), lambda b, t, j: (b, t, 0)), vec, vec,
                  pl.BlockSpec((K, tn), lambda b, t, j: (0, j))],
        out_specs=pl.BlockSpec((bt, tt, tn), lambda b, t, j: (b, t, j)),
        out_shape=jax.ShapeDtypeStruct((B, T, N), jnp.float32),
        scratch_shapes=[pltpu.VMEM((bt * tt, K), jnp.bfloat16)],
        compiler_params=pltpu.CompilerParams(
            dimension_semantics=("arbitrary", "arbitrary", "arbitrary"), vmem_limit_bytes=VMEM_LIMIT_BYTES),
        name="ln_mod_matmul",
    )(x, scale, shift, w)


def _merge_kernel(ya_ref, yb_ref, yc_ref, ga_ref, gb_ref, gc_ref, pa_ref, pb_ref, pc_ref, o_ref,
                  ya_bf, yb_bf, yc_bf):
    @pl.when(pl.program_id(1) == 0)
    def _():
        ya_bf[...] = ya_ref[...].astype(jnp.bfloat16)
        yb_bf[...] = yb_ref[...].astype(jnp.bfloat16)
        yc_bf[...] = yc_ref[...].astype(jnp.bfloat16)

    f = lambda y, p: jnp.dot(y[...], p[...], preferred_element_type=jnp.float32)
    o = (jax.nn.sigmoid(ga_ref[...]) * f(ya_bf, pa_ref) + jax.nn.sigmoid(gb_ref[...]) * f(yb_bf, pb_ref)
         + jax.nn.sigmoid(gc_ref[...]) * f(yc_bf, pc_ref))
    o_ref[...] = o.astype(jnp.bfloat16)


def merge_gated(yA, yB, yC, z, p_a, p_b, p_c):
    N = yA.shape[0]
    D = p_a.shape[1]
    tm = _pick_tile(N, ROW_TILE, 8)
    tn = 512
    assert Z_GATES == 0 and D % tn == 0
    ysp = lambda y: pl.BlockSpec((tm, y.shape[1]), lambda i, j: (i, 0))
    gsp = lambda k: pl.BlockSpec((tm, tn), lambda i, j: (i, k * (D // tn) + j))
    psp = lambda p: pl.BlockSpec((p.shape[0], tn), lambda i, j: (0, j))
    return pl.pallas_call(
        _merge_kernel,
        grid=(N // tm, D // tn),
        in_specs=[ysp(yA), ysp(yB), ysp(yC), gsp(0), gsp(1), gsp(2), psp(p_a), psp(p_b), psp(p_c)],
        out_specs=pl.BlockSpec((tm, tn), lambda i, j: (i, j)),
        out_shape=jax.ShapeDtypeStruct((N, D), jnp.bfloat16),
        scratch_shapes=[pltpu.VMEM((tm, y.shape[1]), jnp.bfloat16) for y in (yA, yB, yC)],
        compiler_params=pltpu.CompilerParams(
            dimension_semantics=("arbitrary", "arbitrary"), vmem_limit_bytes=VMEM_LIMIT_BYTES),
        name="merge_gated",
    )(yA, yB, yC, z, z, z, p_a, p_b, p_c)


def _out_norm_kernel(m_ref, x_ref, gate_ref, scale_ref, shift_ref, lng_ref, lnb_ref, w_ref, x1_ref, h_ref):
    bt, tt, D = x_ref.shape
    y = jnp.dot(m_ref[...].reshape(bt * tt, D), w_ref[...], preferred_element_type=jnp.float32)
    x1 = _layer_norm_rows(DEEPNORM_ALPHA * x_ref[...] + gate_ref[...] * y.reshape(bt, tt, D))
    x1 = x1 * lng_ref[...] + lnb_ref[...]
    x1_ref[...] = x1
    h_ref[...] = (_layer_norm_rows(x1) * (1.0 + scale_ref[...]) + shift_ref[...]).astype(jnp.bfloat16)


def out_proj_norm(merged, x, gate1, scale2, shift2, ln_g, ln_b, w_out):
    B, T, D = x.shape
    bt, tt = _row_blocks(B, T, NORM_ROW_TILE)
    rows = pl.BlockSpec((bt, tt, D), lambda b, t: (b, t, 0))
    vec = pl.BlockSpec((bt, 1, D), lambda b, t: (b, 0, 0))
    par = pl.BlockSpec((1, 1, D), lambda b, t: (0, 0, 0))
    return pl.pallas_call(
        _out_norm_kernel,
        grid=(B // bt, T // tt),
        in_specs=[rows, rows, vec, vec, vec, par, par, pl.BlockSpec((D, D), lambda b, t: (0, 0))],
        out_specs=[rows, rows],
        out_shape=[jax.ShapeDtypeStruct((B, T, D), jnp.float32), jax.ShapeDtypeStruct((B, T, D), jnp.bfloat16)],
        compiler_params=pltpu.CompilerParams(
            dimension_semantics=("arbitrary", "arbitrary"), vmem_limit_bytes=VMEM_LIMIT_BYTES),
        name="out_proj_norm",
    )(merged.reshape(B, T, D), x, gate1, scale2, shift2, ln_g.reshape(1, 1, D), ln_b.reshape(1, 1, D), w_out)


RWKV_CHUNK = 64
RWKV_HEADS_PER_STEP = 8

_NN = (((2,), (1,)), ((0,), (0,)))
_NT = (((2,), (2,)), ((0,), (0,)))
_TN = (((1,), (1,)), ((0,), (0,)))


def _split2(x):
    hi = x.astype(jnp.bfloat16)
    lo = (x - hi.astype(jnp.float32)).astype(jnp.bfloat16)
    return hi, lo


def _dot3(a, b, dims=_NN):
    f = lambda x, y: lax.dot_general(x, y, dims, preferred_element_type=jnp.float32)
    return f(a[0], b[0]) + (f(a[0], b[1]) + f(a[1], b[0]))


def _dot1(a, b, dims=_NN):
    return lax.dot_general(a[0], b[0], dims, preferred_element_type=jnp.float32)


def _split_heads(x):
    n = A_HEAD_DIM
    return jnp.stack([x[:, h * n:(h + 1) * n] for h in range(x.shape[1] // n)], axis=0)


def _rwkv_chunk_kernel(zr_ref, zk_ref, zv_ref, wz_ref, az_ref, w0_ref, a0_ref, kk_ref, ka_ref, s0_ref,
                       y_ref, sfin_ref, st_ref):
    d = pl.program_id(0)
    c = pl.program_id(3)
    n_chunks = pl.num_programs(3)
    HB = st_ref.shape[0]
    C = zr_ref.shape[0]

    @pl.when(c == 0)
    def _():
        st_ref[...] = s0_ref[...]

    k_all = zk_ref[...]
    lw_all = -math.exp(-0.5) * jax.nn.sigmoid(wz_ref[...] + w0_ref[...])
    icl_all = jax.nn.sigmoid(az_ref[...] + a0_ref[...])
    r, v = _split_heads(zr_ref[...]), _split_heads(zv_ref[...])
    lw = _split_heads(lw_all)
    k = _split_heads(k_all * (1.0 + (icl_all - 1.0) * ka_ref[...]))
    kk = _split_heads(k_all * kk_ref[...])
    kk = kk * lax.rsqrt(jnp.sum(kk * kk, axis=-1, keepdims=True) + 1e-12)
    a = -kk
    b = kk * _split_heads(icl_all)

    ti = lax.broadcasted_iota(jnp.int32, (HB, C, C), 1)
    si = lax.broadcasted_iota(jnp.int32, (HB, C, C), 2)
    diff = (si - ti) * (1 - 2 * d)
    m_incl = diff <= 0
    m_strict = diff < 0
    m_incl_bf = jnp.where(m_incl, 1.0, 0.0).astype(jnp.bfloat16)
    eye = jnp.where(diff == 0, 1.0, 0.0).astype(jnp.float32)

    st = st_ref[...]

    lw_hi, lw_lo = _split2(lw)
    lw_lo2 = (lw - lw_hi.astype(jnp.float32) - lw_lo.astype(jnp.float32)).astype(jnp.bfloat16)
    f = lambda y: lax.dot_general(m_incl_bf, y, _NN, preferred_element_type=jnp.float32)
    cum = f(lw_hi) + (f(lw_lo) + f(lw_lo2))
    total = jnp.sum(lw, axis=1, keepdims=True)
    e_cum = jnp.exp(cum)
    e_inv = jnp.exp(-cum)
    e_tot = jnp.exp(total)
    hi = lambda x: (x.astype(jnp.bfloat16),)
    at = hi(a * jnp.exp(cum - lw))
    rt = _split2(r * e_cum)
    bt_f = b * e_inv
    kt_f = k * e_inv
    bt, kt = hi(bt_f), hi(kt_f)
    bh, kh = _split2(bt_f * e_tot), _split2(kt_f * e_tot)
    vs = _split2(v)

    l_ab = jnp.where(m_strict, _dot1(at, bt, _NT), 0.0)
    l_ak = jnp.where(m_strict, _dot1(at, kt, _NT), 0.0)
    m_rb = jnp.where(m_incl, _dot1(rt, bt, _NT), 0.0)
    m_rk = jnp.where(m_incl, _dot1(rt, kt, _NT), 0.0)

    tm = eye + l_ab
    lp = hi(l_ab)
    for _ in range(int(math.log2(C)) - 1):
        lp = hi(_dot1(lp, lp))
        tm = tm + _dot1(hi(tm), lp)
    tms = hi(tm)
    p = _split2(_dot1(tms, at))
    q = _dot1(tms, hi(_dot1(hi(l_ak), vs)))

    sts = _split2(st)
    u = _split2(_dot3(p, sts) + q)
    y = _dot3(rt, sts) + _dot3(_split2(m_rb), u) + _dot3(_split2(m_rk), vs)
    y_ref[...] = jnp.concatenate([y[h] for h in range(HB)], axis=-1)
    st_ref[...] = jnp.swapaxes(e_tot, 1, 2) * st + _dot3(bh, u, _TN) + _dot3(kh, vs, _TN)

    @pl.when(c == n_chunks - 1)
    def _():
        sfin_ref[...] = st_ref[...]


def rwkv7_chunked(zc, wz, az, w0, a0, k_k, k_a, s0t):
    D, B, T, W = wz.shape
    N, H = A_HEAD_DIM, A_HEADS
    C, HB = RWKV_CHUNK, RWKV_HEADS_PER_STEP
    G = H // HB
    lanes = HB * N
    assert T % C == 0 and H % HB == 0 and W == H * N
    n_chunks = T // C
    chunk = lambda d, c: c + d * (n_chunks - 1 - 2 * c)
    zcol = lambda base: pl.BlockSpec((None, C, lanes), lambda d, bb, g, c: (bb, chunk(d, c), base * G + g))
    per_dir = pl.BlockSpec((None, None, C, lanes), lambda d, bb, g, c: (d, bb, chunk(d, c), g))
    dir_row = pl.BlockSpec((None, 1, lanes), lambda d, bb, g, c: (d, 0, g))
    row = pl.BlockSpec((1, lanes), lambda d, bb, g, c: (0, g))
    state = pl.BlockSpec((None, None, HB, N, N), lambda d, bb, g, c: (d, bb, g, 0, 0))
    return pl.pallas_call(
        _rwkv_chunk_kernel,
        grid=(D, B, G, n_chunks),
        in_specs=[zcol(0), zcol(1), zcol(2), per_dir, per_dir, dir_row, dir_row, row, row, state],
        out_specs=[per_dir, state],
        out_shape=[jax.ShapeDtypeStruct((D, B, T, W), jnp.float32),
                   jax.ShapeDtypeStruct((D, B, H, N, N), jnp.float32)],
        scratch_shapes=[pltpu.VMEM((HB, N, N), jnp.float32)],
        compiler_params=pltpu.CompilerParams(
            dimension_semantics=("arbitrary", "arbitrary", "arbitrary", "arbitrary"),
            vmem_limit_bytes=VMEM_LIMIT_BYTES),
        name="rwkv7_chunked",
    )(zc, zc, zc, wz, az, w0.reshape(D, 1, W), a0.reshape(D, 1, W), k_k.reshape(1, W), k_a.reshape(1, W), s0t)


PEER_ROUTE_TOKENS = 256
PEER_TOKENS = 512
PEER_E1_PER_STEP = 8
PEER_E1_PER_PART = 2


def _top_rows(s_ref, n_rows, k, emit):
    for a in range(k):
        s = s_ref[0:n_rows, :]
        mx = jnp.max(s, axis=0, keepdims=True)
        emit(a, mx)
        if a + 1 < k:
            s_ref[0:n_rows, :] = jnp.where(s == mx, NEG_INF, s)


def _peer_route_kernel(ht_ref, wqt_ref, k1_ref, k2_ref, s1_ref, s2_ref, g1_ref, e2_ref, th_ref,
                       work_ref, t1_ref, t2_ref, cand_ref):
    half = PEER_QDIM // 2
    qt = jnp.dot(wqt_ref[...], ht_ref[...], preferred_element_type=jnp.float32)
    for h in range(PEER_HEADS):
        for which, (k_ref, s_out, t_ref) in enumerate(((k1_ref, s1_ref, t1_ref), (k2_ref, s2_ref, t2_ref))):
            q = qt[h * PEER_QDIM + which * half: h * PEER_QDIM + (which + 1) * half, :]
            q_hi, q_lo = _split2(q)
            k_hi, k_lo = _split2(k_ref[h])
            f = lambda x, y: jnp.dot(x, y, preferred_element_type=jnp.float32)
            s = f(k_hi, q_hi) + (f(k_hi, q_lo) + f(k_lo, q_hi))
            s_out[h] = s
            work_ref[...] = s

            def emit(a, mx, t_ref=t_ref):
                t_ref[a:a + 1, :] = mx
            _top_rows(work_ref, N_KEYS, PEER_TOPK, emit)
        t1 = t1_ref[...]
        t2 = t2_ref[...]
        m1 = t1[0:1, :]
        m2 = t2[0:1, :]
        for a in range(PEER_TOPK):
            cand_ref[a * PEER_TOPK:(a + 1) * PEER_TOPK, :] = t1[a:a + 1, :] + t2
        acc = {}

        def emit_c(a, mx):
            e = jnp.exp(mx - (m1 + m2))
            acc['z'] = e if a == 0 else acc['z'] + e
            acc['th'] = mx
        _top_rows(cand_ref, PEER_TOPK * PEER_TOPK, PEER_TOPK, emit_c)
        th_ref[h:h + 1, :] = acc['th']
        g1_ref[h] = jnp.exp(s1_ref[h] - m1) / acc['z']
        e2_ref[h] = jnp.exp(s2_ref[h] - m2)


def peer_route(ht, wqt, keys):
    D, N = ht.shape
    tm = PEER_ROUTE_TOKENS
    assert N % tm == 0
    big = jax.ShapeDtypeStruct((PEER_HEADS, N_KEYS, N), jnp.float32)
    big_spec = pl.BlockSpec((PEER_HEADS, N_KEYS, tm), lambda i: (0, 0, i))
    key_spec = pl.BlockSpec((PEER_HEADS, N_KEYS, PEER_QDIM // 2), lambda i: (0, 0, 0))
    return pl.pallas_call(
        _peer_route_kernel,
        grid=(N // tm,),
        in_specs=[pl.BlockSpec((D, tm), lambda i: (0, i)),
                  pl.BlockSpec((PEER_HEADS * PEER_QDIM, D), lambda i: (0, 0)),
                  key_spec, key_spec],
        out_specs=[big_spec, big_spec, big_spec, big_spec, pl.BlockSpec((PEER_HEADS, tm), lambda i: (0, i))],
        out_shape=[big, big, big, big, jax.ShapeDtypeStruct((PEER_HEADS, N), jnp.float32)],
        scratch_shapes=[pltpu.VMEM((N_KEYS, tm), jnp.float32),
                        pltpu.VMEM((PEER_TOPK, tm), jnp.float32),
                        pltpu.VMEM((PEER_TOPK, tm), jnp.float32),
                        pltpu.VMEM((PEER_TOPK * PEER_TOPK, tm), jnp.float32)],
        compiler_params=pltpu.CompilerParams(
            dimension_semantics=("arbitrary",), vmem_limit_bytes=VMEM_LIMIT_BYTES),
        name="peer_route",
    )(ht, wqt, keys[0], keys[1])


def _gelu_tanh(x):
    return 0.5 * x * (1.0 + jnp.tanh(math.sqrt(2.0 / math.pi) * (x + 0.044715 * (x * x * x))))


def _peer_expert_kernel(ht_ref, u_ref, vt_ref, s1_ref, s2_ref, g1_ref, e2_ref, th_ref, o_ref, wg_ref):
    j = pl.program_id(1)
    tm = ht_ref.shape[1]

    @pl.when(j == 0)
    def _():
        o_ref[...] = jnp.zeros_like(o_ref)

    e1_rows = pl.ds(pl.multiple_of(j * PEER_E1_PER_STEP, PEER_E1_PER_STEP), PEER_E1_PER_STEP)
    ht = ht_ref[...]
    n_parts = PEER_E1_PER_STEP // PEER_E1_PER_PART
    part_slice = lambda p: slice(p * PEER_E1_PER_PART * N_KEYS, (p + 1) * PEER_E1_PER_PART * N_KEYS)
    activation = lambda p: _gelu_tanh(jnp.dot(u_ref[part_slice(p), :], ht, preferred_element_type=jnp.float32))
    act_next = activation(0)
    for part in range(n_parts):
        act = act_next
        if part + 1 < n_parts:
            act_next = activation(part + 1)
        for tc in range(tm // LANE):
            cols = slice(tc * LANE, (tc + 1) * LANE)
            s1_rows = [s1_ref[h, e1_rows, cols] for h in range(PEER_HEADS)]
            g1_rows = [g1_ref[h, e1_rows, cols] for h in range(PEER_HEADS)]
            for ee in range(PEER_E1_PER_PART):
                e = part * PEER_E1_PER_PART + ee
                w = None
                for h in range(PEER_HEADS):
                    score = s2_ref[h, :, cols] + s1_rows[h][e:e + 1, :]
                    gate = e2_ref[h, :, cols] * g1_rows[h][e:e + 1, :]
                    term = jnp.where(score >= th_ref[h:h + 1, cols], gate, 0.0)
                    w = term if w is None else w + term
                rows = slice(ee * N_KEYS, (ee + 1) * N_KEYS)
                wg_ref[part, rows, cols] = (w * act[rows, cols]).astype(jnp.bfloat16)
        if part >= 1:
            o_ref[...] += jnp.dot(vt_ref[:, part_slice(part - 1)], wg_ref[part - 1],
                                  preferred_element_type=jnp.float32)
    o_ref[...] += jnp.dot(vt_ref[:, part_slice(n_parts - 1)], wg_ref[n_parts - 1],
                          preferred_element_type=jnp.float32)


def peer_experts(ht, u, vt, s1, s2, g1, e2, th):
    D, N = ht.shape
    E = u.shape[0]
    tm = PEER_TOKENS
    te = PEER_E1_PER_STEP * N_KEYS
    assert N % tm == 0 and E % te == 0
    big_spec = pl.BlockSpec((PEER_HEADS, N_KEYS, tm), lambda i, j: (0, 0, i))
    return pl.pallas_call(
        _peer_expert_kernel,
        grid=(N // tm, E // te),
        in_specs=[pl.BlockSpec((D, tm), lambda i, j: (0, i)),
                  pl.BlockSpec((te, D), lambda i, j: (j, 0)),
                  pl.BlockSpec((D, te), lambda i, j: (0, j)),
                  big_spec, big_spec, big_spec, big_spec,
                  pl.BlockSpec((PEER_HEADS, tm), lambda i, j: (0, i))],
        out_specs=pl.BlockSpec((D, tm), lambda i, j: (0, i)),
        out_shape=jax.ShapeDtypeStruct((D, N), jnp.float32),
        scratch_shapes=[pltpu.VMEM((PEER_E1_PER_STEP // PEER_E1_PER_PART, PEER_E1_PER_PART * N_KEYS, tm), jnp.bfloat16)],
        compiler_params=pltpu.CompilerParams(
            dimension_semantics=("arbitrary", "arbitrary"), vmem_limit_bytes=VMEM_LIMIT_BYTES),
        name="peer_experts",
    )(ht, u, vt, s1, s2, g1, e2, th)


def peer_ffn_tokens(h, wqt, keys, u, vt):
    ht = jnp.transpose(h).astype(jnp.bfloat16)
    s1, s2, g1, e2, th = peer_route(ht, wqt, keys)
    return jnp.transpose(peer_experts(ht, u, vt, s1, s2, g1, e2, th))


ZB_Q = Z_B // LANE
ZB_K = ZB_Q + B_HEADS
ZB_V = ZB_K + B_KV_HEADS
ZC_Q = ZB_V + B_KV_HEADS
ZC_K = ZC_Q + C_HEADS
ZC_V = ZC_K + C_HEADS
DIFF_Q_ROWS = 256

_DOT_NT = (((1,), (1,)), ((), ()))


def rope_lane_tables(n_tokens, head_dim):
    rows = n_tokens // GRID_W
    row_pos = jnp.repeat(jnp.arange(rows, dtype=jnp.float32), GRID_W)
    col_pos = jnp.tile(jnp.arange(GRID_W, dtype=jnp.float32), rows)
    n_freq = head_dim // 4
    freqs = ROPE_BASE ** (-jnp.arange(n_freq, dtype=jnp.float32) / n_freq)
    ang = jnp.concatenate([row_pos[:, None] * freqs, col_pos[:, None] * freqs], axis=-1)
    cos = jnp.repeat(jnp.cos(ang), 2, axis=-1)
    sin = jnp.repeat(jnp.sin(ang), 2, axis=-1) * jnp.tile(jnp.array([-1.0, 1.0], jnp.float32), head_dim // 2)
    reps = LANE // head_dim
    return jnp.tile(cos, (1, reps)), jnp.tile(sin, (1, reps))


def _rope(x, cos, sin_signed):
    lane = lax.broadcasted_iota(jnp.int32, x.shape, 1)
    partner = jnp.where(lane % 2 == 0, pltpu.roll(x, LANE - 1, axis=1), pltpu.roll(x, 1, axis=1))
    return x * cos + partner * sin_signed


def _bf(x):
    return x.astype(jnp.bfloat16)


def _diff_attn_kernel(lam_ref, q_ref, k_ref, v_ref, *rest, latent, lam_init):
    if latent:
        kc_ref, vc_ref, cq_ref, sq_ref, ck_ref, sk_ref, g_ref, o_ref, kr_ref = rest
    else:
        g_ref, o_ref, kr_ref = rest
    i = pl.program_id(2)

    @pl.when(i == 0)
    def _():
        k = k_ref[...]
        kr_ref[...] = _bf(_rope(k, ck_ref[...], sk_ref[...]) if latent else k)

    q = q_ref[...]
    if latent:
        q = _rope(q, cq_ref[...], sq_ref[...])
    lane = lax.broadcasted_iota(jnp.int32, q.shape, 1)
    scale = C_HEAD_DIM ** -0.5
    lam = lam_ref[0]
    kr = kr_ref[...]
    vb = _bf(v_ref[...])
    if latent:
        kcb, vcb = _bf(kc_ref[...]), _bf(vc_ref[...])

    def softmax_parts(qh):
        s = lax.dot_general(qh, kr, _DOT_NT, preferred_element_type=jnp.float32) * scale
        m = jnp.max(s, axis=-1, keepdims=True)
        if latent:
            sc = lax.dot_general(qh, kcb, _DOT_NT, preferred_element_type=jnp.float32) * scale
            m = jnp.maximum(m, jnp.max(sc, axis=-1, keepdims=True))
            ec = jnp.exp(sc - m)
        e = jnp.exp(s - m)
        z = jnp.sum(e, axis=-1, keepdims=True)
        if latent:
            z = z + jnp.sum(ec, axis=-1, keepdims=True)
            return e, ec, 1.0 / z
        return e, None, 1.0 / z

    e1, ec1, r1 = softmax_parts(_bf(jnp.where(lane < C_HEAD_DIM, q, 0.0)))
    e2, ec2, r2 = softmax_parts(_bf(jnp.where(lane >= C_HEAD_DIM, q, 0.0)))
    r2 = lam * r2
    o = jnp.dot(_bf(e1 * r1 - e2 * r2), vb, preferred_element_type=jnp.float32)
    if latent:
        o = o + jnp.dot(_bf(ec1 * r1 - ec2 * r2), vcb, preferred_element_type=jnp.float32)
    o = o * lax.rsqrt(jnp.mean(o * o, axis=-1, keepdims=True) + 1e-6) * g_ref[...] * (1.0 - lam_init)
    o_ref[...] = o


def diff_attention_mixer(z, lam, lam_init, subln_g, cache=None, layer=None, rope=None):
    B, T, _ = z.shape
    latent = cache is not None
    tq = min(DIFF_Q_ROWS, T)
    head = lambda base: pl.BlockSpec((None, tq, LANE), lambda b, h, i: (b, i, base + h))
    whole = lambda base: pl.BlockSpec((None, T, LANE), lambda b, h, i: (b, 0, base + h))
    in_specs = [pl.BlockSpec(memory_space=pltpu.SMEM), head(ZC_Q), whole(ZC_K), whole(ZC_V)]
    args = [lam.reshape(1), z, z, z]
    if latent:
        P = cache[0].shape[3]
        cspec = pl.BlockSpec((None, None, None, P, LANE), lambda b, h, i: (b, layer, h, 0, 0))
        tq_spec = pl.BlockSpec((tq, LANE), lambda b, h, i: (i, 0))
        tk_spec = pl.BlockSpec((T, LANE), lambda b, h, i: (0, 0))
        in_specs += [cspec, cspec, tq_spec, tq_spec, tk_spec, tk_spec]
        args += [cache[0], cache[1], rope[0], rope[1], rope[0], rope[1]]
    in_specs.append(pl.BlockSpec((1, LANE), lambda b, h, i: (0, 0)))
    args.append(subln_g.reshape(1, LANE))
    return pl.pallas_call(
        partial(_diff_attn_kernel, latent=latent, lam_init=lam_init),
        grid=(B, C_HEADS, T // tq),
        in_specs=in_specs,
        out_specs=pl.BlockSpec((None, tq, LANE), lambda b, h, i: (b, i, h)),
        out_shape=jax.ShapeDtypeStruct((B, T, C_HEADS * LANE), jnp.float32),
        scratch_shapes=[pltpu.VMEM((T, LANE), jnp.bfloat16)],
        compiler_params=pltpu.CompilerParams(
            dimension_semantics=("arbitrary", "arbitrary", "arbitrary"), vmem_limit_bytes=VMEM_LIMIT_BYTES),
        name="diff_attention",
    )(*args)


def _gqa_kernel(sink_ref, q0_ref, q1_ref, *rest, latent, n_tokens):
    if latent:
        (kp_ref, kn_ref, kx_ref, vp_ref, vn_ref, vx_ref, kc_ref, vc_ref,
         cq_ref, sq_ref, cp_ref, sp_ref, cx_ref, sx_ref, o_ref) = rest
    else:
        kc_ref, vc_ref, o_ref = rest
    kvh = pl.program_id(1)
    n = pl.program_id(2)
    scale = B_HEAD_DIM ** -0.5
    q0, q1 = q0_ref[...], q1_ref[...]
    if latent:
        q0 = _rope(q0, cq_ref[...], sq_ref[...])
        q1 = _rope(q1, cq_ref[...], sq_ref[...])
    q = _bf(jnp.concatenate([q0, q1], axis=0))
    row = lax.broadcasted_iota(jnp.int32, (2 * BLOCK, 1), 0)
    sink = jnp.where(row < BLOCK, sink_ref[2 * kvh], sink_ref[2 * kvh + 1])

    s_c = lax.dot_general(q, _bf(kc_ref[...]), _DOT_NT, preferred_element_type=jnp.float32) * scale
    m = jnp.maximum(jnp.max(s_c, axis=-1, keepdims=True), sink)
    if latent:
        kw = jnp.concatenate([_rope(kp_ref[...], cp_ref[...], sp_ref[...]),
                              _rope(kn_ref[...], cq_ref[...], sq_ref[...]),
                              _rope(kx_ref[...], cx_ref[...], sx_ref[...])], axis=0)
        vw = jnp.concatenate([vp_ref[...], vn_ref[...], vx_ref[...]], axis=0)
        s_w = lax.dot_general(q, _bf(kw), _DOT_NT, preferred_element_type=jnp.float32) * scale
        qpos = n * BLOCK + lax.broadcasted_iota(jnp.int32, s_w.shape, 0) % BLOCK
        kpos = (n - 1) * BLOCK + lax.broadcasted_iota(jnp.int32, s_w.shape, 1)
        valid = (jnp.abs(qpos - kpos) <= WINDOW) & (kpos >= 0) & (kpos < n_tokens)
        s_w = jnp.where(valid, s_w, NEG_INF)
        m = jnp.maximum(m, jnp.max(s_w, axis=-1, keepdims=True))
        e_w = jnp.exp(s_w - m)
    e_c = jnp.exp(s_c - m)
    zsum = jnp.sum(e_c, axis=-1, keepdims=True) + jnp.exp(sink - m)
    o = jnp.dot(_bf(e_c), _bf(vc_ref[...]), preferred_element_type=jnp.float32)
    if latent:
        zsum = zsum + jnp.sum(e_w, axis=-1, keepdims=True)
        o = o + jnp.dot(_bf(e_w), _bf(vw), preferred_element_type=jnp.float32)
    o = o / zsum
    o_ref[:, 0:LANE] = o[0:BLOCK]
    o_ref[:, LANE:2 * LANE] = o[BLOCK:2 * BLOCK]


def gqa_mixer(z, sink, cache=None, layer=None, rope=None):
    B, T, _ = z.shape
    latent = cache is not None
    nb = T // BLOCK
    blk = lambda base, off: pl.BlockSpec(
        (None, BLOCK, LANE), lambda b, kvh, n: (b, jnp.clip(n + off, 0, nb - 1), base + kvh))
    qspec = lambda g: pl.BlockSpec((None, BLOCK, LANE), lambda b, kvh, n: (b, n, ZB_Q + 2 * kvh + g))
    in_specs = [pl.BlockSpec(memory_space=pltpu.SMEM), qspec(0), qspec(1)]
    args = [sink, z, z]
    if latent:
        P = cache[0].shape[3]
        cspec = pl.BlockSpec((None, None, None, P, LANE), lambda b, kvh, n: (b, layer, kvh, 0, 0))
        tab = lambda off: pl.BlockSpec((BLOCK, LANE), lambda b, kvh, n: (jnp.clip(n + off, 0, nb - 1), 0))
        in_specs += [blk(ZB_K, -1), blk(ZB_K, 0), blk(ZB_K, 1), blk(ZB_V, -1), blk(ZB_V, 0), blk(ZB_V, 1),
                     cspec, cspec, tab(0), tab(0), tab(-1), tab(-1), tab(1), tab(1)]
        args += [z] * 6 + [cache[0], cache[1]] + [rope[0], rope[1]] * 3
    else:
        whole = lambda base: pl.BlockSpec((None, T, LANE), lambda b, kvh, n: (b, 0, base + kvh))
        in_specs += [whole(ZB_K), whole(ZB_V)]
        args += [z, z]
    return pl.pallas_call(
        partial(_gqa_kernel, latent=latent, n_tokens=T),
        grid=(B, B_KV_HEADS, nb),
        in_specs=in_specs,
        out_specs=pl.BlockSpec((None, BLOCK, 2 * LANE), lambda b, kvh, n: (b, n, kvh)),
        out_shape=jax.ShapeDtypeStruct((B, T, B_HEADS * LANE), jnp.float32),
        compiler_params=pltpu.CompilerParams(
            dimension_semantics=("arbitrary", "arbitrary", "arbitrary"), vmem_limit_bytes=VMEM_LIMIT_BYTES),
        name="gqa_attention",
    )(*args)


def split_cols(z, sizes):
    return jnp.split(z, [int(s) for s in np.cumsum(sizes)[:-1]], axis=-1)


def layer_norm(x, g=None, b=None):
    mu = jnp.mean(x, axis=-1, keepdims=True)
    var = jnp.mean(jnp.square(x - mu), axis=-1, keepdims=True)
    y = (x - mu) * lax.rsqrt(var + LN_EPS)
    if g is not None:
        y = y * g + b
    return y


def centred_conv3(x, w):
    xp = jnp.pad(x, ((0, 0), (1, 1), (0, 0)))
    return xp[:, :-2] * w[0] + xp[:, 1:-1] * w[1] + xp[:, 2:] * w[2]


def rwkv7_mixer(za, S0_f, S0_b, lp):
    Bsz, T, _ = za.shape
    r, k, v, g_d, wd_f, wd_b, ad_f, ad_b = split_cols(
        za, [A_WIDTH, A_WIDTH, A_WIDTH, G_LORA, W_LORA, W_LORA, ICL_LORA, ICL_LORA])
    heads = lambda t: t.reshape(Bsz, T, A_HEADS, A_HEAD_DIM)
    wz = jnp.stack([mm3(jnp.tanh(wd), lp['a_w_up'][d]) for d, wd in enumerate((wd_f, wd_b))], axis=0)
    az = jnp.stack([mm3(ad, lp['a_a_up'][d]) for d, ad in enumerate((ad_f, ad_b))], axis=0)
    s0t = jnp.swapaxes(jnp.stack([S0_f, S0_b], axis=0), -1, -2)
    ys, sfin_t = rwkv7_chunked(za, wz, az, lp['a_w0'], lp['a_a0'], lp['a_k_k'], lp['a_k_a'], s0t)
    states = jnp.swapaxes(sfin_t, -1, -2)
    y = heads(ys[0] + ys[1])
    mu = jnp.mean(y, axis=-1, keepdims=True)
    var = jnp.mean(jnp.square(y - mu), axis=-1, keepdims=True)
    y = ((y - mu) * lax.rsqrt(var + A_GN_EPS)).reshape(Bsz, T, A_WIDTH) * lp['a_gn_g'] + lp['a_gn_b']
    icl = jax.nn.sigmoid(az + lp['a_a0'][:, None, None, :])
    kd_sum = k * (2.0 + (icl[0] + icl[1] - 2.0) * lp['a_k_a'])
    bonus = jnp.sum(heads(r) * heads(kd_sum) * lp['a_r_k'], axis=-1, keepdims=True) * heads(v)
    g = mm3(jax.nn.sigmoid(g_d), lp['a_g_up'])
    out = (y + bonus.reshape(Bsz, T, A_WIDTH)) * g
    return out, states[0], states[1]


def diff_lambda(lv, layer):
    lam_init = 0.8 - 0.6 * math.exp(-0.3 * layer)
    lam = jnp.exp(jnp.sum(lv[0] * lv[1])) - jnp.exp(jnp.sum(lv[2] * lv[3])) + lam_init
    return lam, lam_init


def adaln_params(cond, lp):
    mod = mm(jax.nn.silu(cond), lp['w_mod']) + lp['b_mod']
    return jnp.split(mod[:, None, :], 6, axis=-1)


def mixer_inputs(x, shift, scale, lp):
    z = ln_mod_matmul(x, scale, shift, lp['w_in'])
    return z, centred_conv3(z[..., Z_A:Z_A + A_COLS], lp['a_conv'])


def merge_mixers(x, yA, yB, yC, z, gate1, shift2, scale2, lp):
    n = x.shape[0] * x.shape[1]
    flat = lambda t: t.reshape(n, t.shape[-1])
    merged = merge_gated(flat(yA), flat(yB), flat(yC), flat(z), lp['p_a'], lp['p_b'], lp['p_c'])
    return out_proj_norm(merged, x, gate1, scale2, shift2, lp['ln1_g'], lp['ln1_b'], lp['w_out'])


def peer_and_norm(parts, lp):
    sizes = [x.shape[0] * x.shape[1] for x, _, _ in parts]
    h_all = jnp.concatenate([h.reshape(-1, D_MODEL) for _, h, _ in parts], axis=0)
    y_all = peer_ffn_tokens(h_all, lp['peer_wqt'], lp['peer_keys'], lp['peer_u'], lp['peer_vt'])
    outs, start = [], 0
    for (x, _, gate2), n in zip(parts, sizes):
        y = y_all[start:start + n].reshape(x.shape)
        start += n
        outs.append(layer_norm(DEEPNORM_ALPHA * x + gate2 * y, lp['ln2_g'], lp['ln2_b']))
    return outs


def context_layer(x, cond, lp, layer):
    Bsz, L, _ = x.shape
    shift1, scale1, gate1, shift2, scale2, gate2 = adaln_params(cond, lp)
    z, zA = mixer_inputs(x, shift1, scale1, lp)
    S0 = jnp.zeros((Bsz, A_HEADS, A_HEAD_DIM, A_HEAD_DIM), jnp.float32)
    yA, S_f, S_b = rwkv7_mixer(zA, S0, S0, lp)
    yB = gqa_mixer(z, lp['b_sink'])
    lam, lam_init = diff_lambda(lp['c_lam'], layer)
    yC = diff_attention_mixer(z, lam, lam_init, lp['c_subln'])
    x_mid, h_mid = merge_mixers(x, yA, yB, yC, z, gate1, shift2, scale2, lp)

    def cache_layout(first_block, n_heads):
        t = z[..., first_block * LANE:(first_block + n_heads) * LANE].reshape(Bsz, L, n_heads, LANE)
        return jnp.transpose(t, (0, 2, 1, 3))
    ctx = (S_f, S_b, cache_layout(ZB_K, B_KV_HEADS), cache_layout(ZB_V, B_KV_HEADS),
           cache_layout(ZC_K, C_HEADS), cache_layout(ZC_V, C_HEADS))
    return (x_mid, h_mid, gate2), ctx


def latent_layer(x, cond, lp, layer, S_f0, S_b0, cache_b, cache_c, rope_b, rope_c):
    shift1, scale1, gate1, shift2, scale2, gate2 = adaln_params(cond, lp)
    z, zA = mixer_inputs(x, shift1, scale1, lp)
    yA, _, _ = rwkv7_mixer(zA, S_f0, S_b0, lp)
    yB = gqa_mixer(z, lp['b_sink'], cache=cache_b, layer=layer, rope=rope_b)
    lam, lam_init = diff_lambda(lp['c_lam'], layer)
    yC = diff_attention_mixer(z, lam, lam_init, lp['c_subln'], cache=cache_c, layer=layer, rope=rope_c)
    x_mid, h_mid = merge_mixers(x, yA, yB, yC, z, gate1, shift2, scale2, lp)
    return x_mid, h_mid, gate2


_BF16_WEIGHTS = ('w_mod', 'w_in', 'a_w_up', 'a_a_up', 'a_g_up', 'p_a', 'p_b', 'p_c', 'w_out', 'peer_u')


def kernel(x_prompt, x_sample, state_a_fwd, state_a_bwd, cache_b_k, cache_b_v, cache_c_k, cache_c_v, c, c_ctx, w_mod, b_mod, w_in, a_conv, a_w0, a_w_up, a_a0, a_a_up, a_g_up, a_k_k, a_k_a, a_r_k, a_gn_g, a_gn_b, b_sink, c_lam, c_subln, p_a, p_b, p_c, w_out, ln1_g, ln1_b, ln2_g, ln2_b, peer_wq, peer_keys, peer_u, peer_v):
    params = dict(w_mod=w_mod, b_mod=b_mod, w_in=w_in, a_conv=a_conv, a_w0=a_w0, a_w_up=a_w_up, a_a0=a_a0,
                  a_a_up=a_a_up, a_g_up=a_g_up, a_k_k=a_k_k, a_k_a=a_k_a, a_r_k=a_r_k, a_gn_g=a_gn_g,
                  a_gn_b=a_gn_b, b_sink=b_sink, c_lam=c_lam, c_subln=c_subln, p_a=p_a, p_b=p_b, p_c=p_c,
                  w_out=w_out, ln1_g=ln1_g, ln1_b=ln1_b, ln2_g=ln2_g, ln2_b=ln2_b, peer_wq=peer_wq,
                  peer_keys=peer_keys, peer_u=peer_u, peer_v=peer_v)
    for name in _BF16_WEIGHTS:
        params[name] = params[name].astype(jnp.bfloat16)
    w_in_bf = params['w_in']
    params['w_in'] = jnp.concatenate([w_in_bf[..., IN_COLS - 3 * D_MODEL:], w_in_bf[..., :IN_COLS - 3 * D_MODEL]], axis=-1)
    params['peer_wqt'] = jnp.swapaxes(params.pop('peer_wq'), 1, 2).astype(jnp.bfloat16)
    params['peer_vt'] = jnp.swapaxes(params.pop('peer_v'), 1, 2).astype(jnp.bfloat16)
    n_lat = x_sample.shape[1]
    rope_b = rope_lane_tables(n_lat, B_HEAD_DIM)
    rope_c = rope_lane_tables(n_lat, C_HEAD_DIM)
    cond_ctx = jnp.broadcast_to(c_ctx[None, :], (x_prompt.shape[0], D_MODEL))
    y_prompt, y_sample = x_prompt, x_sample
    new = [[], [], [], [], [], []]
    for layer in range(DEPTH):
        lp = {name: val[layer] for name, val in params.items()}
        part_ctx, ctx = context_layer(y_prompt, cond_ctx, lp, layer)
        for lst, t in zip(new, ctx):
            lst.append(t)
        part_lat = latent_layer(y_sample, c, lp, layer,
                                state_a_fwd[:, layer], state_a_bwd[:, layer],
                                (cache_b_k, cache_b_v), (cache_c_k, cache_c_v), rope_b, rope_c)
        y_prompt, y_sample = peer_and_norm([part_ctx, part_lat], lp)
    return (y_prompt, y_sample) + tuple(jnp.stack(lst, axis=1) for lst in new)
```

```python
import math
from functools import partial

import jax
import jax.numpy as jnp
import numpy as np
from jax import lax
from jax.experimental import pallas as pl
from jax.experimental.pallas import tpu as pltpu

D_MODEL = 2048
DEPTH = 4
GRID_W = 64
BLOCK = 128
A_HEADS = 16
A_HEAD_DIM = 64
A_WIDTH = A_HEADS * A_HEAD_DIM
W_LORA = 64
ICL_LORA = 64
G_LORA = 128
A_COLS = 3 * A_WIDTH + G_LORA + 2 * W_LORA + 2 * ICL_LORA
A_GN_EPS = 64e-5
B_HEADS = 4
B_KV_HEADS = 2
B_HEAD_DIM = 128
B_WIDTH = B_HEADS * B_HEAD_DIM
B_KV_WIDTH = B_KV_HEADS * B_HEAD_DIM
WINDOW = 128
C_HEADS = 4
C_HEAD_DIM = 64
C_WIDTH = C_HEADS * 2 * C_HEAD_DIM
IN_COLS = A_COLS + B_WIDTH + 2 * B_KV_WIDTH + 3 * C_WIDTH + 3 * D_MODEL
PEER_HEADS = 8
N_KEYS = 128
PEER_QDIM = 256
PEER_TOPK = 16
ROPE_BASE = 10000.0
LN_EPS = 1e-5
NEG_INF = -1e30
DEEPNORM_ALPHA = (2 * DEPTH) ** 0.25

LANE = 128
VMEM_LIMIT_BYTES = 56 * 1024 * 1024


def _mm_kernel(x_ref, w_ref, o_ref):
    o_ref[...] = jnp.dot(x_ref[...].astype(jnp.bfloat16), w_ref[...],
                         preferred_element_type=jnp.float32)


def _pick_tile(n, cap, unit):
    if n <= cap:
        return n
    best = None
    for t in range(unit, cap + 1, unit):
        if n % t == 0:
            best = t
    assert best is not None, (n, cap, unit)
    return best


def mm(x, w):
    M, K = x.shape
    N = w.shape[1]
    tm = _pick_tile(M, 512, 8)
    tn = _pick_tile(N, 1024, LANE)
    return pl.pallas_call(
        _mm_kernel,
        grid=(N // tn, M // tm),
        in_specs=[pl.BlockSpec((tm, K), lambda j, i: (i, 0)),
                  pl.BlockSpec((K, tn), lambda j, i: (0, j))],
        out_specs=pl.BlockSpec((tm, tn), lambda j, i: (i, j)),
        out_shape=jax.ShapeDtypeStruct((M, N), jnp.float32),
        compiler_params=pltpu.CompilerParams(
            dimension_semantics=("arbitrary", "arbitrary"),
            vmem_limit_bytes=VMEM_LIMIT_BYTES),
        name="mm",
    )(x, w)


def mm3(x, w):
    B, T, K = x.shape
    return mm(x.reshape(B * T, K), w).reshape(B, T, w.shape[1])


ROW_TILE = 1024
NORM_ROW_TILE = 512
Z_GATES = 0
Z_A = 3 * D_MODEL
Z_B = Z_A + A_COLS
Z_C = Z_B + B_WIDTH + 2 * B_KV_WIDTH


def _row_blocks(n_batch, n_time, rows):
    tt = min(n_time, rows)
    bt = rows // tt
    assert n_time % tt == 0 and n_batch % bt == 0, (n_batch, n_time, rows)
    return bt, tt


def _layer_norm_rows(x):
    mu = jnp.mean(x, axis=-1, keepdims=True)
    xc = x - mu
    return xc * lax.rsqrt(jnp.mean(xc * xc, axis=-1, keepdims=True) + LN_EPS)


def _ln_mod_mm_kernel(x_ref, scale_ref, shift_ref, w_ref, o_ref, h_ref):
    bt, tt, K = x_ref.shape

    @pl.when(pl.program_id(2) == 0)
    def _():
        h = _layer_norm_rows(x_ref[...]) * (1.0 + scale_ref[...]) + shift_ref[...]
        h_ref[...] = h.reshape(bt * tt, K).astype(jnp.bfloat16)

    o = jnp.dot(h_ref[...], w_ref[...], preferred_element_type=jnp.float32)
    o_ref[...] = o.reshape(bt, tt, o.shape[-1])


def ln_mod_matmul(x, scale, shift, w):
    B, T, K = x.shape
    N = w.shape[1]
    bt, tt = _row_blocks(B, T, ROW_TILE)
    tn = _pick_tile(N, 1024, LANE)
    vec = pl.BlockSpec((bt, 1, K), lambda b, t, j: (b, 0, 0))
    return pl.pallas_call(
        _ln_mod_mm_kernel,
        grid=(B // bt, T // tt, N // tn),
        in_specs=[pl.BlockSpec((bt, tt, K), lambda b, t, j: (b, t, 0)), vec, vec,
                  pl.BlockSpec((K, tn), lambda b, t, j: (0, j))],
        out_specs=pl.BlockSpec((bt, tt, tn), lambda b, t, j: (b, t, j)),
        out_shape=jax.ShapeDtypeStruct((B, T, N), jnp.float32),
        scratch_shapes=[pltpu.VMEM((bt * tt, K), jnp.bfloat16)],
        compiler_params=pltpu.CompilerParams(
            dimension_semantics=("arbitrary", "arbitrary", "arbitrary"), vmem_limit_bytes=VMEM_LIMIT_BYTES),
        name="ln_mod_matmul",
    )(x, scale, shift, w)


def _merge_kernel(ya_ref, yb_ref, yc_ref, ga_ref, gb_ref, gc_ref, pa_ref, pb_ref, pc_ref, o_ref,
                  ya_bf, yb_bf, yc_bf):
    @pl.when(pl.program_id(1) == 0)
    def _():
        ya_bf[...] = ya_ref[...].astype(jnp.bfloat16)
        yb_bf[...] = yb_ref[...].astype(jnp.bfloat16)
        yc_bf[...] = yc_ref[...].astype(jnp.bfloat16)

    f = lambda y, p: jnp.dot(y[...], p[...], preferred_element_type=jnp.float32)
    o = (jax.nn.sigmoid(ga_ref[...]) * f(ya_bf, pa_ref) + jax.nn.sigmoid(gb_ref[...]) * f(yb_bf, pb_ref)
         + jax.nn.sigmoid(gc_ref[...]) * f(yc_bf, pc_ref))
    o_ref[...] = o.astype(jnp.bfloat16)


def merge_gated(yA, yB, yC, z, p_a, p_b, p_c):
    N = yA.shape[0]
    D = p_a.shape[1]
    tm = _pick_tile(N, ROW_TILE, 8)
    tn = 512
    assert Z_GATES == 0 and D % tn == 0
    ysp = lambda y: pl.BlockSpec((tm, y.shape[1]), lambda i, j: (i, 0))
    gsp = lambda k: pl.BlockSpec((tm, tn), lambda i, j: (i, k * (D // tn) + j))
    psp = lambda p: pl.BlockSpec((p.shape[0], tn), lambda i, j: (0, j))
    return pl.pallas_call(
        _merge_kernel,
        grid=(N // tm, D // tn),
        in_specs=[ysp(yA), ysp(yB), ysp(yC), gsp(0), gsp(1), gsp(2), psp(p_a), psp(p_b), psp(p_c)],
        out_specs=pl.BlockSpec((tm, tn), lambda i, j: (i, j)),
        out_shape=jax.ShapeDtypeStruct((N, D), jnp.bfloat16),
        scratch_shapes=[pltpu.VMEM((tm, y.shape[1]), jnp.bfloat16) for y in (yA, yB, yC)],
        compiler_params=pltpu.CompilerParams(
            dimension_semantics=("arbitrary", "arbitrary"), vmem_limit_bytes=VMEM_LIMIT_BYTES),
        name="merge_gated",
    )(yA, yB, yC, z, z, z, p_a, p_b, p_c)


def _out_norm_kernel(m_ref, x_ref, gate_ref, scale_ref, shift_ref, lng_ref, lnb_ref, w_ref, x1_ref, ht_ref):
    bt, tt, D = x_ref.shape
    y = jnp.dot(m_ref[...].reshape(bt * tt, D), w_ref[...], preferred_element_type=jnp.float32)
    x1 = _layer_norm_rows(DEEPNORM_ALPHA * x_ref[...] + gate_ref[...] * y.reshape(bt, tt, D))
    x1 = x1 * lng_ref[...] + lnb_ref[...]
    x1_ref[...] = x1
    h = _layer_norm_rows(x1) * (1.0 + scale_ref[...]) + shift_ref[...]
    ht_ref[...] = jnp.transpose(h.reshape(bt * tt, D)).astype(jnp.bfloat16)


def out_proj_norm(merged, x, gate1, scale2, shift2, ln_g, ln_b, w_out):
    B, T, D = x.shape
    bt, tt = _row_blocks(B, T, NORM_ROW_TILE)
    rows = pl.BlockSpec((bt, tt, D), lambda b, t: (b, t, 0))
    vec = pl.BlockSpec((bt, 1, D), lambda b, t: (b, 0, 0))
    par = pl.BlockSpec((1, 1, D), lambda b, t: (0, 0, 0))
    return pl.pallas_call(
        _out_norm_kernel,
        grid=(B // bt, T // tt),
        in_specs=[rows, rows, vec, vec, vec, par, par, pl.BlockSpec((D, D), lambda b, t: (0, 0))],
        out_specs=[rows, pl.BlockSpec((D, bt * tt), lambda b, t: (0, b * (T // tt) + t))],
        out_shape=[jax.ShapeDtypeStruct((B, T, D), jnp.float32), jax.ShapeDtypeStruct((D, B * T), jnp.bfloat16)],
        compiler_params=pltpu.CompilerParams(
            dimension_semantics=("arbitrary", "arbitrary"), vmem_limit_bytes=VMEM_LIMIT_BYTES),
        name="out_proj_norm",
    )(merged.reshape(B, T, D), x, gate1, scale2, shift2, ln_g.reshape(1, 1, D), ln_b.reshape(1, 1, D), w_out)


def _peer_norm_kernel(yt_ref, x_ref, gate_ref, lng_ref, lnb_ref, o_ref):
    bt, tt, D = x_ref.shape
    y = jnp.transpose(yt_ref[...]).reshape(bt, tt, D)
    x = _layer_norm_rows(DEEPNORM_ALPHA * x_ref[...] + gate_ref[...] * y)
    o_ref[...] = x * lng_ref[...] + lnb_ref[...]


def peer_residual_norm(yt, x, gate2, ln_g, ln_b):
    B, T, D = x.shape
    bt, tt = _row_blocks(B, T, NORM_ROW_TILE)
    rows = pl.BlockSpec((bt, tt, D), lambda b, t: (b, t, 0))
    par = pl.BlockSpec((1, 1, D), lambda b, t: (0, 0, 0))
    return pl.pallas_call(
        _peer_norm_kernel,
        grid=(B // bt, T // tt),
        in_specs=[pl.BlockSpec((D, bt * tt), lambda b, t: (0, b * (T // tt) + t)), rows,
                  pl.BlockSpec((bt, 1, D), lambda b, t: (b, 0, 0)), par, par],
        out_specs=rows,
        out_shape=jax.ShapeDtypeStruct((B, T, D), jnp.float32),
        compiler_params=pltpu.CompilerParams(
            dimension_semantics=("arbitrary", "arbitrary"), vmem_limit_bytes=VMEM_LIMIT_BYTES),
        name="peer_residual_norm",
    )(yt, x, gate2, ln_g.reshape(1, 1, D), ln_b.reshape(1, 1, D))


RWKV_CHUNK = 64
RWKV_HEADS_PER_STEP = 16

_NN = (((2,), (1,)), ((0,), (0,)))
_NT = (((2,), (2,)), ((0,), (0,)))
_TN = (((1,), (1,)), ((0,), (0,)))


def _split2(x):
    hi = x.astype(jnp.bfloat16)
    lo = (x - hi.astype(jnp.float32)).astype(jnp.bfloat16)
    return hi, lo


def _dot3(a, b, dims=_NN):
    f = lambda x, y: lax.dot_general(x, y, dims, preferred_element_type=jnp.float32)
    return f(a[0], b[0]) + (f(a[0], b[1]) + f(a[1], b[0]))


def _dot1(a, b, dims=_NN):
    return lax.dot_general(a[0], b[0], dims, preferred_element_type=jnp.float32)


def _split_heads(x):
    n = A_HEAD_DIM
    return jnp.stack([x[:, h * n:(h + 1) * n] for h in range(x.shape[1] // n)], axis=0)


def _rwkv_chunk_kernel(zr_ref, zk_ref, zv_ref, wz_ref, az_ref, w0_ref, a0_ref, kk_ref, ka_ref, s0_ref,
                       y_ref, sfin_ref, st_ref):
    d = pl.program_id(0)
    c = pl.program_id(3)
    n_chunks = pl.num_programs(3)
    HB = st_ref.shape[0]
    C = zr_ref.shape[0]

    @pl.when(c == 0)
    def _():
        st_ref[...] = s0_ref[...]

    k_all = zk_ref[...]
    lw_all = -math.exp(-0.5) * jax.nn.sigmoid(wz_ref[...] + w0_ref[...])
    icl_all = jax.nn.sigmoid(az_ref[...] + a0_ref[...])
    r, v = _split_heads(zr_ref[...]), _split_heads(zv_ref[...])
    lw = _split_heads(lw_all)
    k = _split_heads(k_all * (1.0 + (icl_all - 1.0) * ka_ref[...]))
    kk = _split_heads(k_all * kk_ref[...])
    kk = kk * lax.rsqrt(jnp.sum(kk * kk, axis=-1, keepdims=True) + 1e-12)
    a = -kk
    b = kk * _split_heads(icl_all)

    ti = lax.broadcasted_iota(jnp.int32, (HB, C, C), 1)
    si = lax.broadcasted_iota(jnp.int32, (HB, C, C), 2)
    diff = (si - ti) * (1 - 2 * d)
    m_incl = diff <= 0
    m_strict = diff < 0
    m_incl_bf = jnp.where(m_incl, 1.0, 0.0).astype(jnp.bfloat16)
    eye = jnp.where(diff == 0, 1.0, 0.0).astype(jnp.float32)

    st = st_ref[...]

    lw_hi, lw_lo = _split2(lw)
    lw_lo2 = (lw - lw_hi.astype(jnp.float32) - lw_lo.astype(jnp.float32)).astype(jnp.bfloat16)
    f = lambda y: lax.dot_general(m_incl_bf, y, _NN, preferred_element_type=jnp.float32)
    cum = f(lw_hi) + (f(lw_lo) + f(lw_lo2))
    total = jnp.sum(lw, axis=1, keepdims=True)
    e_cum = jnp.exp(cum)
    e_inv = jnp.exp(-cum)
    e_tot = jnp.exp(total)
    hi = lambda x: (x.astype(jnp.bfloat16),)
    at = hi(a * jnp.exp(cum - lw))
    rt = _split2(r * e_cum)
    bt_f = b * e_inv
    kt_f = k * e_inv
    bt, kt = hi(bt_f), hi(kt_f)
    bh, kh = _split2(bt_f * e_tot), _split2(kt_f * e_tot)
    vs = _split2(v)

    l_ab = jnp.where(m_strict, _dot1(at, bt, _NT), 0.0)
    l_ak = jnp.where(m_strict, _dot1(at, kt, _NT), 0.0)
    m_rb = jnp.where(m_incl, _dot1(rt, bt, _NT), 0.0)
    m_rk = jnp.where(m_incl, _dot1(rt, kt, _NT), 0.0)

    tm = eye + l_ab
    lp = hi(l_ab)
    for _ in range(int(math.log2(C)) - 1):
        lp = hi(_dot1(lp, lp))
        tm = tm + _dot1(hi(tm), lp)
    tms = hi(tm)
    p = _split2(_dot1(tms, at))
    q = _dot1(tms, hi(_dot1(hi(l_ak), vs)))

    sts = _split2(st)
    u = _split2(_dot3(p, sts) + q)
    y = _dot3(rt, sts) + _dot3(_split2(m_rb), u) + _dot3(_split2(m_rk), vs)
    y_ref[...] = jnp.concatenate([y[h] for h in range(HB)], axis=-1)
    st_ref[...] = jnp.swapaxes(e_tot, 1, 2) * st + _dot3(bh, u, _TN) + _dot3(kh, vs, _TN)

    @pl.when(c == n_chunks - 1)
    def _():
        sfin_ref[...] = st_ref[...]


def rwkv7_chunked(zc, wz, az, w0, a0, k_k, k_a, s0t):
    D, B, T, W = wz.shape
    N, H = A_HEAD_DIM, A_HEADS
    C, HB = RWKV_CHUNK, RWKV_HEADS_PER_STEP
    G = H // HB
    lanes = HB * N
    assert T % C == 0 and H % HB == 0 and W == H * N
    n_chunks = T // C
    chunk = lambda d, c: c + d * (n_chunks - 1 - 2 * c)
    zcol = lambda base: pl.BlockSpec((None, C, lanes), lambda d, bb, g, c: (bb, chunk(d, c), base * G + g))
    per_dir = pl.BlockSpec((None, None, C, lanes), lambda d, bb, g, c: (d, bb, chunk(d, c), g))
    dir_row = pl.BlockSpec((None, 1, lanes), lambda d, bb, g, c: (d, 0, g))
    row = pl.BlockSpec((1, lanes), lambda d, bb, g, c: (0, g))
    state = pl.BlockSpec((None, None, HB, N, N), lambda d, bb, g, c: (d, bb, g, 0, 0))
    return pl.pallas_call(
        _rwkv_chunk_kernel,
        grid=(D, B, G, n_chunks),
        in_specs=[zcol(0), zcol(1), zcol(2), per_dir, per_dir, dir_row, dir_row, row, row, state],
        out_specs=[per_dir, state],
        out_shape=[jax.ShapeDtypeStruct((D, B, T, W), jnp.float32),
                   jax.ShapeDtypeStruct((D, B, H, N, N), jnp.float32)],
        scratch_shapes=[pltpu.VMEM((HB, N, N), jnp.float32)],
        compiler_params=pltpu.CompilerParams(
            dimension_semantics=("arbitrary", "arbitrary", "arbitrary", "arbitrary"),
            vmem_limit_bytes=VMEM_LIMIT_BYTES),
        name="rwkv7_chunked",
    )(zc, zc, zc, wz, az, w0.reshape(D, 1, W), a0.reshape(D, 1, W), k_k.reshape(1, W), k_a.reshape(1, W), s0t)


PEER_ROUTE_TOKENS = 256
PEER_TOKENS = 512
PEER_E1_PER_STEP = 8
PEER_E1_PER_PART = 2
PEER_CAND_ROWS = 56


def _top_rows(s_ref, n_rows, k, emit):
    for a in range(k):
        s = s_ref[0:n_rows, :]
        mx = jnp.max(s, axis=0, keepdims=True)
        emit(a, mx)
        if a + 1 < k:
            s_ref[0:n_rows, :] = jnp.where(s == mx, NEG_INF, s)


def _peer_route_kernel(ht_ref, wqt_ref, k1_ref, k2_ref, s1_ref, s2_ref, g1_ref, e2_ref, th_ref,
                       work_ref, t1_ref, t2_ref, cand_ref):
    half = PEER_QDIM // 2
    qt = jnp.dot(wqt_ref[...], ht_ref[...], preferred_element_type=jnp.float32)
    for h in range(PEER_HEADS):
        for which, (k_ref, s_out, t_ref) in enumerate(((k1_ref, s1_ref, t1_ref), (k2_ref, s2_ref, t2_ref))):
            q = qt[h * PEER_QDIM + which * half: h * PEER_QDIM + (which + 1) * half, :]
            q_hi, q_lo = _split2(q)
            k_hi, k_lo = _split2(k_ref[h])
            f = lambda x, y: jnp.dot(x, y, preferred_element_type=jnp.float32)
            s = f(k_hi, q_hi) + (f(k_hi, q_lo) + f(k_lo, q_hi))
            s_out[h] = s
            work_ref[...] = s

            def emit(a, mx, t_ref=t_ref):
                t_ref[a:a + 1, :] = mx
            _top_rows(work_ref, N_KEYS, PEER_TOPK, emit)
        t1 = t1_ref[...]
        t2 = t2_ref[...]
        m1 = t1[0:1, :]
        m2 = t2[0:1, :]
        row = 0
        for a in range(PEER_TOPK):
            nb = PEER_TOPK // (a + 1)
            cand_ref[row:row + nb, :] = t1[a:a + 1, :] + t2[0:nb, :]
            row += nb
        cand_ref[row:PEER_CAND_ROWS, :] = jnp.full((PEER_CAND_ROWS - row, t1.shape[1]), NEG_INF, jnp.float32)
        acc = {}

        def emit_c(a, mx):
            e = jnp.exp(mx - (m1 + m2))
            acc['z'] = e if a == 0 else acc['z'] + e
            acc['th'] = mx
        _top_rows(cand_ref, PEER_CAND_ROWS, PEER_TOPK, emit_c)
        th_ref[h:h + 1, :] = acc['th']
        g1_ref[h] = jnp.exp(s1_ref[h] - m1) / acc['z']
        e2_ref[h] = jnp.exp(s2_ref[h] - m2)


def peer_route(ht, wqt, keys):
    D, N = ht.shape
    tm = PEER_ROUTE_TOKENS
    assert N % tm == 0
    big = jax.ShapeDtypeStruct((PEER_HEADS, N_KEYS, N), jnp.float32)
    big_spec = pl.BlockSpec((PEER_HEADS, N_KEYS, tm), lambda i: (0, 0, i))
    key_spec = pl.BlockSpec((PEER_HEADS, N_KEYS, PEER_QDIM // 2), lambda i: (0, 0, 0))
    return pl.pallas_call(
        _peer_route_kernel,
        grid=(N // tm,),
        in_specs=[pl.BlockSpec((D, tm), lambda i: (0, i)),
                  pl.BlockSpec((PEER_HEADS * PEER_QDIM, D), lambda i: (0, 0)),
                  key_spec, key_spec],
        out_specs=[big_spec, big_spec, big_spec, big_spec, pl.BlockSpec((PEER_HEADS, tm), lambda i: (0, i))],
        out_shape=[big, big, big, big, jax.ShapeDtypeStruct((PEER_HEADS, N), jnp.float32)],
        scratch_shapes=[pltpu.VMEM((N_KEYS, tm), jnp.float32),
                        pltpu.VMEM((PEER_TOPK, tm), jnp.float32),
                        pltpu.VMEM((PEER_TOPK, tm), jnp.float32),
                        pltpu.VMEM((PEER_CAND_ROWS, tm), jnp.float32)],
        compiler_params=pltpu.CompilerParams(
            dimension_semantics=("arbitrary",), vmem_limit_bytes=VMEM_LIMIT_BYTES),
        name="peer_route",
    )(ht, wqt, keys[0], keys[1])


def _gelu_tanh(x):
    return 0.5 * x * (1.0 + jnp.tanh(math.sqrt(2.0 / math.pi) * (x + 0.044715 * (x * x * x))))


def _peer_expert_kernel(ht_ref, u_ref, vt_ref, s1_ref, s2_ref, g1_ref, e2_ref, th_ref, o_ref, wg_ref):
    j = pl.program_id(1)
    tm = ht_ref.shape[1]

    @pl.when(j == 0)
    def _():
        o_ref[...] = jnp.zeros_like(o_ref)

    e1_rows = pl.ds(pl.multiple_of(j * PEER_E1_PER_STEP, PEER_E1_PER_STEP), PEER_E1_PER_STEP)
    ht = ht_ref[...]
    n_parts = PEER_E1_PER_STEP // PEER_E1_PER_PART
    part_slice = lambda p: slice(p * PEER_E1_PER_PART * N_KEYS, (p + 1) * PEER_E1_PER_PART * N_KEYS)
    activation = lambda p: _gelu_tanh(jnp.dot(u_ref[part_slice(p), :], ht, preferred_element_type=jnp.float32))
    act_next = activation(0)
    for part in range(n_parts):
        act = act_next
        if part + 1 < n_parts:
            act_next = activation(part + 1)
        for tc in range(tm // LANE):
            cols = slice(tc * LANE, (tc + 1) * LANE)
            s1_rows = [s1_ref[h, e1_rows, cols] for h in range(PEER_HEADS)]
            g1_rows = [g1_ref[h, e1_rows, cols] for h in range(PEER_HEADS)]
            for ee in range(PEER_E1_PER_PART):
                e = part * PEER_E1_PER_PART + ee
                w = None
                for h in range(PEER_HEADS):
                    score = s2_ref[h, :, cols] + s1_rows[h][e:e + 1, :]
                    gate = e2_ref[h, :, cols] * g1_rows[h][e:e + 1, :]
                    term = jnp.where(score >= th_ref[h:h + 1, cols], gate, 0.0)
                    w = term if w is None else w + term
                rows = slice(ee * N_KEYS, (ee + 1) * N_KEYS)
                wg_ref[part, rows, cols] = (w * act[rows, cols]).astype(jnp.bfloat16)
        if part >= 1:
            o_ref[...] += jnp.dot(vt_ref[:, part_slice(part - 1)], wg_ref[part - 1],
                                  preferred_element_type=jnp.float32)
    o_ref[...] += jnp.dot(vt_ref[:, part_slice(n_parts - 1)], wg_ref[n_parts - 1],
                          preferred_element_type=jnp.float32)


def peer_experts(ht, u, vt, s1, s2, g1, e2, th):
    D, N = ht.shape
    E = u.shape[0]
    tm = PEER_TOKENS
    te = PEER_E1_PER_STEP * N_KEYS
    assert N % tm == 0 and E % te == 0
    big_spec = pl.BlockSpec((PEER_HEADS, N_KEYS, tm), lambda i, j: (0, 0, i))
    return pl.pallas_call(
        _peer_expert_kernel,
        grid=(N // tm, E // te),
        in_specs=[pl.BlockSpec((D, tm), lambda i, j: (0, i)),
                  pl.BlockSpec((te, D), lambda i, j: (j, 0)),
                  pl.BlockSpec((D, te), lambda i, j: (0, j)),
                  big_spec, big_spec, big_spec, big_spec,
                  pl.BlockSpec((PEER_HEADS, tm), lambda i, j: (0, i))],
        out_specs=pl.BlockSpec((D, tm), lambda i, j: (0, i)),
        out_shape=jax.ShapeDtypeStruct((D, N), jnp.float32),
        scratch_shapes=[pltpu.VMEM((PEER_E1_PER_STEP // PEER_E1_PER_PART, PEER_E1_PER_PART * N_KEYS, tm), jnp.bfloat16)],
        compiler_params=pltpu.CompilerParams(
            dimension_semantics=("arbitrary", "arbitrary"), vmem_limit_bytes=VMEM_LIMIT_BYTES),
        name="peer_experts",
    )(ht, u, vt, s1, s2, g1, e2, th)


def peer_ffn_tokens(ht, wqt, keys, u, vt):
    s1, s2, g1, e2, th = peer_route(ht, wqt, keys)
    return peer_experts(ht, u, vt, s1, s2, g1, e2, th)


ZB_Q = Z_B // LANE
ZB_K = ZB_Q + B_HEADS
ZB_V = ZB_K + B_KV_HEADS
ZC_Q = ZB_V + B_KV_HEADS
ZC_K = ZC_Q + C_HEADS
ZC_V = ZC_K + C_HEADS
DIFF_Q_ROWS = 256

_DOT_NT = (((1,), (1,)), ((), ()))


def rope_lane_tables(n_tokens, head_dim):
    rows = n_tokens // GRID_W
    row_pos = jnp.repeat(jnp.arange(rows, dtype=jnp.float32), GRID_W)
    col_pos = jnp.tile(jnp.arange(GRID_W, dtype=jnp.float32), rows)
    n_freq = head_dim // 4
    freqs = ROPE_BASE ** (-jnp.arange(n_freq, dtype=jnp.float32) / n_freq)
    ang = jnp.concatenate([row_pos[:, None] * freqs, col_pos[:, None] * freqs], axis=-1)
    cos = jnp.repeat(jnp.cos(ang), 2, axis=-1)
    sin = jnp.repeat(jnp.sin(ang), 2, axis=-1) * jnp.tile(jnp.array([-1.0, 1.0], jnp.float32), head_dim // 2)
    reps = LANE // head_dim
    return jnp.tile(cos, (1, reps)), jnp.tile(sin, (1, reps))


def _rope(x, cos, sin_signed):
    lane = lax.broadcasted_iota(jnp.int32, x.shape, 1)
    partner = jnp.where(lane % 2 == 0, pltpu.roll(x, LANE - 1, axis=1), pltpu.roll(x, 1, axis=1))
    return x * cos + partner * sin_signed


def _bf(x):
    return x.astype(jnp.bfloat16)


def _diff_attn_kernel(lam_ref, q_ref, k_ref, v_ref, *rest, latent, lam_init):
    if latent:
        kc_ref, vc_ref, cq_ref, sq_ref, ck_ref, sk_ref, g_ref, o_ref, kr_ref = rest
    else:
        g_ref, o_ref, kr_ref = rest
    i = pl.program_id(2)

    @pl.when(i == 0)
    def _():
        k = k_ref[...]
        kr_ref[...] = _bf(_rope(k, ck_ref[...], sk_ref[...]) if latent else k)

    q = q_ref[...]
    if latent:
        q = _rope(q, cq_ref[...], sq_ref[...])
    lane = lax.broadcasted_iota(jnp.int32, q.shape, 1)
    scale = C_HEAD_DIM ** -0.5
    lam = lam_ref[0]
    kr = kr_ref[...]
    vb = _bf(v_ref[...])
    if latent:
        kcb, vcb = _bf(kc_ref[...]), _bf(vc_ref[...])

    def softmax_parts(qh):
        s = lax.dot_general(qh, kr, _DOT_NT, preferred_element_type=jnp.float32) * scale
        m = jnp.max(s, axis=-1, keepdims=True)
        if latent:
            sc = lax.dot_general(qh, kcb, _DOT_NT, preferred_element_type=jnp.float32) * scale
            m = jnp.maximum(m, jnp.max(sc, axis=-1, keepdims=True))
            ec = jnp.exp(sc - m)
        e = jnp.exp(s - m)
        z = jnp.sum(e, axis=-1, keepdims=True)
        if latent:
            z = z + jnp.sum(ec, axis=-1, keepdims=True)
            return e, ec, 1.0 / z
        return e, None, 1.0 / z

    e1, ec1, r1 = softmax_parts(_bf(jnp.where(lane < C_HEAD_DIM, q, 0.0)))
    e2, ec2, r2 = softmax_parts(_bf(jnp.where(lane >= C_HEAD_DIM, q, 0.0)))
    r2 = lam * r2
    o = jnp.dot(_bf(e1 * r1 - e2 * r2), vb, preferred_element_type=jnp.float32)
    if latent:
        o = o + jnp.dot(_bf(ec1 * r1 - ec2 * r2), vcb, preferred_element_type=jnp.float32)
    o = o * lax.rsqrt(jnp.mean(o * o, axis=-1, keepdims=True) + 1e-6) * g_ref[...] * (1.0 - lam_init)
    o_ref[...] = o


def diff_attention_mixer(z, lam, lam_init, subln_g, cache=None, layer=None, rope=None):
    B, T, _ = z.shape
    latent = cache is not None
    tq = min(DIFF_Q_ROWS, T)
    head = lambda base: pl.BlockSpec((None, tq, LANE), lambda b, h, i: (b, i, base + h))
    whole = lambda base: pl.BlockSpec((None, T, LANE), lambda b, h, i: (b, 0, base + h))
    in_specs = [pl.BlockSpec(memory_space=pltpu.SMEM), head(ZC_Q), whole(ZC_K), whole(ZC_V)]
    args = [lam.reshape(1), z, z, z]
    if latent:
        P = cache[0].shape[3]
        cspec = pl.BlockSpec((None, None, None, P, LANE), lambda b, h, i: (b, layer, h, 0, 0))
        tq_spec = pl.BlockSpec((tq, LANE), lambda b, h, i: (i, 0))
        tk_spec = pl.BlockSpec((T, LANE), lambda b, h, i: (0, 0))
        in_specs += [cspec, cspec, tq_spec, tq_spec, tk_spec, tk_spec]
        args += [cache[0], cache[1], rope[0], rope[1], rope[0], rope[1]]
    in_specs.append(pl.BlockSpec((1, LANE), lambda b, h, i: (0, 0)))
    args.append(subln_g.reshape(1, LANE))
    return pl.pallas_call(
        partial(_diff_attn_kernel, latent=latent, lam_init=lam_init),
        grid=(B, C_HEADS, T // tq),
        in_specs=in_specs,
        out_specs=pl.BlockSpec((None, tq, LANE), lambda b, h, i: (b, i, h)),
        out_shape=jax.ShapeDtypeStruct((B, T, C_HEADS * LANE), jnp.float32),
        scratch_shapes=[pltpu.VMEM((T, LANE), jnp.bfloat16)],
        compiler_params=pltpu.CompilerParams(
            dimension_semantics=("arbitrary", "arbitrary", "arbitrary"), vmem_limit_bytes=VMEM_LIMIT_BYTES),
        name="diff_attention",
    )(*args)


def _gqa_kernel(sink_ref, q0_ref, q1_ref, *rest, latent, n_tokens):
    if latent:
        (kp_ref, kn_ref, kx_ref, vp_ref, vn_ref, vx_ref, kc_ref, vc_ref,
         cq_ref, sq_ref, cp_ref, sp_ref, cx_ref, sx_ref, o_ref) = rest
    else:
        kc_ref, vc_ref, o_ref = rest
    kvh = pl.program_id(1)
    n = pl.program_id(2)
    scale = B_HEAD_DIM ** -0.5
    q0, q1 = q0_ref[...], q1_ref[...]
    if latent:
        q0 = _rope(q0, cq_ref[...], sq_ref[...])
        q1 = _rope(q1, cq_ref[...], sq_ref[...])
    q = _bf(jnp.concatenate([q0, q1], axis=0))
    row = lax.broadcasted_iota(jnp.int32, (2 * BLOCK, 1), 0)
    sink = jnp.where(row < BLOCK, sink_ref[2 * kvh], sink_ref[2 * kvh + 1])

    s_c = lax.dot_general(q, _bf(kc_ref[...]), _DOT_NT, preferred_element_type=jnp.float32) * scale
    m = jnp.maximum(jnp.max(s_c, axis=-1, keepdims=True), sink)
    if latent:
        kw = jnp.concatenate([_rope(kp_ref[...], cp_ref[...], sp_ref[...]),
                              _rope(kn_ref[...], cq_ref[...], sq_ref[...]),
                              _rope(kx_ref[...], cx_ref[...], sx_ref[...])], axis=0)
        vw = jnp.concatenate([vp_ref[...], vn_ref[...], vx_ref[...]], axis=0)
        s_w = lax.dot_general(q, _bf(kw), _DOT_NT, preferred_element_type=jnp.float32) * scale
        qpos = n * BLOCK + lax.broadcasted_iota(jnp.int32, s_w.shape, 0) % BLOCK
        kpos = (n - 1) * BLOCK + lax.broadcasted_iota(jnp.int32, s_w.shape, 1)
        valid = (jnp.abs(qpos - kpos) <= WINDOW) & (kpos >= 0) & (kpos < n_tokens)
        s_w = jnp.where(valid, s_w, NEG_INF)
        m = jnp.maximum(m, jnp.max(s_w, axis=-1, keepdims=True))
        e_w = jnp.exp(s_w - m)
    e_c = jnp.exp(s_c - m)
    zsum = jnp.sum(e_c, axis=-1, keepdims=True) + jnp.exp(sink - m)
    o = jnp.dot(_bf(e_c), _bf(vc_ref[...]), preferred_element_type=jnp.float32)
    if latent:
        zsum = zsum + jnp.sum(e_w, axis=-1, keepdims=True)
        o = o + jnp.dot(_bf(e_w), _bf(vw), preferred_element_type=jnp.float32)
    o = o / zsum
    o_ref[:, 0:LANE] = o[0:BLOCK]
    o_ref[:, LANE:2 * LANE] = o[BLOCK:2 * BLOCK]


def gqa_mixer(z, sink, cache=None, layer=None, rope=None):
    B, T, _ = z.shape
    latent = cache is not None
    nb = T // BLOCK
    blk = lambda base, off: pl.BlockSpec(
        (None, BLOCK, LANE), lambda b, kvh, n: (b, jnp.clip(n + off, 0, nb - 1), base + kvh))
    qspec = lambda g: pl.BlockSpec((None, BLOCK, LANE), lambda b, kvh, n: (b, n, ZB_Q + 2 * kvh + g))
    in_specs = [pl.BlockSpec(memory_space=pltpu.SMEM), qspec(0), qspec(1)]
    args = [sink, z, z]
    if latent:
        P = cache[0].shape[3]
        cspec = pl.BlockSpec((None, None, None, P, LANE), lambda b, kvh, n: (b, layer, kvh, 0, 0))
        tab = lambda off: pl.BlockSpec((BLOCK, LANE), lambda b, kvh, n: (jnp.clip(n + off, 0, nb - 1), 0))
        in_specs += [blk(ZB_K, -1), blk(ZB_K, 0), blk(ZB_K, 1), blk(ZB_V, -1), blk(ZB_V, 0), blk(ZB_V, 1),
                     cspec, cspec, tab(0), tab(0), tab(-1), tab(-1), tab(1), tab(1)]
        args += [z] * 6 + [cache[0], cache[1]] + [rope[0], rope[1]] * 3
    else:
        whole = lambda base: pl.BlockSpec((None, T, LANE), lambda b, kvh, n: (b, 0, base + kvh))
        in_specs += [whole(ZB_K), whole(ZB_V)]
        args += [z, z]
    return pl.pallas_call(
        partial(_gqa_kernel, latent=latent, n_tokens=T),
        grid=(B, B_KV_HEADS, nb),
        in_specs=in_specs,
        out_specs=pl.BlockSpec((None, BLOCK, 2 * LANE), lambda b, kvh, n: (b, n, kvh)),
        out_shape=jax.ShapeDtypeStruct((B, T, B_HEADS * LANE), jnp.float32),
        compiler_params=pltpu.CompilerParams(
            dimension_semantics=("arbitrary", "arbitrary", "arbitrary"), vmem_limit_bytes=VMEM_LIMIT_BYTES),
        name="gqa_attention",
    )(*args)


def split_cols(z, sizes):
    return jnp.split(z, [int(s) for s in np.cumsum(sizes)[:-1]], axis=-1)


def layer_norm(x, g=None, b=None):
    mu = jnp.mean(x, axis=-1, keepdims=True)
    var = jnp.mean(jnp.square(x - mu), axis=-1, keepdims=True)
    y = (x - mu) * lax.rsqrt(var + LN_EPS)
    if g is not None:
        y = y * g + b
    return y


def centred_conv3(x, w):
    xp = jnp.pad(x, ((0, 0), (1, 1), (0, 0)))
    return xp[:, :-2] * w[0] + xp[:, 1:-1] * w[1] + xp[:, 2:] * w[2]


def rwkv7_mixer(za, S0_f, S0_b, lp):
    Bsz, T, _ = za.shape
    r, k, v, g_d, wd_f, wd_b, ad_f, ad_b = split_cols(
        za, [A_WIDTH, A_WIDTH, A_WIDTH, G_LORA, W_LORA, W_LORA, ICL_LORA, ICL_LORA])
    heads = lambda t: t.reshape(Bsz, T, A_HEADS, A_HEAD_DIM)
    wz = jnp.stack([mm3(jnp.tanh(wd), lp['a_w_up'][d]) for d, wd in enumerate((wd_f, wd_b))], axis=0)
    az = jnp.stack([mm3(ad, lp['a_a_up'][d]) for d, ad in enumerate((ad_f, ad_b))], axis=0)
    s0t = jnp.swapaxes(jnp.stack([S0_f, S0_b], axis=0), -1, -2)
    ys, sfin_t = rwkv7_chunked(za, wz, az, lp['a_w0'], lp['a_a0'], lp['a_k_k'], lp['a_k_a'], s0t)
    states = jnp.swapaxes(sfin_t, -1, -2)
    y = heads(ys[0] + ys[1])
    mu = jnp.mean(y, axis=-1, keepdims=True)
    var = jnp.mean(jnp.square(y - mu), axis=-1, keepdims=True)
    y = ((y - mu) * lax.rsqrt(var + A_GN_EPS)).reshape(Bsz, T, A_WIDTH) * lp['a_gn_g'] + lp['a_gn_b']
    icl = jax.nn.sigmoid(az + lp['a_a0'][:, None, None, :])
    kd_sum = k * (2.0 + (icl[0] + icl[1] - 2.0) * lp['a_k_a'])
    bonus = jnp.sum(heads(r) * heads(kd_sum) * lp['a_r_k'], axis=-1, keepdims=True) * heads(v)
    g = mm3(jax.nn.sigmoid(g_d), lp['a_g_up'])
    out = (y + bonus.reshape(Bsz, T, A_WIDTH)) * g
    return out, states[0], states[1]


def diff_lambda(lv, layer):
    lam_init = 0.8 - 0.6 * math.exp(-0.3 * layer)
    lam = jnp.exp(jnp.sum(lv[0] * lv[1])) - jnp.exp(jnp.sum(lv[2] * lv[3])) + lam_init
    return lam, lam_init


def adaln_params(cond, lp):
    mod = mm(jax.nn.silu(cond), lp['w_mod']) + lp['b_mod']
    return jnp.split(mod[:, None, :], 6, axis=-1)


def mixer_inputs(x, shift, scale, lp):
    z = ln_mod_matmul(x, scale, shift, lp['w_in'])
    return z, centred_conv3(z[..., Z_A:Z_A + A_COLS], lp['a_conv'])


def merge_mixers(x, yA, yB, yC, z, gate1, shift2, scale2, lp):
    n = x.shape[0] * x.shape[1]
    flat = lambda t: t.reshape(n, t.shape[-1])
    merged = merge_gated(flat(yA), flat(yB), flat(yC), flat(z), lp['p_a'], lp['p_b'], lp['p_c'])
    return out_proj_norm(merged, x, gate1, scale2, shift2, lp['ln1_g'], lp['ln1_b'], lp['w_out'])


def peer_and_norm(part, lp):
    x, ht, gate2 = part
    yt = peer_ffn_tokens(ht, lp['peer_wqt'], lp['peer_keys'], lp['peer_u'], lp['peer_vt'])
    return peer_residual_norm(yt, x, gate2, lp['ln2_g'], lp['ln2_b'])


def context_layer(x, cond, lp, layer):
    Bsz, L, _ = x.shape
    shift1, scale1, gate1, shift2, scale2, gate2 = adaln_params(cond, lp)
    z, zA = mixer_inputs(x, shift1, scale1, lp)
    S0 = jnp.zeros((Bsz, A_HEADS, A_HEAD_DIM, A_HEAD_DIM), jnp.float32)
    yA, S_f, S_b = rwkv7_mixer(zA, S0, S0, lp)
    yB = gqa_mixer(z, lp['b_sink'])
    lam, lam_init = diff_lambda(lp['c_lam'], layer)
    yC = diff_attention_mixer(z, lam, lam_init, lp['c_subln'])
    x_mid, h_mid = merge_mixers(x, yA, yB, yC, z, gate1, shift2, scale2, lp)

    def cache_layout(first_block, n_heads):
        t = z[..., first_block * LANE:(first_block + n_heads) * LANE].reshape(Bsz, L, n_heads, LANE)
        return jnp.transpose(t, (0, 2, 1, 3))
    ctx = (S_f, S_b, cache_layout(ZB_K, B_KV_HEADS), cache_layout(ZB_V, B_KV_HEADS),
           cache_layout(ZC_K, C_HEADS), cache_layout(ZC_V, C_HEADS))
    return (x_mid, h_mid, gate2), ctx


def latent_layer(x, cond, lp, layer, S_f0, S_b0, cache_b, cache_c, rope_b, rope_c):
    shift1, scale1, gate1, shift2, scale2, gate2 = adaln_params(cond, lp)
    z, zA = mixer_inputs(x, shift1, scale1, lp)
    yA, _, _ = rwkv7_mixer(zA, S_f0, S_b0, lp)
    yB = gqa_mixer(z, lp['b_sink'], cache=cache_b, layer=layer, rope=rope_b)
    lam, lam_init = diff_lambda(lp['c_lam'], layer)
    yC = diff_attention_mixer(z, lam, lam_init, lp['c_subln'], cache=cache_c, layer=layer, rope=rope_c)
    x_mid, h_mid = merge_mixers(x, yA, yB, yC, z, gate1, shift2, scale2, lp)
    return x_mid, h_mid, gate2


_BF16_WEIGHTS = ('w_mod', 'w_in', 'a_w_up', 'a_a_up', 'a_g_up', 'p_a', 'p_b', 'p_c', 'w_out', 'peer_u')


def kernel(x_prompt, x_sample, state_a_fwd, state_a_bwd, cache_b_k, cache_b_v, cache_c_k, cache_c_v, c, c_ctx, w_mod, b_mod, w_in, a_conv, a_w0, a_w_up, a_a0, a_a_up, a_g_up, a_k_k, a_k_a, a_r_k, a_gn_g, a_gn_b, b_sink, c_lam, c_subln, p_a, p_b, p_c, w_out, ln1_g, ln1_b, ln2_g, ln2_b, peer_wq, peer_keys, peer_u, peer_v):
    params = dict(w_mod=w_mod, b_mod=b_mod, w_in=w_in, a_conv=a_conv, a_w0=a_w0, a_w_up=a_w_up, a_a0=a_a0,
                  a_a_up=a_a_up, a_g_up=a_g_up, a_k_k=a_k_k, a_k_a=a_k_a, a_r_k=a_r_k, a_gn_g=a_gn_g,
                  a_gn_b=a_gn_b, b_sink=b_sink, c_lam=c_lam, c_subln=c_subln, p_a=p_a, p_b=p_b, p_c=p_c,
                  w_out=w_out, ln1_g=ln1_g, ln1_b=ln1_b, ln2_g=ln2_g, ln2_b=ln2_b, peer_wq=peer_wq,
                  peer_keys=peer_keys, peer_u=peer_u, peer_v=peer_v)
    for name in _BF16_WEIGHTS:
        params[name] = params[name].astype(jnp.bfloat16)
    w_in_bf = params['w_in']
    params['w_in'] = jnp.concatenate([w_in_bf[..., IN_COLS - 3 * D_MODEL:], w_in_bf[..., :IN_COLS - 3 * D_MODEL]], axis=-1)
    params['peer_wqt'] = jnp.swapaxes(params.pop('peer_wq'), 1, 2).astype(jnp.bfloat16)
    params['peer_vt'] = jnp.swapaxes(params.pop('peer_v'), 1, 2).astype(jnp.bfloat16)
    n_lat = x_sample.shape[1]
    rope_b = rope_lane_tables(n_lat, B_HEAD_DIM)
    rope_c = rope_lane_tables(n_lat, C_HEAD_DIM)
    cond_ctx = jnp.broadcast_to(c_ctx[None, :], (x_prompt.shape[0], D_MODEL))
    y_prompt, y_sample = x_prompt, x_sample
    new = [[], [], [], [], [], []]
    for layer in range(DEPTH):
        lp = {name: val[layer] for name, val in params.items()}
        part_ctx, ctx = context_layer(y_prompt, cond_ctx, lp, layer)
        for lst, t in zip(new, ctx):
            lst.append(t)
        part_lat = latent_layer(y_sample, c, lp, layer,
                                state_a_fwd[:, layer], state_a_bwd[:, layer],
                                (cache_b_k, cache_b_v), (cache_c_k, cache_c_v), rope_b, rope_c)
        y_prompt = peer_and_norm(part_ctx, lp)
        y_sample = peer_and_norm(part_lat, lp)
    return (y_prompt, y_sample) + tuple(jnp.stack(lst, axis=1) for lst in new)
```

```python
import math
from functools import partial

import jax
import jax.numpy as jnp
import numpy as np
from jax import lax
from jax.experimental import pallas as pl
from jax.experimental.pallas import tpu as pltpu

D_MODEL = 2048
DEPTH = 4
GRID_W = 64
BLOCK = 128
A_HEADS = 16
A_HEAD_DIM = 64
A_WIDTH = A_HEADS * A_HEAD_DIM
W_LORA = 64
ICL_LORA = 64
G_LORA = 128
A_COLS = 3 * A_WIDTH + G_LORA + 2 * W_LORA + 2 * ICL_LORA
A_GN_EPS = 64e-5
B_HEADS = 4
B_KV_HEADS = 2
B_HEAD_DIM = 128
B_WIDTH = B_HEADS * B_HEAD_DIM
B_KV_WIDTH = B_KV_HEADS * B_HEAD_DIM
WINDOW = 128
C_HEADS = 4
C_HEAD_DIM = 64
C_WIDTH = C_HEADS * 2 * C_HEAD_DIM
IN_COLS = A_COLS + B_WIDTH + 2 * B_KV_WIDTH + 3 * C_WIDTH + 3 * D_MODEL
PEER_HEADS = 8
N_KEYS = 128
PEER_QDIM = 256
PEER_TOPK = 16
ROPE_BASE = 10000.0
LN_EPS = 1e-5
NEG_INF = -1e30
DEEPNORM_ALPHA = (2 * DEPTH) ** 0.25

LANE = 128
VMEM_LIMIT_BYTES = 56 * 1024 * 1024


def _mm_kernel(x_ref, w_ref, o_ref):
    o_ref[...] = jnp.dot(x_ref[...].astype(jnp.bfloat16), w_ref[...],
                         preferred_element_type=jnp.float32)


def _pick_tile(n, cap, unit):
    if n <= cap:
        return n
    best = None
    for t in range(unit, cap + 1, unit):
        if n % t == 0:
            best = t
    assert best is not None, (n, cap, unit)
    return best


def mm(x, w):
    M, K = x.shape
    N = w.shape[1]
    tm = _pick_tile(M, 512, 8)
    tn = _pick_tile(N, 1024, LANE)
    return pl.pallas_call(
        _mm_kernel,
        grid=(N // tn, M // tm),
        in_specs=[pl.BlockSpec((tm, K), lambda j, i: (i, 0)),
                  pl.BlockSpec((K, tn), lambda j, i: (0, j))],
        out_specs=pl.BlockSpec((tm, tn), lambda j, i: (i, j)),
        out_shape=jax.ShapeDtypeStruct((M, N), jnp.float32),
        compiler_params=pltpu.CompilerParams(
            dimension_semantics=("arbitrary", "arbitrary"),
            vmem_limit_bytes=VMEM_LIMIT_BYTES),
        name="mm",
    )(x, w)


def mm3(x, w):
    B, T, K = x.shape
    return mm(x.reshape(B * T, K), w).reshape(B, T, w.shape[1])


ROW_TILE = 1024
NORM_ROW_TILE = 512
Z_GATES = 0
Z_A = 3 * D_MODEL
Z_B = Z_A + A_COLS
Z_C = Z_B + B_WIDTH + 2 * B_KV_WIDTH


def _row_blocks(n_batch, n_time, rows):
    tt = min(n_time, rows)
    bt = rows // tt
    assert n_time % tt == 0 and n_batch % bt == 0, (n_batch, n_time, rows)
    return bt, tt


def _layer_norm_rows(x):
    mu = jnp.mean(x, axis=-1, keepdims=True)
    xc = x - mu
    return xc * lax.rsqrt(jnp.mean(xc * xc, axis=-1, keepdims=True) + LN_EPS)


def _ln_mod_mm_kernel(x_ref, scale_ref, shift_ref, w_ref, o_ref, h_ref):
    bt, tt, K = x_ref.shape

    @pl.when(pl.program_id(2) == 0)
    def _():
        h = _layer_norm_rows(x_ref[...]) * (1.0 + scale_ref[...]) + shift_ref[...]
        h_ref[...] = h.reshape(bt * tt, K).astype(jnp.bfloat16)

    o = jnp.dot(h_ref[...], w_ref[...], preferred_element_type=jnp.float32)
    o_ref[...] = o.reshape(bt, tt, o.shape[-1])


def ln_mod_matmul(x, scale, shift, w):
    B, T, K = x.shape
    N = w.shape[1]
    bt, tt = _row_blocks(B, T, ROW_TILE)
    tn = _pick_tile(N, 1024, LANE)
    vec = pl.BlockSpec((bt, 1, K), lambda b, t, j: (b, 0, 0))
    return pl.pallas_call(
        _ln_mod_mm_kernel,
        grid=(B // bt, T // tt, N // tn),
        in_specs=[pl.BlockSpec((bt, tt, K), lambda b, t, j: (b, t, 0)), vec, vec,
                  pl.BlockSpec((K, tn), lambda b, t, j: (0, j))],
        out_specs=pl.BlockSpec((bt, tt, tn), lambda b, t, j: (b, t, j)),
        out_shape=jax.ShapeDtypeStruct((B, T, N), jnp.float32),
        scratch_shapes=[pltpu.VMEM((bt * tt, K), jnp.bfloat16)],
        compiler_params=pltpu.CompilerParams(
            dimension_semantics=("arbitrary", "arbitrary", "arbitrary"), vmem_limit_bytes=VMEM_LIMIT_BYTES),
        name="ln_mod_matmul",
    )(x, scale, shift, w)


def _merge_kernel(ya_ref, yb_ref, yc_ref, ga_ref, gb_ref, gc_ref, pa_ref, pb_ref, pc_ref, o_ref,
                  ya_bf, yb_bf, yc_bf):
    @pl.when(pl.program_id(1) == 0)
    def _():
        ya_bf[...] = ya_ref[...].astype(jnp.bfloat16)
        yb_bf[...] = yb_ref[...].astype(jnp.bfloat16)
        yc_bf[...] = yc_ref[...].astype(jnp.bfloat16)

    f = lambda y, p: jnp.dot(y[...], p[...], preferred_element_type=jnp.float32)
    o = (jax.nn.sigmoid(ga_ref[...]) * f(ya_bf, pa_ref) + jax.nn.sigmoid(gb_ref[...]) * f(yb_bf, pb_ref)
         + jax.nn.sigmoid(gc_ref[...]) * f(yc_bf, pc_ref))
    o_ref[...] = o.astype(jnp.bfloat16)


def merge_gated(yA, yB, yC, z, p_a, p_b, p_c):
    N = yA.shape[0]
    D = p_a.shape[1]
    tm = _pick_tile(N, ROW_TILE, 8)
    tn = 512
    assert Z_GATES == 0 and D % tn == 0
    ysp = lambda y: pl.BlockSpec((tm, y.shape[1]), lambda i, j: (i, 0))
    gsp = lambda k: pl.BlockSpec((tm, tn), lambda i, j: (i, k * (D // tn) + j))
    psp = lambda p: pl.BlockSpec((p.shape[0], tn), lambda i, j: (0, j))
    return pl.pallas_call(
        _merge_kernel,
        grid=(N // tm, D // tn),
        in_specs=[ysp(yA), ysp(yB), ysp(yC), gsp(0), gsp(1), gsp(2), psp(p_a), psp(p_b), psp(p_c)],
        out_specs=pl.BlockSpec((tm, tn), lambda i, j: (i, j)),
        out_shape=jax.ShapeDtypeStruct((N, D), jnp.bfloat16),
        scratch_shapes=[pltpu.VMEM((tm, y.shape[1]), jnp.bfloat16) for y in (yA, yB, yC)],
        compiler_params=pltpu.CompilerParams(
            dimension_semantics=("arbitrary", "arbitrary"), vmem_limit_bytes=VMEM_LIMIT_BYTES),
        name="merge_gated",
    )(yA, yB, yC, z, z, z, p_a, p_b, p_c)


def _out_norm_kernel(m_ref, x_ref, gate_ref, scale_ref, shift_ref, lng_ref, lnb_ref, w_ref, x1_ref, ht_ref):
    bt, tt, D = x_ref.shape
    y = jnp.dot(m_ref[...].reshape(bt * tt, D), w_ref[...], preferred_element_type=jnp.float32)
    x1 = _layer_norm_rows(DEEPNORM_ALPHA * x_ref[...] + gate_ref[...] * y.reshape(bt, tt, D))
    x1 = x1 * lng_ref[...] + lnb_ref[...]
    x1_ref[...] = x1
    h = _layer_norm_rows(x1) * (1.0 + scale_ref[...]) + shift_ref[...]
    ht_ref[...] = jnp.transpose(h.reshape(bt * tt, D)).astype(jnp.bfloat16)


def out_proj_norm(merged, x, gate1, scale2, shift2, ln_g, ln_b, w_out):
    B, T, D = x.shape
    bt, tt = _row_blocks(B, T, NORM_ROW_TILE)
    rows = pl.BlockSpec((bt, tt, D), lambda b, t: (b, t, 0))
    vec = pl.BlockSpec((bt, 1, D), lambda b, t: (b, 0, 0))
    par = pl.BlockSpec((1, 1, D), lambda b, t: (0, 0, 0))
    return pl.pallas_call(
        _out_norm_kernel,
        grid=(B // bt, T // tt),
        in_specs=[rows, rows, vec, vec, vec, par, par, pl.BlockSpec((D, D), lambda b, t: (0, 0))],
        out_specs=[rows, pl.BlockSpec((D, bt * tt), lambda b, t: (0, b * (T // tt) + t))],
        out_shape=[jax.ShapeDtypeStruct((B, T, D), jnp.float32), jax.ShapeDtypeStruct((D, B * T), jnp.bfloat16)],
        compiler_params=pltpu.CompilerParams(
            dimension_semantics=("arbitrary", "arbitrary"), vmem_limit_bytes=VMEM_LIMIT_BYTES),
        name="out_proj_norm",
    )(merged.reshape(B, T, D), x, gate1, scale2, shift2, ln_g.reshape(1, 1, D), ln_b.reshape(1, 1, D), w_out)


def _peer_norm_kernel(yt_ref, x_ref, gate_ref, lng_ref, lnb_ref, o_ref):
    bt, tt, D = x_ref.shape
    y = jnp.transpose(yt_ref[...]).reshape(bt, tt, D)
    x = _layer_norm_rows(DEEPNORM_ALPHA * x_ref[...] + gate_ref[...] * y)
    o_ref[...] = x * lng_ref[...] + lnb_ref[...]


def peer_residual_norm(yt, x, gate2, ln_g, ln_b):
    B, T, D = x.shape
    bt, tt = _row_blocks(B, T, NORM_ROW_TILE)
    rows = pl.BlockSpec((bt, tt, D), lambda b, t: (b, t, 0))
    par = pl.BlockSpec((1, 1, D), lambda b, t: (0, 0, 0))
    return pl.pallas_call(
        _peer_norm_kernel,
        grid=(B // bt, T // tt),
        in_specs=[pl.BlockSpec((D, bt * tt), lambda b, t: (0, b * (T // tt) + t)), rows,
                  pl.BlockSpec((bt, 1, D), lambda b, t: (b, 0, 0)), par, par],
        out_specs=rows,
        out_shape=jax.ShapeDtypeStruct((B, T, D), jnp.float32),
        compiler_params=pltpu.CompilerParams(
            dimension_semantics=("arbitrary", "arbitrary"), vmem_limit_bytes=VMEM_LIMIT_BYTES),
        name="peer_residual_norm",
    )(yt, x, gate2, ln_g.reshape(1, 1, D), ln_b.reshape(1, 1, D))


RWKV_CHUNK = 64
RWKV_HEADS_PER_STEP = 16

_NN = (((2,), (1,)), ((0,), (0,)))
_NT = (((2,), (2,)), ((0,), (0,)))
_TN = (((1,), (1,)), ((0,), (0,)))


def _split2(x):
    hi = x.astype(jnp.bfloat16)
    lo = (x - hi.astype(jnp.float32)).astype(jnp.bfloat16)
    return hi, lo


def _dot3(a, b, dims=_NN):
    f = lambda x, y: lax.dot_general(x, y, dims, preferred_element_type=jnp.float32)
    return f(a[0], b[0]) + (f(a[0], b[1]) + f(a[1], b[0]))


def _dot1(a, b, dims=_NN):
    return lax.dot_general(a[0], b[0], dims, preferred_element_type=jnp.float32)


def _split_heads(x):
    n = A_HEAD_DIM
    return jnp.stack([x[:, h * n:(h + 1) * n] for h in range(x.shape[1] // n)], axis=0)


def _rwkv_chunk_kernel(zr_ref, zk_ref, zv_ref, wz_ref, az_ref, w0_ref, a0_ref, kk_ref, ka_ref, s0_ref,
                       y_ref, sfin_ref, st_ref):
    d = pl.program_id(0)
    c = pl.program_id(3)
    n_chunks = pl.num_programs(3)
    HB = st_ref.shape[0]
    C = zr_ref.shape[0]

    @pl.when(c == 0)
    def _():
        st_ref[...] = s0_ref[...]

    k_all = zk_ref[...]
    lw_all = -math.exp(-0.5) * jax.nn.sigmoid(wz_ref[...] + w0_ref[...])
    icl_all = jax.nn.sigmoid(az_ref[...] + a0_ref[...])
    r, v = _split_heads(zr_ref[...]), _split_heads(zv_ref[...])
    lw = _split_heads(lw_all)
    k = _split_heads(k_all * (1.0 + (icl_all - 1.0) * ka_ref[...]))
    kk = _split_heads(k_all * kk_ref[...])
    kk = kk * lax.rsqrt(jnp.sum(kk * kk, axis=-1, keepdims=True) + 1e-12)
    a = -kk
    b = kk * _split_heads(icl_all)

    ti = lax.broadcasted_iota(jnp.int32, (HB, C, C), 1)
    si = lax.broadcasted_iota(jnp.int32, (HB, C, C), 2)
    diff = (si - ti) * (1 - 2 * d)
    m_incl = diff <= 0
    m_strict = diff < 0
    m_incl_bf = jnp.where(m_incl, 1.0, 0.0).astype(jnp.bfloat16)
    eye = jnp.where(diff == 0, 1.0, 0.0).astype(jnp.float32)

    st = st_ref[...]

    lw_hi, lw_lo = _split2(lw)
    lw_lo2 = (lw - lw_hi.astype(jnp.float32) - lw_lo.astype(jnp.float32)).astype(jnp.bfloat16)
    f = lambda y: lax.dot_general(m_incl_bf, y, _NN, preferred_element_type=jnp.float32)
    cum = f(lw_hi) + (f(lw_lo) + f(lw_lo2))
    total = jnp.sum(lw, axis=1, keepdims=True)
    e_cum = jnp.exp(cum)
    e_inv = jnp.exp(-cum)
    e_tot = jnp.exp(total)
    hi = lambda x: (x.astype(jnp.bfloat16),)
    at = hi(a * jnp.exp(cum - lw))
    rt = _split2(r * e_cum)
    bt_f = b * e_inv
    kt_f = k * e_inv
    bt, kt = hi(bt_f), hi(kt_f)
    bh, kh = _split2(bt_f * e_tot), _split2(kt_f * e_tot)
    vs = _split2(v)

    l_ab = jnp.where(m_strict, _dot1(at, bt, _NT), 0.0)
    l_ak = jnp.where(m_strict, _dot1(at, kt, _NT), 0.0)
    m_rb = jnp.where(m_incl, _dot1(rt, bt, _NT), 0.0)
    m_rk = jnp.where(m_incl, _dot1(rt, kt, _NT), 0.0)

    tm = eye + l_ab
    lp = hi(l_ab)
    for _ in range(int(math.log2(C)) - 1):
        lp = hi(_dot1(lp, lp))
        tm = tm + _dot1(hi(tm), lp)
    tms = hi(tm)
    p = _split2(_dot1(tms, at))
    q = _dot1(tms, hi(_dot1(hi(l_ak), vs)))

    sts = _split2(st)
    u = _split2(_dot3(p, sts) + q)
    y = _dot3(rt, sts) + _dot3(_split2(m_rb), u) + _dot3(_split2(m_rk), vs)
    y_ref[...] = jnp.concatenate([y[h] for h in range(HB)], axis=-1)
    st_ref[...] = jnp.swapaxes(e_tot, 1, 2) * st + _dot3(bh, u, _TN) + _dot3(kh, vs, _TN)

    @pl.when(c == n_chunks - 1)
    def _():
        sfin_ref[...] = st_ref[...]


def rwkv7_chunked(zc, wz, az, w0, a0, k_k, k_a, s0t):
    D, B, T, W = wz.shape
    N, H = A_HEAD_DIM, A_HEADS
    C, HB = RWKV_CHUNK, RWKV_HEADS_PER_STEP
    G = H // HB
    lanes = HB * N
    assert T % C == 0 and H % HB == 0 and W == H * N
    n_chunks = T // C
    chunk = lambda d, c: c + d * (n_chunks - 1 - 2 * c)
    zcol = lambda base: pl.BlockSpec((None, C, lanes), lambda d, bb, g, c: (bb, chunk(d, c), base * G + g))
    per_dir = pl.BlockSpec((None, None, C, lanes), lambda d, bb, g, c: (d, bb, chunk(d, c), g))
    dir_row = pl.BlockSpec((None, 1, lanes), lambda d, bb, g, c: (d, 0, g))
    row = pl.BlockSpec((1, lanes), lambda d, bb, g, c: (0, g))
    state = pl.BlockSpec((None, None, HB, N, N), lambda d, bb, g, c: (d, bb, g, 0, 0))
    return pl.pallas_call(
        _rwkv_chunk_kernel,
        grid=(D, B, G, n_chunks),
        in_specs=[zcol(0), zcol(1), zcol(2), per_dir, per_dir, dir_row, dir_row, row, row, state],
        out_specs=[per_dir, state],
        out_shape=[jax.ShapeDtypeStruct((D, B, T, W), jnp.float32),
                   jax.ShapeDtypeStruct((D, B, H, N, N), jnp.float32)],
        scratch_shapes=[pltpu.VMEM((HB, N, N), jnp.float32)],
        compiler_params=pltpu.CompilerParams(
            dimension_semantics=("arbitrary", "arbitrary", "arbitrary", "arbitrary"),
            vmem_limit_bytes=VMEM_LIMIT_BYTES),
        name="rwkv7_chunked",
    )(zc, zc, zc, wz, az, w0.reshape(D, 1, W), a0.reshape(D, 1, W), k_k.reshape(1, W), k_a.reshape(1, W), s0t)


CONV_COLS = 384
SUBLANE = 8


def _conv3_kernel(x_ref, prev_ref, next_ref, w_ref, o_ref):
    t = pl.program_id(1)
    x = x_ref[...]
    tt = x.shape[0]
    row = lax.broadcasted_iota(jnp.int32, x.shape, 0)
    before = jnp.where(t > 0, prev_ref[SUBLANE - 1:SUBLANE, :], 0.0)
    after = jnp.where(t < pl.num_programs(1) - 1, next_ref[0:1, :], 0.0)
    x_prev = jnp.where(row == 0, before, pltpu.roll(x, 1, axis=0))
    x_next = jnp.where(row == tt - 1, after, pltpu.roll(x, tt - 1, axis=0))
    o_ref[...] = x_prev * w_ref[0:1, :] + x * w_ref[1:2, :] + x_next * w_ref[2:3, :]


def centred_conv3(z, w):
    B, T, _ = z.shape
    tt = min(T, 512)
    cb = CONV_COLS
    assert Z_A % cb == 0 and A_COLS % cb == 0 and T % tt == 0
    base = Z_A // cb
    groups = tt // SUBLANE
    last = T // SUBLANE - 1
    return pl.pallas_call(
        _conv3_kernel,
        grid=(B, T // tt, A_COLS // cb),
        in_specs=[pl.BlockSpec((None, tt, cb), lambda b, t, j: (b, t, base + j)),
                  pl.BlockSpec((None, SUBLANE, cb), lambda b, t, j: (b, jnp.maximum(t * groups - 1, 0), base + j)),
                  pl.BlockSpec((None, SUBLANE, cb), lambda b, t, j: (b, jnp.minimum((t + 1) * groups, last), base + j)),
                  pl.BlockSpec((3, cb), lambda b, t, j: (0, j))],
        out_specs=pl.BlockSpec((None, tt, cb), lambda b, t, j: (b, t, j)),
        out_shape=jax.ShapeDtypeStruct((B, T, A_COLS), jnp.float32),
        compiler_params=pltpu.CompilerParams(
            dimension_semantics=("arbitrary", "arbitrary", "arbitrary"), vmem_limit_bytes=VMEM_LIMIT_BYTES),
        name="conv3",
    )(z, z, z, w)


def _rwkv_post_kernel(yf_ref, yb_ref, r_ref, k_ref, v_ref, gd_ref, azf_ref, azb_ref, a0f_ref, a0b_ref,
                      ka_ref, rk_ref, gng_ref, gnb_ref, gup_ref, o_ref, ones_ref):
    W = o_ref.shape[-1]
    rows = o_ref.shape[0] * o_ref.shape[1]

    @pl.when((pl.program_id(0) == 0) & (pl.program_id(1) == 0))
    def _():
        hr = lax.broadcasted_iota(jnp.int32, (W, W), 0) // A_HEAD_DIM
        hc = lax.broadcasted_iota(jnp.int32, (W, W), 1) // A_HEAD_DIM
        ones_ref[...] = jnp.where(hr == hc, 1.0, 0.0).astype(jnp.bfloat16)

    def head_sum(x):
        hi, lo = _split2(x)
        f = lambda p: jnp.dot(p, ones_ref[...], preferred_element_type=jnp.float32)
        return f(hi) + f(lo)

    flat = lambda ref: ref[...].reshape(rows, ref.shape[-1])
    inv_n = 1.0 / A_HEAD_DIM
    y = flat(yf_ref) + flat(yb_ref)
    yc = y - head_sum(y) * inv_n
    var = head_sum(yc * yc) * inv_n
    yn = yc * lax.rsqrt(var + A_GN_EPS) * gng_ref[...] + gnb_ref[...]
    icl_sum = jax.nn.sigmoid(flat(azf_ref) + a0f_ref[...]) + jax.nn.sigmoid(flat(azb_ref) + a0b_ref[...])
    kd_sum = flat(k_ref) * (2.0 + (icl_sum - 2.0) * ka_ref[...])
    bonus = head_sum(flat(r_ref) * kd_sum * rk_ref[...]) * flat(v_ref)
    g = jnp.dot(jax.nn.sigmoid(flat(gd_ref)).astype(jnp.bfloat16), gup_ref[...], preferred_element_type=jnp.float32)
    o_ref[...] = ((yn + bonus) * g).reshape(o_ref.shape)


def rwkv_post(ys, za, az, a0, k_a, r_k, gn_g, gn_b, g_up):
    _, B, T, W = ys.shape
    bt, tt = _row_blocks(B, T, NORM_ROW_TILE)
    both = lambda d: pl.BlockSpec((None, bt, tt, W), lambda b, t: (d, b, t, 0))
    zcol = lambda j: pl.BlockSpec((bt, tt, W), lambda b, t: (b, t, j))
    dvec = lambda d: pl.BlockSpec((None, 1, W), lambda b, t: (d, 0, 0))
    vec = pl.BlockSpec((1, W), lambda b, t: (0, 0))
    assert (3 * W) % G_LORA == 0
    return pl.pallas_call(
        _rwkv_post_kernel,
        grid=(B // bt, T // tt),
        in_specs=[both(0), both(1), zcol(0), zcol(1), zcol(2),
                  pl.BlockSpec((bt, tt, G_LORA), lambda b, t: (b, t, 3 * W // G_LORA)),
                  both(0), both(1), dvec(0), dvec(1), vec, vec, vec, vec,
                  pl.BlockSpec((G_LORA, W), lambda b, t: (0, 0))],
        out_specs=pl.BlockSpec((bt, tt, W), lambda b, t: (b, t, 0)),
        out_shape=jax.ShapeDtypeStruct((B, T, W), jnp.float32),
        scratch_shapes=[pltpu.VMEM((W, W), jnp.bfloat16)],
        compiler_params=pltpu.CompilerParams(
            dimension_semantics=("arbitrary", "arbitrary"), vmem_limit_bytes=VMEM_LIMIT_BYTES),
        name="rwkv_post",
    )(ys, ys, za, za, za, za, az, az, a0.reshape(2, 1, W), a0.reshape(2, 1, W),
      k_a.reshape(1, W), r_k.reshape(1, W), gn_g.reshape(1, W), gn_b.reshape(1, W), g_up)


PEER_ROUTE_TOKENS = 256
PEER_TOKENS = 512
PEER_E1_PER_STEP = 8
PEER_E1_PER_PART = 2
PEER_CAND_ROWS = 56


def _top_rows(s_ref, n_rows, k, emit):
    for a in range(k):
        s = s_ref[0:n_rows, :]
        mx = jnp.max(s, axis=0, keepdims=True)
        emit(a, mx)
        if a + 1 < k:
            s_ref[0:n_rows, :] = jnp.where(s == mx, NEG_INF, s)


def _peer_route_kernel(ht_ref, wqt_ref, k1_ref, k2_ref, s1_ref, s2_ref, g1_ref, e2_ref, th_ref,
                       work_ref, t1_ref, t2_ref, cand_ref):
    half = PEER_QDIM // 2
    qt = jnp.dot(wqt_ref[...], ht_ref[...], preferred_element_type=jnp.float32)
    for h in range(PEER_HEADS):
        for which, (k_ref, s_out, t_ref) in enumerate(((k1_ref, s1_ref, t1_ref), (k2_ref, s2_ref, t2_ref))):
            q = qt[h * PEER_QDIM + which * half: h * PEER_QDIM + (which + 1) * half, :]
            q_hi, q_lo = _split2(q)
            k_hi, k_lo = _split2(k_ref[h])
            f = lambda x, y: jnp.dot(x, y, preferred_element_type=jnp.float32)
            s = f(k_hi, q_hi) + (f(k_hi, q_lo) + f(k_lo, q_hi))
            s_out[h] = s
            work_ref[...] = s

            def emit(a, mx, t_ref=t_ref):
                t_ref[a:a + 1, :] = mx
            _top_rows(work_ref, N_KEYS, PEER_TOPK, emit)
        t1 = t1_ref[...]
        t2 = t2_ref[...]
        m1 = t1[0:1, :]
        m2 = t2[0:1, :]
        row = 0
        for a in range(PEER_TOPK):
            nb = PEER_TOPK // (a + 1)
            cand_ref[row:row + nb, :] = t1[a:a + 1, :] + t2[0:nb, :]
            row += nb
        cand_ref[row:PEER_CAND_ROWS, :] = jnp.full((PEER_CAND_ROWS - row, t1.shape[1]), NEG_INF, jnp.float32)
        acc = {}

        def emit_c(a, mx):
            e = jnp.exp(mx - (m1 + m2))
            acc['z'] = e if a == 0 else acc['z'] + e
            acc['th'] = mx
        _top_rows(cand_ref, PEER_CAND_ROWS, PEER_TOPK, emit_c)
        th_ref[h:h + 1, :] = acc['th']
        g1_ref[h] = jnp.exp(s1_ref[h] - m1) / acc['z']
        e2_ref[h] = jnp.exp(s2_ref[h] - m2)


def peer_route(ht, wqt, keys):
    D, N = ht.shape
    tm = PEER_ROUTE_TOKENS
    assert N % tm == 0
    big = jax.ShapeDtypeStruct((PEER_HEADS, N_KEYS, N), jnp.float32)
    big_spec = pl.BlockSpec((PEER_HEADS, N_KEYS, tm), lambda i: (0, 0, i))
    key_spec = pl.BlockSpec((PEER_HEADS, N_KEYS, PEER_QDIM // 2), lambda i: (0, 0, 0))
    return pl.pallas_call(
        _peer_route_kernel,
        grid=(N // tm,),
        in_specs=[pl.BlockSpec((D, tm), lambda i: (0, i)),
                  pl.BlockSpec((PEER_HEADS * PEER_QDIM, D), lambda i: (0, 0)),
                  key_spec, key_spec],
        out_specs=[big_spec, big_spec, big_spec, big_spec, pl.BlockSpec((PEER_HEADS, tm), lambda i: (0, i))],
        out_shape=[big, big, big, big, jax.ShapeDtypeStruct((PEER_HEADS, N), jnp.float32)],
        scratch_shapes=[pltpu.VMEM((N_KEYS, tm), jnp.float32),
                        pltpu.VMEM((PEER_TOPK, tm), jnp.float32),
                        pltpu.VMEM((PEER_TOPK, tm), jnp.float32),
                        pltpu.VMEM((PEER_CAND_ROWS, tm), jnp.float32)],
        compiler_params=pltpu.CompilerParams(
            dimension_semantics=("arbitrary",), vmem_limit_bytes=VMEM_LIMIT_BYTES),
        name="peer_route",
    )(ht, wqt, keys[0], keys[1])


def _gelu_tanh(x):
    return 0.5 * x * (1.0 + jnp.tanh(math.sqrt(2.0 / math.pi) * (x + 0.044715 * (x * x * x))))


def _peer_expert_kernel(ht_ref, u_ref, vt_ref, s1_ref, s2_ref, g1_ref, e2_ref, th_ref, o_ref, wg_ref):
    j = pl.program_id(1)
    tm = ht_ref.shape[1]

    @pl.when(j == 0)
    def _():
        o_ref[...] = jnp.zeros_like(o_ref)

    e1_rows = pl.ds(pl.multiple_of(j * PEER_E1_PER_STEP, PEER_E1_PER_STEP), PEER_E1_PER_STEP)
    ht = ht_ref[...]
    n_parts = PEER_E1_PER_STEP // PEER_E1_PER_PART
    part_slice = lambda p: slice(p * PEER_E1_PER_PART * N_KEYS, (p + 1) * PEER_E1_PER_PART * N_KEYS)
    activation = lambda p: _gelu_tanh(jnp.dot(u_ref[part_slice(p), :], ht, preferred_element_type=jnp.float32))
    act_next = activation(0)
    for part in range(n_parts):
        act = act_next
        if part + 1 < n_parts:
            act_next = activation(part + 1)
        for tc in range(tm // LANE):
            cols = slice(tc * LANE, (tc + 1) * LANE)
            s1_rows = [s1_ref[h, e1_rows, cols] for h in range(PEER_HEADS)]
            g1_rows = [g1_ref[h, e1_rows, cols] for h in range(PEER_HEADS)]
            for ee in range(PEER_E1_PER_PART):
                e = part * PEER_E1_PER_PART + ee
                w = None
                for h in range(PEER_HEADS):
                    score = s2_ref[h, :, cols] + s1_rows[h][e:e + 1, :]
                    gate = e2_ref[h, :, cols] * g1_rows[h][e:e + 1, :]
                    term = jnp.where(score >= th_ref[h:h + 1, cols], gate, 0.0)
                    w = term if w is None else w + term
                rows = slice(ee * N_KEYS, (ee + 1) * N_KEYS)
                wg_ref[part, rows, cols] = (w * act[rows, cols]).astype(jnp.bfloat16)
        if part >= 1:
            o_ref[...] += jnp.dot(vt_ref[:, part_slice(part - 1)], wg_ref[part - 1],
                                  preferred_element_type=jnp.float32)
    o_ref[...] += jnp.dot(vt_ref[:, part_slice(n_parts - 1)], wg_ref[n_parts - 1],
                          preferred_element_type=jnp.float32)


def peer_experts(ht, u, vt, s1, s2, g1, e2, th):
    D, N = ht.shape
    E = u.shape[0]
    tm = PEER_TOKENS
    te = PEER_E1_PER_STEP * N_KEYS
    assert N % tm == 0 and E % te == 0
    big_spec = pl.BlockSpec((PEER_HEADS, N_KEYS, tm), lambda i, j: (0, 0, i))
    return pl.pallas_call(
        _peer_expert_kernel,
        grid=(N // tm, E // te),
        in_specs=[pl.BlockSpec((D, tm), lambda i, j: (0, i)),
                  pl.BlockSpec((te, D), lambda i, j: (j, 0)),
                  pl.BlockSpec((D, te), lambda i, j: (0, j)),
                  big_spec, big_spec, big_spec, big_spec,
                  pl.BlockSpec((PEER_HEADS, tm), lambda i, j: (0, i))],
        out_specs=pl.BlockSpec((D, tm), lambda i, j: (0, i)),
        out_shape=jax.ShapeDtypeStruct((D, N), jnp.float32),
        scratch_shapes=[pltpu.VMEM((PEER_E1_PER_STEP // PEER_E1_PER_PART, PEER_E1_PER_PART * N_KEYS, tm), jnp.bfloat16)],
        compiler_params=pltpu.CompilerParams(
            dimension_semantics=("arbitrary", "arbitrary"), vmem_limit_bytes=VMEM_LIMIT_BYTES),
        name="peer_experts",
    )(ht, u, vt, s1, s2, g1, e2, th)


def peer_ffn_tokens(ht, wqt, keys, u, vt):
    s1, s2, g1, e2, th = peer_route(ht, wqt, keys)
    return peer_experts(ht, u, vt, s1, s2, g1, e2, th)


ZB_Q = Z_B // LANE
ZB_K = ZB_Q + B_HEADS
ZB_V = ZB_K + B_KV_HEADS
ZC_Q = ZB_V + B_KV_HEADS
ZC_K = ZC_Q + C_HEADS
ZC_V = ZC_K + C_HEADS
DIFF_Q_ROWS = 256

_DOT_NT = (((1,), (1,)), ((), ()))


def rope_lane_tables(n_tokens, head_dim):
    rows = n_tokens // GRID_W
    row_pos = jnp.repeat(jnp.arange(rows, dtype=jnp.float32), GRID_W)
    col_pos = jnp.tile(jnp.arange(GRID_W, dtype=jnp.float32), rows)
    n_freq = head_dim // 4
    freqs = ROPE_BASE ** (-jnp.arange(n_freq, dtype=jnp.float32) / n_freq)
    ang = jnp.concatenate([row_pos[:, None] * freqs, col_pos[:, None] * freqs], axis=-1)
    cos = jnp.repeat(jnp.cos(ang), 2, axis=-1)
    sin = jnp.repeat(jnp.sin(ang), 2, axis=-1) * jnp.tile(jnp.array([-1.0, 1.0], jnp.float32), head_dim // 2)
    reps = LANE // head_dim
    return jnp.tile(cos, (1, reps)), jnp.tile(sin, (1, reps))


def _rope(x, cos, sin_signed):
    lane = lax.broadcasted_iota(jnp.int32, x.shape, 1)
    partner = jnp.where(lane % 2 == 0, pltpu.roll(x, LANE - 1, axis=1), pltpu.roll(x, 1, axis=1))
    return x * cos + partner * sin_signed


def _bf(x):
    return x.astype(jnp.bfloat16)


def _diff_attn_kernel(lam_ref, q_ref, k_ref, v_ref, *rest, latent, lam_init):
    if latent:
        kc_ref, vc_ref, cq_ref, sq_ref, ck_ref, sk_ref, g_ref, o_ref, kr_ref, vx_ref, vcx_ref = rest
    else:
        g_ref, o_ref, kr_ref, vx_ref = rest
    i = pl.program_id(2)

    @pl.when(i == 0)
    def _():
        k = k_ref[...]
        kr_ref[...] = _bf(_rope(k, ck_ref[...], sk_ref[...]) if latent else k)
        vx_ref[:, 0:LANE] = _bf(v_ref[...])
        vx_ref[:, LANE:2 * LANE] = jnp.ones((vx_ref.shape[0], LANE), jnp.bfloat16)
        if latent:
            vcx_ref[:, 0:LANE] = _bf(vc_ref[...])
            vcx_ref[:, LANE:2 * LANE] = jnp.ones((vcx_ref.shape[0], LANE), jnp.bfloat16)

    q = q_ref[...]
    if latent:
        q = _rope(q, cq_ref[...], sq_ref[...])
    q = q * (C_HEAD_DIM ** -0.5)
    lane = lax.broadcasted_iota(jnp.int32, q.shape, 1)
    kr = kr_ref[...]
    if latent:
        kcb = _bf(kc_ref[...])

    def softmax_attend(qh):
        s = lax.dot_general(qh, kr, _DOT_NT, preferred_element_type=jnp.float32)
        m = jnp.max(s, axis=-1, keepdims=True)
        if latent:
            sc = lax.dot_general(qh, kcb, _DOT_NT, preferred_element_type=jnp.float32)
            m = jnp.maximum(m, jnp.max(sc, axis=-1, keepdims=True))
        oz = jnp.dot(jnp.exp(_bf(s - m)), vx_ref[...], preferred_element_type=jnp.float32)
        if latent:
            oz = oz + jnp.dot(jnp.exp(_bf(sc - m)), vcx_ref[...], preferred_element_type=jnp.float32)
        return oz[:, 0:LANE] / oz[:, LANE:2 * LANE]

    o1 = softmax_attend(_bf(jnp.where(lane < C_HEAD_DIM, q, 0.0)))
    o2 = softmax_attend(_bf(jnp.where(lane >= C_HEAD_DIM, q, 0.0)))
    o = o1 - lam_ref[0] * o2
    o = o * lax.rsqrt(jnp.mean(o * o, axis=-1, keepdims=True) + 1e-6) * g_ref[...] * (1.0 - lam_init)
    o_ref[...] = o


def diff_attention_mixer(z, lam, lam_init, subln_g, cache=None, layer=None, rope=None):
    B, T, _ = z.shape
    latent = cache is not None
    tq = min(DIFF_Q_ROWS, T)
    head = lambda base: pl.BlockSpec((None, tq, LANE), lambda b, h, i: (b, i, base + h))
    whole = lambda base: pl.BlockSpec((None, T, LANE), lambda b, h, i: (b, 0, base + h))
    in_specs = [pl.BlockSpec(memory_space=pltpu.SMEM), head(ZC_Q), whole(ZC_K), whole(ZC_V)]
    args = [lam.reshape(1), z, z, z]
    if latent:
        P = cache[0].shape[3]
        cspec = pl.BlockSpec((None, None, None, P, LANE), lambda b, h, i: (b, layer, h, 0, 0))
        tq_spec = pl.BlockSpec((tq, LANE), lambda b, h, i: (i, 0))
        tk_spec = pl.BlockSpec((T, LANE), lambda b, h, i: (0, 0))
        in_specs += [cspec, cspec, tq_spec, tq_spec, tk_spec, tk_spec]
        args += [cache[0], cache[1], rope[0], rope[1], rope[0], rope[1]]
    in_specs.append(pl.BlockSpec((1, LANE), lambda b, h, i: (0, 0)))
    args.append(subln_g.reshape(1, LANE))
    return pl.pallas_call(
        partial(_diff_attn_kernel, latent=latent, lam_init=lam_init),
        grid=(B, C_HEADS, T // tq),
        in_specs=in_specs,
        out_specs=pl.BlockSpec((None, tq, LANE), lambda b, h, i: (b, i, h)),
        out_shape=jax.ShapeDtypeStruct((B, T, C_HEADS * LANE), jnp.float32),
        scratch_shapes=([pltpu.VMEM((T, LANE), jnp.bfloat16), pltpu.VMEM((T, 2 * LANE), jnp.bfloat16)]
                        + ([pltpu.VMEM((cache[0].shape[3], 2 * LANE), jnp.bfloat16)] if latent else [])),
        compiler_params=pltpu.CompilerParams(
            dimension_semantics=("arbitrary", "arbitrary", "arbitrary"), vmem_limit_bytes=VMEM_LIMIT_BYTES),
        name="diff_attention",
    )(*args)


def _gqa_kernel(sink_ref, q0_ref, q1_ref, *rest, latent, n_tokens):
    if latent:
        (kp_ref, kn_ref, kx_ref, vp_ref, vn_ref, vx_ref, kc_ref, vc_ref,
         cq_ref, sq_ref, cp_ref, sp_ref, cx_ref, sx_ref, o_ref) = rest
    else:
        kc_ref, vc_ref, o_ref = rest
    kvh = pl.program_id(1)
    n = pl.program_id(2)
    scale = B_HEAD_DIM ** -0.5
    q0, q1 = q0_ref[...], q1_ref[...]
    if latent:
        q0 = _rope(q0, cq_ref[...], sq_ref[...])
        q1 = _rope(q1, cq_ref[...], sq_ref[...])
    q = _bf(jnp.concatenate([q0, q1], axis=0))
    row = lax.broadcasted_iota(jnp.int32, (2 * BLOCK, 1), 0)
    sink = jnp.where(row < BLOCK, sink_ref[2 * kvh], sink_ref[2 * kvh + 1])

    s_c = lax.dot_general(q, _bf(kc_ref[...]), _DOT_NT, preferred_element_type=jnp.float32) * scale
    m = jnp.maximum(jnp.max(s_c, axis=-1, keepdims=True), sink)
    if latent:
        kw = jnp.concatenate([_rope(kp_ref[...], cp_ref[...], sp_ref[...]),
                              _rope(kn_ref[...], cq_ref[...], sq_ref[...]),
                              _rope(kx_ref[...], cx_ref[...], sx_ref[...])], axis=0)
        vw = jnp.concatenate([vp_ref[...], vn_ref[...], vx_ref[...]], axis=0)
        s_w = lax.dot_general(q, _bf(kw), _DOT_NT, preferred_element_type=jnp.float32) * scale
        qpos = n * BLOCK + lax.broadcasted_iota(jnp.int32, s_w.shape, 0) % BLOCK
        kpos = (n - 1) * BLOCK + lax.broadcasted_iota(jnp.int32, s_w.shape, 1)
        valid = (jnp.abs(qpos - kpos) <= WINDOW) & (kpos >= 0) & (kpos < n_tokens)
        s_w = jnp.where(valid, s_w, NEG_INF)
        m = jnp.maximum(m, jnp.max(s_w, axis=-1, keepdims=True))
        e_w = jnp.exp(s_w - m)
    e_c = jnp.exp(s_c - m)
    zsum = jnp.sum(e_c, axis=-1, keepdims=True) + jnp.exp(sink - m)
    o = jnp.dot(_bf(e_c), _bf(vc_ref[...]), preferred_element_type=jnp.float32)
    if latent:
        zsum = zsum + jnp.sum(e_w, axis=-1, keepdims=True)
        o = o + jnp.dot(_bf(e_w), _bf(vw), preferred_element_type=jnp.float32)
    o = o / zsum
    o_ref[:, 0:LANE] = o[0:BLOCK]
    o_ref[:, LANE:2 * LANE] = o[BLOCK:2 * BLOCK]


def gqa_mixer(z, sink, cache=None, layer=None, rope=None):
    B, T, _ = z.shape
    latent = cache is not None
    nb = T // BLOCK
    blk = lambda base, off: pl.BlockSpec(
        (None, BLOCK, LANE), lambda b, kvh, n: (b, jnp.clip(n + off, 0, nb - 1), base + kvh))
    qspec = lambda g: pl.BlockSpec((None, BLOCK, LANE), lambda b, kvh, n: (b, n, ZB_Q + 2 * kvh + g))
    in_specs = [pl.BlockSpec(memory_space=pltpu.SMEM), qspec(0), qspec(1)]
    args = [sink, z, z]
    if latent:
        P = cache[0].shape[3]
        cspec = pl.BlockSpec((None, None, None, P, LANE), lambda b, kvh, n: (b, layer, kvh, 0, 0))
        tab = lambda off: pl.BlockSpec((BLOCK, LANE), lambda b, kvh, n: (jnp.clip(n + off, 0, nb - 1), 0))
        in_specs += [blk(ZB_K, -1), blk(ZB_K, 0), blk(ZB_K, 1), blk(ZB_V, -1), blk(ZB_V, 0), blk(ZB_V, 1),
                     cspec, cspec, tab(0), tab(0), tab(-1), tab(-1), tab(1), tab(1)]
        args += [z] * 6 + [cache[0], cache[1]] + [rope[0], rope[1]] * 3
    else:
        whole = lambda base: pl.BlockSpec((None, T, LANE), lambda b, kvh, n: (b, 0, base + kvh))
        in_specs += [whole(ZB_K), whole(ZB_V)]
        args += [z, z]
    return pl.pallas_call(
        partial(_gqa_kernel, latent=latent, n_tokens=T),
        grid=(B, B_KV_HEADS, nb),
        in_specs=in_specs,
        out_specs=pl.BlockSpec((None, BLOCK, 2 * LANE), lambda b, kvh, n: (b, n, kvh)),
        out_shape=jax.ShapeDtypeStruct((B, T, B_HEADS * LANE), jnp.float32),
        compiler_params=pltpu.CompilerParams(
            dimension_semantics=("arbitrary", "arbitrary", "arbitrary"), vmem_limit_bytes=VMEM_LIMIT_BYTES),
        name="gqa_attention",
    )(*args)


def split_cols(z, sizes):
    return jnp.split(z, [int(s) for s in np.cumsum(sizes)[:-1]], axis=-1)


def layer_norm(x, g=None, b=None):
    mu = jnp.mean(x, axis=-1, keepdims=True)
    var = jnp.mean(jnp.square(x - mu), axis=-1, keepdims=True)
    y = (x - mu) * lax.rsqrt(var + LN_EPS)
    if g is not None:
        y = y * g + b
    return y


def rwkv7_mixer(za, S0_f, S0_b, lp):
    lora_in = za[..., 3 * A_WIDTH + G_LORA:]
    wd_f, wd_b, ad_f, ad_b = split_cols(lora_in, [W_LORA, W_LORA, ICL_LORA, ICL_LORA])
    wz = jnp.stack([mm3(jnp.tanh(wd), lp['a_w_up'][d]) for d, wd in enumerate((wd_f, wd_b))], axis=0)
    az = jnp.stack([mm3(ad, lp['a_a_up'][d]) for d, ad in enumerate((ad_f, ad_b))], axis=0)
    s0t = jnp.swapaxes(jnp.stack([S0_f, S0_b], axis=0), -1, -2)
    ys, sfin_t = rwkv7_chunked(za, wz, az, lp['a_w0'], lp['a_a0'], lp['a_k_k'], lp['a_k_a'], s0t)
    states = jnp.swapaxes(sfin_t, -1, -2)
    out = rwkv_post(ys, za, az, lp['a_a0'], lp['a_k_a'], lp['a_r_k'], lp['a_gn_g'], lp['a_gn_b'], lp['a_g_up'])
    return out, states[0], states[1]


def diff_lambda(lv, layer):
    lam_init = 0.8 - 0.6 * math.exp(-0.3 * layer)
    lam = jnp.exp(jnp.sum(lv[0] * lv[1])) - jnp.exp(jnp.sum(lv[2] * lv[3])) + lam_init
    return lam, lam_init


def adaln_params(cond, lp):
    mod = mm(jax.nn.silu(cond), lp['w_mod']) + lp['b_mod']
    return jnp.split(mod[:, None, :], 6, axis=-1)


def mixer_inputs(x, shift, scale, lp):
    z = ln_mod_matmul(x, scale, shift, lp['w_in'])
    return z, centred_conv3(z, lp['a_conv'])


def merge_mixers(x, yA, yB, yC, z, gate1, shift2, scale2, lp):
    n = x.shape[0] * x.shape[1]
    flat = lambda t: t.reshape(n, t.shape[-1])
    merged = merge_gated(flat(yA), flat(yB), flat(yC), flat(z), lp['p_a'], lp['p_b'], lp['p_c'])
    return out_proj_norm(merged, x, gate1, scale2, shift2, lp['ln1_g'], lp['ln1_b'], lp['w_out'])


def peer_and_norm(part, lp):
    x, ht, gate2 = part
    yt = peer_ffn_tokens(ht, lp['peer_wqt'], lp['peer_keys'], lp['peer_u'], lp['peer_vt'])
    return peer_residual_norm(yt, x, gate2, lp['ln2_g'], lp['ln2_b'])


def context_layer(x, cond, lp, layer):
    Bsz, L, _ = x.shape
    shift1, scale1, gate1, shift2, scale2, gate2 = adaln_params(cond, lp)
    z, zA = mixer_inputs(x, shift1, scale1, lp)
    S0 = jnp.zeros((Bsz, A_HEADS, A_HEAD_DIM, A_HEAD_DIM), jnp.float32)
    yA, S_f, S_b = rwkv7_mixer(zA, S0, S0, lp)
    yB = gqa_mixer(z, lp['b_sink'])
    lam, lam_init = diff_lambda(lp['c_lam'], layer)
    yC = diff_attention_mixer(z, lam, lam_init, lp['c_subln'])
    x_mid, h_mid = merge_mixers(x, yA, yB, yC, z, gate1, shift2, scale2, lp)

    def cache_layout(first_block, n_heads):
        t = z[..., first_block * LANE:(first_block + n_heads) * LANE].reshape(Bsz, L, n_heads, LANE)
        return jnp.transpose(t, (0, 2, 1, 3))
    ctx = (S_f, S_b, cache_layout(ZB_K, B_KV_HEADS), cache_layout(ZB_V, B_KV_HEADS),
           cache_layout(ZC_K, C_HEADS), cache_layout(ZC_V, C_HEADS))
    return (x_mid, h_mid, gate2), ctx


def latent_layer(x, cond, lp, layer, S_f0, S_b0, cache_b, cache_c, rope_b, rope_c):
    shift1, scale1, gate1, shift2, scale2, gate2 = adaln_params(cond, lp)
    z, zA = mixer_inputs(x, shift1, scale1, lp)
    yA, _, _ = rwkv7_mixer(zA, S_f0, S_b0, lp)
    yB = gqa_mixer(z, lp['b_sink'], cache=cache_b, layer=layer, rope=rope_b)
    lam, lam_init = diff_lambda(lp['c_lam'], layer)
    yC = diff_attention_mixer(z, lam, lam_init, lp['c_subln'], cache=cache_c, layer=layer, rope=rope_c)
    x_mid, h_mid = merge_mixers(x, yA, yB, yC, z, gate1, shift2, scale2, lp)
    return x_mid, h_mid, gate2


_BF16_WEIGHTS = ('w_mod', 'w_in', 'a_w_up', 'a_a_up', 'a_g_up', 'p_a', 'p_b', 'p_c', 'w_out', 'peer_u')


def kernel(x_prompt, x_sample, state_a_fwd, state_a_bwd, cache_b_k, cache_b_v, cache_c_k, cache_c_v, c, c_ctx, w_mod, b_mod, w_in, a_conv, a_w0, a_w_up, a_a0, a_a_up, a_g_up, a_k_k, a_k_a, a_r_k, a_gn_g, a_gn_b, b_sink, c_lam, c_subln, p_a, p_b, p_c, w_out, ln1_g, ln1_b, ln2_g, ln2_b, peer_wq, peer_keys, peer_u, peer_v):
    params = dict(w_mod=w_mod, b_mod=b_mod, w_in=w_in, a_conv=a_conv, a_w0=a_w0, a_w_up=a_w_up, a_a0=a_a0,
                  a_a_up=a_a_up, a_g_up=a_g_up, a_k_k=a_k_k, a_k_a=a_k_a, a_r_k=a_r_k, a_gn_g=a_gn_g,
                  a_gn_b=a_gn_b, b_sink=b_sink, c_lam=c_lam, c_subln=c_subln, p_a=p_a, p_b=p_b, p_c=p_c,
                  w_out=w_out, ln1_g=ln1_g, ln1_b=ln1_b, ln2_g=ln2_g, ln2_b=ln2_b, peer_wq=peer_wq,
                  peer_keys=peer_keys, peer_u=peer_u, peer_v=peer_v)
    for name in _BF16_WEIGHTS:
        params[name] = params[name].astype(jnp.bfloat16)
    w_in_bf = params['w_in']
    params['w_in'] = jnp.concatenate([w_in_bf[..., IN_COLS - 3 * D_MODEL:], w_in_bf[..., :IN_COLS - 3 * D_MODEL]], axis=-1)
    params['peer_wqt'] = jnp.swapaxes(params.pop('peer_wq'), 1, 2).astype(jnp.bfloat16)
    params['peer_vt'] = jnp.swapaxes(params.pop('peer_v'), 1, 2).astype(jnp.bfloat16)
    n_lat = x_sample.shape[1]
    rope_b = rope_lane_tables(n_lat, B_HEAD_DIM)
    rope_c = rope_lane_tables(n_lat, C_HEAD_DIM)
    cond_ctx = jnp.broadcast_to(c_ctx[None, :], (x_prompt.shape[0], D_MODEL))
    y_prompt, y_sample = x_prompt, x_sample
    new = [[], [], [], [], [], []]
    for layer in range(DEPTH):
        lp = {name: val[layer] for name, val in params.items()}
        part_ctx, ctx = context_layer(y_prompt, cond_ctx, lp, layer)
        for lst, t in zip(new, ctx):
            lst.append(t)
        part_lat = latent_layer(y_sample, c, lp, layer,
                                state_a_fwd[:, layer], state_a_bwd[:, layer],
                                (cache_b_k, cache_b_v), (cache_c_k, cache_c_v), rope_b, rope_c)
        y_prompt = peer_and_norm(part_ctx, lp)
        y_sample = peer_and_norm(part_lat, lp)
    return (y_prompt, y_sample) + tuple(jnp.stack(lst, axis=1) for lst in new)
```

```python
import math
from functools import partial

import jax
import jax.numpy as jnp
import numpy as np
from jax import lax
from jax.experimental import pallas as pl
from jax.experimental.pallas import tpu as pltpu

D_MODEL = 2048
DEPTH = 4
GRID_W = 64
BLOCK = 128
A_HEADS = 16
A_HEAD_DIM = 64
A_WIDTH = A_HEADS * A_HEAD_DIM
W_LORA = 64
ICL_LORA = 64
G_LORA = 128
A_COLS = 3 * A_WIDTH + G_LORA + 2 * W_LORA + 2 * ICL_LORA
A_GN_EPS = 64e-5
B_HEADS = 4
B_KV_HEADS = 2
B_HEAD_DIM = 128
B_WIDTH = B_HEADS * B_HEAD_DIM
B_KV_WIDTH = B_KV_HEADS * B_HEAD_DIM
WINDOW = 128
C_HEADS = 4
C_HEAD_DIM = 64
C_WIDTH = C_HEADS * 2 * C_HEAD_DIM
IN_COLS = A_COLS + B_WIDTH + 2 * B_KV_WIDTH + 3 * C_WIDTH + 3 * D_MODEL
PEER_HEADS = 8
N_KEYS = 128
PEER_QDIM = 256
PEER_TOPK = 16
ROPE_BASE = 10000.0
LN_EPS = 1e-5
NEG_INF = -1e30
DEEPNORM_ALPHA = (2 * DEPTH) ** 0.25

LANE = 128
VMEM_LIMIT_BYTES = 56 * 1024 * 1024


def _mm_kernel(x_ref, w_ref, o_ref):
    o_ref[...] = jnp.dot(x_ref[...].astype(jnp.bfloat16), w_ref[...],
                         preferred_element_type=jnp.float32)


def _pick_tile(n, cap, unit):
    if n <= cap:
        return n
    best = None
    for t in range(unit, cap + 1, unit):
        if n % t == 0:
            best = t
    assert best is not None, (n, cap, unit)
    return best


def mm(x, w):
    M, K = x.shape
    N = w.shape[1]
    tm = _pick_tile(M, 512, 8)
    tn = _pick_tile(N, 1024, LANE)
    return pl.pallas_call(
        _mm_kernel,
        grid=(N // tn, M // tm),
        in_specs=[pl.BlockSpec((tm, K), lambda j, i: (i, 0)),
                  pl.BlockSpec((K, tn), lambda j, i: (0, j))],
        out_specs=pl.BlockSpec((tm, tn), lambda j, i: (i, j)),
        out_shape=jax.ShapeDtypeStruct((M, N), jnp.float32),
        compiler_params=pltpu.CompilerParams(
            dimension_semantics=("arbitrary", "arbitrary"),
            vmem_limit_bytes=VMEM_LIMIT_BYTES),
        name="mm",
    )(x, w)


def mm3(x, w):
    B, T, K = x.shape
    return mm(x.reshape(B * T, K), w).reshape(B, T, w.shape[1])


ROW_TILE = 1024
NORM_ROW_TILE = 512
Z_GATES = 0
Z_A = 3 * D_MODEL
Z_B = Z_A + A_COLS
Z_C = Z_B + B_WIDTH + 2 * B_KV_WIDTH


def _row_blocks(n_batch, n_time, rows):
    tt = min(n_time, rows)
    bt = rows // tt
    assert n_time % tt == 0 and n_batch % bt == 0, (n_batch, n_time, rows)
    return bt, tt


def _layer_norm_rows(x):
    mu = jnp.mean(x, axis=-1, keepdims=True)
    xc = x - mu
    return xc * lax.rsqrt(jnp.mean(xc * xc, axis=-1, keepdims=True) + LN_EPS)


def _ln_mod_mm_kernel(x_ref, scale_ref, shift_ref, w_ref, o_ref, h_ref):
    bt, tt, K = x_ref.shape

    @pl.when(pl.program_id(2) == 0)
    def _():
        h = _layer_norm_rows(x_ref[...]) * (1.0 + scale_ref[...]) + shift_ref[...]
        h_ref[...] = h.reshape(bt * tt, K).astype(jnp.bfloat16)

    o = jnp.dot(h_ref[...], w_ref[...], preferred_element_type=jnp.float32)
    o_ref[...] = o.reshape(bt, tt, o.shape[-1])


def ln_mod_matmul(x, scale, shift, w):
    B, T, K = x.shape
    N = w.shape[1]
    bt, tt = _row_blocks(B, T, ROW_TILE)
    tn = _pick_tile(N, 1024, LANE)
    vec = pl.BlockSpec((bt, 1, K), lambda b, t, j: (b, 0, 0))
    return pl.pallas_call(
        _ln_mod_mm_kernel,
        grid=(B // bt, T // tt, N // tn),
        in_specs=[pl.BlockSpec((bt, tt, K), lambda b, t, j: (b, t, 0)), vec, vec,
                  pl.BlockSpec((K, tn), lambda b, t, j: (0, j))],
        out_specs=pl.BlockSpec((bt, tt, tn), lambda b, t, j: (b, t, j)),
        out_shape=jax.ShapeDtypeStruct((B, T, N), jnp.float32),
        scratch_shapes=[pltpu.VMEM((bt * tt, K), jnp.bfloat16)],
        compiler_params=pltpu.CompilerParams(
            dimension_semantics=("arbitrary", "arbitrary", "arbitrary"), vmem_limit_bytes=VMEM_LIMIT_BYTES),
        name="ln_mod_matmul",
    )(x, scale, shift, w)


def _merge_kernel(ya_ref, yb_ref, yc_ref, ga_ref, gb_ref, gc_ref, pa_ref, pb_ref, pc_ref, o_ref,
                  ya_bf, yb_bf, yc_bf):
    @pl.when(pl.program_id(1) == 0)
    def _():
        ya_bf[...] = ya_ref[...].astype(jnp.bfloat16)
        yb_bf[...] = yb_ref[...].astype(jnp.bfloat16)
        yc_bf[...] = yc_ref[...].astype(jnp.bfloat16)

    f = lambda y, p: jnp.dot(y[...], p[...], preferred_element_type=jnp.float32)
    o = (jax.nn.sigmoid(ga_ref[...]) * f(ya_bf, pa_ref) + jax.nn.sigmoid(gb_ref[...]) * f(yb_bf, pb_ref)
         + jax.nn.sigmoid(gc_ref[...]) * f(yc_bf, pc_ref))
    o_ref[...] = o.astype(jnp.bfloat16)


def merge_gated(yA, yB, yC, z, p_a, p_b, p_c):
    N = yA.shape[0]
    D = p_a.shape[1]
    tm = _pick_tile(N, ROW_TILE, 8)
    tn = 512
    assert Z_GATES == 0 and D % tn == 0
    ysp = lambda y: pl.BlockSpec((tm, y.shape[1]), lambda i, j: (i, 0))
    gsp = lambda k: pl.BlockSpec((tm, tn), lambda i, j: (i, k * (D // tn) + j))
    psp = lambda p: pl.BlockSpec((p.shape[0], tn), lambda i, j: (0, j))
    return pl.pallas_call(
        _merge_kernel,
        grid=(N // tm, D // tn),
        in_specs=[ysp(yA), ysp(yB), ysp(yC), gsp(0), gsp(1), gsp(2), psp(p_a), psp(p_b), psp(p_c)],
        out_specs=pl.BlockSpec((tm, tn), lambda i, j: (i, j)),
        out_shape=jax.ShapeDtypeStruct((N, D), jnp.bfloat16),
        scratch_shapes=[pltpu.VMEM((tm, y.shape[1]), jnp.bfloat16) for y in (yA, yB, yC)],
        compiler_params=pltpu.CompilerParams(
            dimension_semantics=("arbitrary", "arbitrary"), vmem_limit_bytes=VMEM_LIMIT_BYTES),
        name="merge_gated",
    )(yA, yB, yC, z, z, z, p_a, p_b, p_c)


def _out_norm_kernel(m_ref, x_ref, gate_ref, scale_ref, shift_ref, lng_ref, lnb_ref, w_ref, x1_ref, ht_ref):
    bt, tt, D = x_ref.shape
    y = jnp.dot(m_ref[...].reshape(bt * tt, D), w_ref[...], preferred_element_type=jnp.float32)
    x1 = _layer_norm_rows(DEEPNORM_ALPHA * x_ref[...] + gate_ref[...] * y.reshape(bt, tt, D))
    x1 = x1 * lng_ref[...] + lnb_ref[...]
    x1_ref[...] = x1
    h = _layer_norm_rows(x1) * (1.0 + scale_ref[...]) + shift_ref[...]
    ht_ref[...] = jnp.transpose(h.reshape(bt * tt, D)).astype(jnp.bfloat16)


def out_proj_norm(merged, x, gate1, scale2, shift2, ln_g, ln_b, w_out):
    B, T, D = x.shape
    bt, tt = _row_blocks(B, T, NORM_ROW_TILE)
    rows = pl.BlockSpec((bt, tt, D), lambda b, t: (b, t, 0))
    vec = pl.BlockSpec((bt, 1, D), lambda b, t: (b, 0, 0))
    par = pl.BlockSpec((1, 1, D), lambda b, t: (0, 0, 0))
    return pl.pallas_call(
        _out_norm_kernel,
        grid=(B // bt, T // tt),
        in_specs=[rows, rows, vec, vec, vec, par, par, pl.BlockSpec((D, D), lambda b, t: (0, 0))],
        out_specs=[rows, pl.BlockSpec((D, bt * tt), lambda b, t: (0, b * (T // tt) + t))],
        out_shape=[jax.ShapeDtypeStruct((B, T, D), jnp.float32), jax.ShapeDtypeStruct((D, B * T), jnp.bfloat16)],
        compiler_params=pltpu.CompilerParams(
            dimension_semantics=("arbitrary", "arbitrary"), vmem_limit_bytes=VMEM_LIMIT_BYTES),
        name="out_proj_norm",
    )(merged.reshape(B, T, D), x, gate1, scale2, shift2, ln_g.reshape(1, 1, D), ln_b.reshape(1, 1, D), w_out)


def _peer_norm_kernel(yt_ref, x_ref, gate_ref, lng_ref, lnb_ref, o_ref):
    bt, tt, D = x_ref.shape
    y = jnp.transpose(yt_ref[...]).reshape(bt, tt, D)
    x = _layer_norm_rows(DEEPNORM_ALPHA * x_ref[...] + gate_ref[...] * y)
    o_ref[...] = x * lng_ref[...] + lnb_ref[...]


def peer_residual_norm(yt, x, gate2, ln_g, ln_b):
    B, T, D = x.shape
    bt, tt = _row_blocks(B, T, NORM_ROW_TILE)
    rows = pl.BlockSpec((bt, tt, D), lambda b, t: (b, t, 0))
    par = pl.BlockSpec((1, 1, D), lambda b, t: (0, 0, 0))
    return pl.pallas_call(
        _peer_norm_kernel,
        grid=(B // bt, T // tt),
        in_specs=[pl.BlockSpec((D, bt * tt), lambda b, t: (0, b * (T // tt) + t)), rows,
                  pl.BlockSpec((bt, 1, D), lambda b, t: (b, 0, 0)), par, par],
        out_specs=rows,
        out_shape=jax.ShapeDtypeStruct((B, T, D), jnp.float32),
        compiler_params=pltpu.CompilerParams(
            dimension_semantics=("arbitrary", "arbitrary"), vmem_limit_bytes=VMEM_LIMIT_BYTES),
        name="peer_residual_norm",
    )(yt, x, gate2, ln_g.reshape(1, 1, D), ln_b.reshape(1, 1, D))


RWKV_CHUNK = 64
RWKV_HEADS_PER_STEP = 16

_NN = (((2,), (1,)), ((0,), (0,)))
_NT = (((2,), (2,)), ((0,), (0,)))
_TN = (((1,), (1,)), ((0,), (0,)))


def _split2(x):
    hi = x.astype(jnp.bfloat16)
    lo = (x - hi.astype(jnp.float32)).astype(jnp.bfloat16)
    return hi, lo


def _dot3(a, b, dims=_NN):
    f = lambda x, y: lax.dot_general(x, y, dims, preferred_element_type=jnp.float32)
    return f(a[0], b[0]) + (f(a[0], b[1]) + f(a[1], b[0]))


def _dot1(a, b, dims=_NN):
    return lax.dot_general(a[0], b[0], dims, preferred_element_type=jnp.float32)


def _split_heads(x):
    n = A_HEAD_DIM
    return jnp.stack([x[:, h * n:(h + 1) * n] for h in range(x.shape[1] // n)], axis=0)


def _lora_half(x, d, w_ref):
    lane = lax.broadcasted_iota(jnp.int32, x.shape, 1)
    xd = jnp.where(lane // (LANE // 2) == d, x, 0.0).astype(jnp.bfloat16)
    return jnp.dot(xd, w_ref[...], preferred_element_type=jnp.float32)


def _rwkv_chunk_kernel(zr_ref, zk_ref, zv_ref, wd_ref, ad_ref, wup_ref, aup_ref, w0_ref, a0_ref, kk_ref, ka_ref,
                       s0_ref, y_ref, sfin_ref, st_ref):
    d = pl.program_id(0)
    c = pl.program_id(3)
    n_chunks = pl.num_programs(3)
    HB = st_ref.shape[0]
    C = zr_ref.shape[0]

    @pl.when(c == 0)
    def _():
        st_ref[...] = s0_ref[...]

    k_all = zk_ref[...]
    lw_all = -math.exp(-0.5) * jax.nn.sigmoid(_lora_half(jnp.tanh(wd_ref[...]), d, wup_ref) + w0_ref[...])
    icl_all = jax.nn.sigmoid(_lora_half(ad_ref[...], d, aup_ref) + a0_ref[...])
    r, v = _split_heads(zr_ref[...]), _split_heads(zv_ref[...])
    lw = _split_heads(lw_all)
    k = _split_heads(k_all * (1.0 + (icl_all - 1.0) * ka_ref[...]))
    kk = _split_heads(k_all * kk_ref[...])
    kk = kk * lax.rsqrt(jnp.sum(kk * kk, axis=-1, keepdims=True) + 1e-12)
    a = -kk
    b = kk * _split_heads(icl_all)

    ti = lax.broadcasted_iota(jnp.int32, (HB, C, C), 1)
    si = lax.broadcasted_iota(jnp.int32, (HB, C, C), 2)
    diff = (si - ti) * (1 - 2 * d)
    m_incl = diff <= 0
    m_strict = diff < 0
    m_incl_bf = jnp.where(m_incl, 1.0, 0.0).astype(jnp.bfloat16)
    eye = jnp.where(diff == 0, 1.0, 0.0).astype(jnp.float32)

    st = st_ref[...]

    lw_hi, lw_lo = _split2(lw)
    lw_lo2 = (lw - lw_hi.astype(jnp.float32) - lw_lo.astype(jnp.float32)).astype(jnp.bfloat16)
    f = lambda y: lax.dot_general(m_incl_bf, y, _NN, preferred_element_type=jnp.float32)
    cum = f(lw_hi) + (f(lw_lo) + f(lw_lo2))
    total = jnp.sum(lw, axis=1, keepdims=True)
    e_cum = jnp.exp(cum)
    e_inv = jnp.exp(-cum)
    e_tot = jnp.exp(total)
    hi = lambda x: (x.astype(jnp.bfloat16),)
    at = hi(a * jnp.exp(cum - lw))
    rt = _split2(r * e_cum)
    bt_f = b * e_inv
    kt_f = k * e_inv
    bt, kt = hi(bt_f), hi(kt_f)
    bh, kh = _split2(bt_f * e_tot), _split2(kt_f * e_tot)
    vs = _split2(v)

    l_ab = jnp.where(m_strict, _dot1(at, bt, _NT), 0.0)
    l_ak = jnp.where(m_strict, _dot1(at, kt, _NT), 0.0)
    m_rb = jnp.where(m_incl, _dot1(rt, bt, _NT), 0.0)
    m_rk = jnp.where(m_incl, _dot1(rt, kt, _NT), 0.0)

    tm = eye + l_ab
    lp = hi(l_ab)
    for _ in range(int(math.log2(C)) - 1):
        lp = hi(_dot1(lp, lp))
        tm = tm + _dot1(hi(tm), lp)
    tms = hi(tm)
    p = _split2(_dot1(tms, at))
    q = _dot1(tms, hi(_dot1(hi(l_ak), vs)))

    sts = _split2(st)
    u = _split2(_dot3(p, sts) + q)
    y = _dot3(rt, sts) + _dot3(_split2(m_rb), u) + _dot3(_split2(m_rk), vs)
    y_ref[...] = jnp.concatenate([y[h] for h in range(HB)], axis=-1)
    st_ref[...] = jnp.swapaxes(e_tot, 1, 2) * st + _dot3(bh, u, _TN) + _dot3(kh, vs, _TN)

    @pl.when(c == n_chunks - 1)
    def _():
        sfin_ref[...] = st_ref[...]


def rwkv7_chunked(zc, w_up, a_up, w0, a0, k_k, k_a, s0t):
    B, T, _ = zc.shape
    D = 2
    N, H, W = A_HEAD_DIM, A_HEADS, A_WIDTH
    C, HB = RWKV_CHUNK, RWKV_HEADS_PER_STEP
    G = H // HB
    lanes = HB * N
    assert T % C == 0 and H % HB == 0 and 2 * W_LORA == LANE and 2 * ICL_LORA == LANE
    n_chunks = T // C
    lora_block = (3 * W + G_LORA) // LANE
    chunk = lambda d, c: c + d * (n_chunks - 1 - 2 * c)
    zcol = lambda base: pl.BlockSpec((None, C, lanes), lambda d, bb, g, c: (bb, chunk(d, c), base * G + g))
    lora = lambda off: pl.BlockSpec((None, C, LANE), lambda d, bb, g, c: (bb, chunk(d, c), lora_block + off))
    up = pl.BlockSpec((LANE, lanes), lambda d, bb, g, c: (0, g))
    per_dir = pl.BlockSpec((None, None, C, lanes), lambda d, bb, g, c: (d, bb, chunk(d, c), g))
    dir_row = pl.BlockSpec((None, 1, lanes), lambda d, bb, g, c: (d, 0, g))
    row = pl.BlockSpec((1, lanes), lambda d, bb, g, c: (0, g))
    state = pl.BlockSpec((None, None, HB, N, N), lambda d, bb, g, c: (d, bb, g, 0, 0))
    return pl.pallas_call(
        _rwkv_chunk_kernel,
        grid=(D, B, G, n_chunks),
        in_specs=[zcol(0), zcol(1), zcol(2), lora(0), lora(1), up, up, dir_row, dir_row, row, row, state],
        out_specs=[per_dir, state],
        out_shape=[jax.ShapeDtypeStruct((D, B, T, W), jnp.float32),
                   jax.ShapeDtypeStruct((D, B, H, N, N), jnp.float32)],
        scratch_shapes=[pltpu.VMEM((HB, N, N), jnp.float32)],
        compiler_params=pltpu.CompilerParams(
            dimension_semantics=("arbitrary", "arbitrary", "arbitrary", "arbitrary"),
            vmem_limit_bytes=VMEM_LIMIT_BYTES),
        name="rwkv7_chunked",
    )(zc, zc, zc, zc, zc, w_up.reshape(LANE, W), a_up.reshape(LANE, W), w0.reshape(D, 1, W), a0.reshape(D, 1, W),
      k_k.reshape(1, W), k_a.reshape(1, W), s0t)


CONV_COLS = 384
CONV_ROWS = 2048
SUBLANE = 8


def _conv3_kernel(x_ref, prev_ref, next_ref, w_ref, o_ref):
    t = pl.program_id(1)
    x = x_ref[...]
    tt = x.shape[0]
    row = lax.broadcasted_iota(jnp.int32, x.shape, 0)
    before = jnp.where(t > 0, prev_ref[SUBLANE - 1:SUBLANE, :], 0.0)
    after = jnp.where(t < pl.num_programs(1) - 1, next_ref[0:1, :], 0.0)
    x_prev = jnp.where(row == 0, before, pltpu.roll(x, 1, axis=0))
    x_next = jnp.where(row == tt - 1, after, pltpu.roll(x, tt - 1, axis=0))
    o_ref[...] = x_prev * w_ref[0:1, :] + x * w_ref[1:2, :] + x_next * w_ref[2:3, :]


def centred_conv3(z, w):
    B, T, _ = z.shape
    tt = min(T, CONV_ROWS)
    cb = CONV_COLS
    assert Z_A % cb == 0 and A_COLS % cb == 0 and T % tt == 0
    base = Z_A // cb
    groups = tt // SUBLANE
    last = T // SUBLANE - 1
    return pl.pallas_call(
        _conv3_kernel,
        grid=(B, T // tt, A_COLS // cb),
        in_specs=[pl.BlockSpec((None, tt, cb), lambda b, t, j: (b, t, base + j)),
                  pl.BlockSpec((None, SUBLANE, cb), lambda b, t, j: (b, jnp.maximum(t * groups - 1, 0), base + j)),
                  pl.BlockSpec((None, SUBLANE, cb), lambda b, t, j: (b, jnp.minimum((t + 1) * groups, last), base + j)),
                  pl.BlockSpec((3, cb), lambda b, t, j: (0, j))],
        out_specs=pl.BlockSpec((None, tt, cb), lambda b, t, j: (b, t, j)),
        out_shape=jax.ShapeDtypeStruct((B, T, A_COLS), jnp.float32),
        compiler_params=pltpu.CompilerParams(
            dimension_semantics=("arbitrary", "arbitrary", "arbitrary"), vmem_limit_bytes=VMEM_LIMIT_BYTES),
        name="conv3",
    )(z, z, z, w)


def _rwkv_post_kernel(yf_ref, yb_ref, r_ref, k_ref, v_ref, gd_ref, ad_ref, aup_ref, a0f_ref, a0b_ref,
                      ka_ref, rk_ref, gng_ref, gnb_ref, gup_ref, o_ref, ones_ref):
    W = o_ref.shape[-1]
    rows = o_ref.shape[0] * o_ref.shape[1]

    @pl.when((pl.program_id(0) == 0) & (pl.program_id(1) == 0))
    def _():
        hr = lax.broadcasted_iota(jnp.int32, (W, W), 0) // A_HEAD_DIM
        hc = lax.broadcasted_iota(jnp.int32, (W, W), 1) // A_HEAD_DIM
        ones_ref[...] = jnp.where(hr == hc, 1.0, 0.0).astype(jnp.bfloat16)

    def head_sum(x):
        hi, lo = _split2(x)
        f = lambda p: jnp.dot(p, ones_ref[...], preferred_element_type=jnp.float32)
        return f(hi) + f(lo)

    flat = lambda ref: ref[...].reshape(rows, ref.shape[-1])
    inv_n = 1.0 / A_HEAD_DIM
    y = flat(yf_ref) + flat(yb_ref)
    yc = y - head_sum(y) * inv_n
    var = head_sum(yc * yc) * inv_n
    yn = yc * lax.rsqrt(var + A_GN_EPS) * gng_ref[...] + gnb_ref[...]
    ad = flat(ad_ref)
    icl_sum = (jax.nn.sigmoid(_lora_half(ad, 0, aup_ref) + a0f_ref[...])
               + jax.nn.sigmoid(_lora_half(ad, 1, aup_ref) + a0b_ref[...]))
    kd_sum = flat(k_ref) * (2.0 + (icl_sum - 2.0) * ka_ref[...])
    bonus = head_sum(flat(r_ref) * kd_sum * rk_ref[...]) * flat(v_ref)
    g = jnp.dot(jax.nn.sigmoid(flat(gd_ref)).astype(jnp.bfloat16), gup_ref[...], preferred_element_type=jnp.float32)
    o_ref[...] = ((yn + bonus) * g).reshape(o_ref.shape)


def rwkv_post(ys, za, a_up, a0, k_a, r_k, gn_g, gn_b, g_up):
    _, B, T, W = ys.shape
    bt, tt = _row_blocks(B, T, NORM_ROW_TILE)
    both = lambda d: pl.BlockSpec((None, bt, tt, W), lambda b, t: (d, b, t, 0))
    zcol = lambda j: pl.BlockSpec((bt, tt, W), lambda b, t: (b, t, j))
    dvec = lambda d: pl.BlockSpec((None, 1, W), lambda b, t: (d, 0, 0))
    vec = pl.BlockSpec((1, W), lambda b, t: (0, 0))
    assert (3 * W) % G_LORA == 0
    return pl.pallas_call(
        _rwkv_post_kernel,
        grid=(B // bt, T // tt),
        in_specs=[both(0), both(1), zcol(0), zcol(1), zcol(2),
                  pl.BlockSpec((bt, tt, G_LORA), lambda b, t: (b, t, 3 * W // G_LORA)),
                  pl.BlockSpec((bt, tt, LANE), lambda b, t: (b, t, (3 * W + G_LORA) // LANE + 1)),
                  pl.BlockSpec((LANE, W), lambda b, t: (0, 0)), dvec(0), dvec(1), vec, vec, vec, vec,
                  pl.BlockSpec((G_LORA, W), lambda b, t: (0, 0))],
        out_specs=pl.BlockSpec((bt, tt, W), lambda b, t: (b, t, 0)),
        out_shape=jax.ShapeDtypeStruct((B, T, W), jnp.float32),
        scratch_shapes=[pltpu.VMEM((W, W), jnp.bfloat16)],
        compiler_params=pltpu.CompilerParams(
            dimension_semantics=("arbitrary", "arbitrary"), vmem_limit_bytes=VMEM_LIMIT_BYTES),
        name="rwkv_post",
    )(ys, ys, za, za, za, za, za, a_up.reshape(LANE, W), a0.reshape(2, 1, W), a0.reshape(2, 1, W),
      k_a.reshape(1, W), r_k.reshape(1, W), gn_g.reshape(1, W), gn_b.reshape(1, W), g_up)


PEER_ROUTE_TOKENS = 256
PEER_TOKENS = 512
PEER_E1_PER_STEP = 8
PEER_E1_PER_PART = 2
PEER_CAND_ROWS = 56


def _top_rows(s_ref, n_rows, k, emit):
    for a in range(k):
        s = s_ref[0:n_rows, :]
        mx = jnp.max(s, axis=0, keepdims=True)
        emit(a, mx)
        if a + 1 < k:
            s_ref[0:n_rows, :] = jnp.where(s == mx, NEG_INF, s)


def _peer_route_kernel(ht_ref, wqt_ref, k1_ref, k2_ref, s1_ref, s2_ref, g1_ref, e2_ref, th_ref,
                       work_ref, t1_ref, t2_ref, cand_ref):
    half = PEER_QDIM // 2
    qt = jnp.dot(wqt_ref[...], ht_ref[...], preferred_element_type=jnp.float32)
    for h in range(PEER_HEADS):
        for which, (k_ref, s_out, t_ref) in enumerate(((k1_ref, s1_ref, t1_ref), (k2_ref, s2_ref, t2_ref))):
            q = qt[h * PEER_QDIM + which * half: h * PEER_QDIM + (which + 1) * half, :]
            q_hi, q_lo = _split2(q)
            k_hi, k_lo = _split2(k_ref[h])
            f = lambda x, y: jnp.dot(x, y, preferred_element_type=jnp.float32)
            s = f(k_hi, q_hi) + (f(k_hi, q_lo) + f(k_lo, q_hi))
            s_out[h] = s
            work_ref[...] = s

            def emit(a, mx, t_ref=t_ref):
                t_ref[a:a + 1, :] = mx
            _top_rows(work_ref, N_KEYS, PEER_TOPK, emit)
        t1 = t1_ref[...]
        t2 = t2_ref[...]
        m1 = t1[0:1, :]
        m2 = t2[0:1, :]
        row = 0
        for a in range(PEER_TOPK):
            nb = PEER_TOPK // (a + 1)
            cand_ref[row:row + nb, :] = t1[a:a + 1, :] + t2[0:nb, :]
            row += nb
        cand_ref[row:PEER_CAND_ROWS, :] = jnp.full((PEER_CAND_ROWS - row, t1.shape[1]), NEG_INF, jnp.float32)
        acc = {}

        def emit_c(a, mx):
            e = jnp.exp(mx - (m1 + m2))
            acc['z'] = e if a == 0 else acc['z'] + e
            acc['th'] = mx
        _top_rows(cand_ref, PEER_CAND_ROWS, PEER_TOPK, emit_c)
        th_ref[h:h + 1, :] = acc['th']
        g1_ref[h] = jnp.exp(s1_ref[h] - m1) / acc['z']
        e2_ref[h] = jnp.exp(s2_ref[h] - m2)


def peer_route(ht, wqt, keys):
    D, N = ht.shape
    tm = PEER_ROUTE_TOKENS
    assert N % tm == 0
    big = jax.ShapeDtypeStruct((PEER_HEADS, N_KEYS, N), jnp.float32)
    big_spec = pl.BlockSpec((PEER_HEADS, N_KEYS, tm), lambda i: (0, 0, i))
    key_spec = pl.BlockSpec((PEER_HEADS, N_KEYS, PEER_QDIM // 2), lambda i: (0, 0, 0))
    return pl.pallas_call(
        _peer_route_kernel,
        grid=(N // tm,),
        in_specs=[pl.BlockSpec((D, tm), lambda i: (0, i)),
                  pl.BlockSpec((PEER_HEADS * PEER_QDIM, D), lambda i: (0, 0)),
                  key_spec, key_spec],
        out_specs=[big_spec, big_spec, big_spec, big_spec, pl.BlockSpec((PEER_HEADS, tm), lambda i: (0, i))],
        out_shape=[big, big, big, big, jax.ShapeDtypeStruct((PEER_HEADS, N), jnp.float32)],
        scratch_shapes=[pltpu.VMEM((N_KEYS, tm), jnp.float32),
                        pltpu.VMEM((PEER_TOPK, tm), jnp.float32),
                        pltpu.VMEM((PEER_TOPK, tm), jnp.float32),
                        pltpu.VMEM((PEER_CAND_ROWS, tm), jnp.float32)],
        compiler_params=pltpu.CompilerParams(
            dimension_semantics=("arbitrary",), vmem_limit_bytes=VMEM_LIMIT_BYTES),
        name="peer_route",
    )(ht, wqt, keys[0], keys[1])


def _gelu_tanh(x):
    return 0.5 * x * (1.0 + jnp.tanh(math.sqrt(2.0 / math.pi) * (x + 0.044715 * (x * x * x))))


def _peer_expert_kernel(ht_ref, u_ref, vt_ref, s1_ref, s2_ref, g1_ref, e2_ref, th_ref, o_ref, wg_ref):
    j = pl.program_id(1)
    tm = ht_ref.shape[1]

    @pl.when(j == 0)
    def _():
        o_ref[...] = jnp.zeros_like(o_ref)

    e1_rows = pl.ds(pl.multiple_of(j * PEER_E1_PER_STEP, PEER_E1_PER_STEP), PEER_E1_PER_STEP)
    ht = ht_ref[...]
    n_parts = PEER_E1_PER_STEP // PEER_E1_PER_PART
    part_slice = lambda p: slice(p * PEER_E1_PER_PART * N_KEYS, (p + 1) * PEER_E1_PER_PART * N_KEYS)
    activation = lambda p: _gelu_tanh(jnp.dot(u_ref[part_slice(p), :], ht, preferred_element_type=jnp.float32))
    act_next = activation(0)
    for part in range(n_parts):
        act = act_next
        if part + 1 < n_parts:
            act_next = activation(part + 1)
        for tc in range(tm // LANE):
            cols = slice(tc * LANE, (tc + 1) * LANE)
            s1_rows = [s1_ref[h, e1_rows, cols] for h in range(PEER_HEADS)]
            g1_rows = [g1_ref[h, e1_rows, cols] for h in range(PEER_HEADS)]
            for ee in range(PEER_E1_PER_PART):
                e = part * PEER_E1_PER_PART + ee
                w = None
                for h in range(PEER_HEADS):
                    score = s2_ref[h, :, cols] + s1_rows[h][e:e + 1, :]
                    gate = e2_ref[h, :, cols] * g1_rows[h][e:e + 1, :]
                    term = jnp.where(score >= th_ref[h:h + 1, cols], gate, 0.0)
                    w = term if w is None else w + term
                rows = slice(ee * N_KEYS, (ee + 1) * N_KEYS)
                wg_ref[part, rows, cols] = (w * act[rows, cols]).astype(jnp.bfloat16)
        if part >= 1:
            o_ref[...] += jnp.dot(vt_ref[:, part_slice(part - 1)], wg_ref[part - 1],
                                  preferred_element_type=jnp.float32)
    o_ref[...] += jnp.dot(vt_ref[:, part_slice(n_parts - 1)], wg_ref[n_parts - 1],
                          preferred_element_type=jnp.float32)


def peer_experts(ht, u, vt, s1, s2, g1, e2, th):
    D, N = ht.shape
    E = u.shape[0]
    tm = PEER_TOKENS
    te = PEER_E1_PER_STEP * N_KEYS
    assert N % tm == 0 and E % te == 0
    big_spec = pl.BlockSpec((PEER_HEADS, N_KEYS, tm), lambda i, j: (0, 0, i))
    return pl.pallas_call(
        _peer_expert_kernel,
        grid=(N // tm, E // te),
        in_specs=[pl.BlockSpec((D, tm), lambda i, j: (0, i)),
                  pl.BlockSpec((te, D), lambda i, j: (j, 0)),
                  pl.BlockSpec((D, te), lambda i, j: (0, j)),
                  big_spec, big_spec, big_spec, big_spec,
                  pl.BlockSpec((PEER_HEADS, tm), lambda i, j: (0, i))],
        out_specs=pl.BlockSpec((D, tm), lambda i, j: (0, i)),
        out_shape=jax.ShapeDtypeStruct((D, N), jnp.float32),
        scratch_shapes=[pltpu.VMEM((PEER_E1_PER_STEP // PEER_E1_PER_PART, PEER_E1_PER_PART * N_KEYS, tm), jnp.bfloat16)],
        compiler_params=pltpu.CompilerParams(
            dimension_semantics=("arbitrary", "arbitrary"), vmem_limit_bytes=VMEM_LIMIT_BYTES),
        name="peer_experts",
    )(ht, u, vt, s1, s2, g1, e2, th)


def peer_ffn_tokens(ht, wqt, keys, u, vt):
    s1, s2, g1, e2, th = peer_route(ht, wqt, keys)
    return peer_experts(ht, u, vt, s1, s2, g1, e2, th)


ZB_Q = Z_B // LANE
ZB_K = ZB_Q + B_HEADS
ZB_V = ZB_K + B_KV_HEADS
ZC_Q = ZB_V + B_KV_HEADS
ZC_K = ZC_Q + C_HEADS
ZC_V = ZC_K + C_HEADS
DIFF_Q_ROWS = 256

_DOT_NT = (((1,), (1,)), ((), ()))


def rope_lane_tables(n_tokens, head_dim):
    rows = n_tokens // GRID_W
    row_pos = jnp.repeat(jnp.arange(rows, dtype=jnp.float32), GRID_W)
    col_pos = jnp.tile(jnp.arange(GRID_W, dtype=jnp.float32), rows)
    n_freq = head_dim // 4
    freqs = ROPE_BASE ** (-jnp.arange(n_freq, dtype=jnp.float32) / n_freq)
    ang = jnp.concatenate([row_pos[:, None] * freqs, col_pos[:, None] * freqs], axis=-1)
    cos = jnp.repeat(jnp.cos(ang), 2, axis=-1)
    sin = jnp.repeat(jnp.sin(ang), 2, axis=-1) * jnp.tile(jnp.array([-1.0, 1.0], jnp.float32), head_dim // 2)
    reps = LANE // head_dim
    return jnp.tile(cos, (1, reps)), jnp.tile(sin, (1, reps))


def _rope(x, cos, sin_signed):
    lane = lax.broadcasted_iota(jnp.int32, x.shape, 1)
    partner = jnp.where(lane % 2 == 0, pltpu.roll(x, LANE - 1, axis=1), pltpu.roll(x, 1, axis=1))
    return x * cos + partner * sin_signed


def _bf(x):
    return x.astype(jnp.bfloat16)


def _diff_attn_kernel(lam_ref, q_ref, k_ref, v_ref, *rest, latent, lam_init):
    if latent:
        kc_ref, vc_ref, cq_ref, sq_ref, ck_ref, sk_ref, g_ref, o_ref, kr_ref = rest
    else:
        g_ref, o_ref, kr_ref = rest
    i = pl.program_id(2)

    @pl.when(i == 0)
    def _():
        k = k_ref[...]
        kr_ref[...] = _bf(_rope(k, ck_ref[...], sk_ref[...]) if latent else k)

    q = q_ref[...]
    if latent:
        q = _rope(q, cq_ref[...], sq_ref[...])
    lane = lax.broadcasted_iota(jnp.int32, q.shape, 1)
    scale = C_HEAD_DIM ** -0.5
    lam = lam_ref[0]
    kr = kr_ref[...]
    vb = _bf(v_ref[...])
    if latent:
        kcb, vcb = _bf(kc_ref[...]), _bf(vc_ref[...])

    def softmax_parts(qh):
        s = lax.dot_general(qh, kr, _DOT_NT, preferred_element_type=jnp.float32) * scale
        m = jnp.max(s, axis=-1, keepdims=True)
        if latent:
            sc = lax.dot_general(qh, kcb, _DOT_NT, preferred_element_type=jnp.float32) * scale
            m = jnp.maximum(m, jnp.max(sc, axis=-1, keepdims=True))
            ec = jnp.exp(sc - m)
        e = jnp.exp(s - m)
        z = jnp.sum(e, axis=-1, keepdims=True)
        if latent:
            z = z + jnp.sum(ec, axis=-1, keepdims=True)
            return e, ec, 1.0 / z
        return e, None, 1.0 / z

    e1, ec1, r1 = softmax_parts(_bf(jnp.where(lane < C_HEAD_DIM, q, 0.0)))
    e2, ec2, r2 = softmax_parts(_bf(jnp.where(lane >= C_HEAD_DIM, q, 0.0)))
    r2 = lam * r2
    o = jnp.dot(_bf(e1 * r1 - e2 * r2), vb, preferred_element_type=jnp.float32)
    if latent:
        o = o + jnp.dot(_bf(ec1 * r1 - ec2 * r2), vcb, preferred_element_type=jnp.float32)
    o = o * lax.rsqrt(jnp.mean(o * o, axis=-1, keepdims=True) + 1e-6) * g_ref[...] * (1.0 - lam_init)
    o_ref[...] = o


def diff_attention_mixer(z, lam, lam_init, subln_g, cache=None, layer=None, rope=None):
    B, T, _ = z.shape
    latent = cache is not None
    tq = min(DIFF_Q_ROWS, T)
    head = lambda base: pl.BlockSpec((None, tq, LANE), lambda b, h, i: (b, i, base + h))
    whole = lambda base: pl.BlockSpec((None, T, LANE), lambda b, h, i: (b, 0, base + h))
    in_specs = [pl.BlockSpec(memory_space=pltpu.SMEM), head(ZC_Q), whole(ZC_K), whole(ZC_V)]
    args = [lam.reshape(1), z, z, z]
    if latent:
        P = cache[0].shape[3]
        cspec = pl.BlockSpec((None, None, None, P, LANE), lambda b, h, i: (b, layer, h, 0, 0))
        tq_spec = pl.BlockSpec((tq, LANE), lambda b, h, i: (i, 0))
        tk_spec = pl.BlockSpec((T, LANE), lambda b, h, i: (0, 0))
        in_specs += [cspec, cspec, tq_spec, tq_spec, tk_spec, tk_spec]
        args += [cache[0], cache[1], rope[0], rope[1], rope[0], rope[1]]
    in_specs.append(pl.BlockSpec((1, LANE), lambda b, h, i: (0, 0)))
    args.append(subln_g.reshape(1, LANE))
    return pl.pallas_call(
        partial(_diff_attn_kernel, latent=latent, lam_init=lam_init),
        grid=(B, C_HEADS, T // tq),
        in_specs=in_specs,
        out_specs=pl.BlockSpec((None, tq, LANE), lambda b, h, i: (b, i, h)),
        out_shape=jax.ShapeDtypeStruct((B, T, C_HEADS * LANE), jnp.float32),
        scratch_shapes=[pltpu.VMEM((T, LANE), jnp.bfloat16)],
        compiler_params=pltpu.CompilerParams(
            dimension_semantics=("arbitrary", "arbitrary", "arbitrary"), vmem_limit_bytes=VMEM_LIMIT_BYTES),
        name="diff_attention",
    )(*args)


def _gqa_kernel(sink_ref, q0_ref, q1_ref, *rest, latent, n_tokens):
    if latent:
        (kp_ref, kn_ref, kx_ref, vp_ref, vn_ref, vx_ref, kc_ref, vc_ref,
         cq_ref, sq_ref, cp_ref, sp_ref, cx_ref, sx_ref, o_ref) = rest
    else:
        kc_ref, vc_ref, o_ref = rest
    kvh = pl.program_id(1)
    n = pl.program_id(2)
    scale = B_HEAD_DIM ** -0.5
    q0, q1 = q0_ref[...], q1_ref[...]
    if latent:
        q0 = _rope(q0, cq_ref[...], sq_ref[...])
        q1 = _rope(q1, cq_ref[...], sq_ref[...])
    q = _bf(jnp.concatenate([q0, q1], axis=0))
    row = lax.broadcasted_iota(jnp.int32, (2 * BLOCK, 1), 0)
    sink = jnp.where(row < BLOCK, sink_ref[2 * kvh], sink_ref[2 * kvh + 1])

    s_c = lax.dot_general(q, _bf(kc_ref[...]), _DOT_NT, preferred_element_type=jnp.float32) * scale
    m = jnp.maximum(jnp.max(s_c, axis=-1, keepdims=True), sink)
    if latent:
        kw = jnp.concatenate([_rope(kp_ref[...], cp_ref[...], sp_ref[...]),
                              _rope(kn_ref[...], cq_ref[...], sq_ref[...]),
                              _rope(kx_ref[...], cx_ref[...], sx_ref[...])], axis=0)
        vw = jnp.concatenate([vp_ref[...], vn_ref[...], vx_ref[...]], axis=0)
        s_w = lax.dot_general(q, _bf(kw), _DOT_NT, preferred_element_type=jnp.float32) * scale
        qpos = n * BLOCK + lax.broadcasted_iota(jnp.int32, s_w.shape, 0) % BLOCK
        kpos = (n - 1) * BLOCK + lax.broadcasted_iota(jnp.int32, s_w.shape, 1)
        valid = (jnp.abs(qpos - kpos) <= WINDOW) & (kpos >= 0) & (kpos < n_tokens)
        s_w = jnp.where(valid, s_w, NEG_INF)
        m = jnp.maximum(m, jnp.max(s_w, axis=-1, keepdims=True))
        e_w = jnp.exp(s_w - m)
    e_c = jnp.exp(s_c - m)
    zsum = jnp.sum(e_c, axis=-1, keepdims=True) + jnp.exp(sink - m)
    o = jnp.dot(_bf(e_c), _bf(vc_ref[...]), preferred_element_type=jnp.float32)
    if latent:
        zsum = zsum + jnp.sum(e_w, axis=-1, keepdims=True)
        o = o + jnp.dot(_bf(e_w), _bf(vw), preferred_element_type=jnp.float32)
    o = o / zsum
    o_ref[:, 0:LANE] = o[0:BLOCK]
    o_ref[:, LANE:2 * LANE] = o[BLOCK:2 * BLOCK]


def gqa_mixer(z, sink, cache=None, layer=None, rope=None):
    B, T, _ = z.shape
    latent = cache is not None
    nb = T // BLOCK
    blk = lambda base, off: pl.BlockSpec(
        (None, BLOCK, LANE), lambda b, kvh, n: (b, jnp.clip(n + off, 0, nb - 1), base + kvh))
    qspec = lambda g: pl.BlockSpec((None, BLOCK, LANE), lambda b, kvh, n: (b, n, ZB_Q + 2 * kvh + g))
    in_specs = [pl.BlockSpec(memory_space=pltpu.SMEM), qspec(0), qspec(1)]
    args = [sink, z, z]
    if latent:
        P = cache[0].shape[3]
        cspec = pl.BlockSpec((None, None, None, P, LANE), lambda b, kvh, n: (b, layer, kvh, 0, 0))
        tab = lambda off: pl.BlockSpec((BLOCK, LANE), lambda b, kvh, n: (jnp.clip(n + off, 0, nb - 1), 0))
        in_specs += [blk(ZB_K, -1), blk(ZB_K, 0), blk(ZB_K, 1), blk(ZB_V, -1), blk(ZB_V, 0), blk(ZB_V, 1),
                     cspec, cspec, tab(0), tab(0), tab(-1), tab(-1), tab(1), tab(1)]
        args += [z] * 6 + [cache[0], cache[1]] + [rope[0], rope[1]] * 3
    else:
        whole = lambda base: pl.BlockSpec((None, T, LANE), lambda b, kvh, n: (b, 0, base + kvh))
        in_specs += [whole(ZB_K), whole(ZB_V)]
        args += [z, z]
    return pl.pallas_call(
        partial(_gqa_kernel, latent=latent, n_tokens=T),
        grid=(B, B_KV_HEADS, nb),
        in_specs=in_specs,
        out_specs=pl.BlockSpec((None, BLOCK, 2 * LANE), lambda b, kvh, n: (b, n, kvh)),
        out_shape=jax.ShapeDtypeStruct((B, T, B_HEADS * LANE), jnp.float32),
        compiler_params=pltpu.CompilerParams(
            dimension_semantics=("arbitrary", "arbitrary", "arbitrary"), vmem_limit_bytes=VMEM_LIMIT_BYTES),
        name="gqa_attention",
    )(*args)


def split_cols(z, sizes):
    return jnp.split(z, [int(s) for s in np.cumsum(sizes)[:-1]], axis=-1)


def layer_norm(x, g=None, b=None):
    mu = jnp.mean(x, axis=-1, keepdims=True)
    var = jnp.mean(jnp.square(x - mu), axis=-1, keepdims=True)
    y = (x - mu) * lax.rsqrt(var + LN_EPS)
    if g is not None:
        y = y * g + b
    return y


def rwkv7_mixer(za, S0_f, S0_b, lp):
    s0t = jnp.swapaxes(jnp.stack([S0_f, S0_b], axis=0), -1, -2)
    ys, sfin_t = rwkv7_chunked(za, lp['a_w_up'], lp['a_a_up'], lp['a_w0'], lp['a_a0'], lp['a_k_k'], lp['a_k_a'], s0t)
    states = jnp.swapaxes(sfin_t, -1, -2)
    out = rwkv_post(ys, za, lp['a_a_up'], lp['a_a0'], lp['a_k_a'], lp['a_r_k'], lp['a_gn_g'], lp['a_gn_b'],
                    lp['a_g_up'])
    return out, states[0], states[1]


def diff_lambda(lv, layer):
    lam_init = 0.8 - 0.6 * math.exp(-0.3 * layer)
    lam = jnp.exp(jnp.sum(lv[0] * lv[1])) - jnp.exp(jnp.sum(lv[2] * lv[3])) + lam_init
    return lam, lam_init


def adaln_params(cond, lp):
    mod = mm(jax.nn.silu(cond), lp['w_mod']) + lp['b_mod']
    return jnp.split(mod[:, None, :], 6, axis=-1)


def mixer_inputs(x, shift, scale, lp):
    z = ln_mod_matmul(x, scale, shift, lp['w_in'])
    return z, centred_conv3(z, lp['a_conv'])


def merge_mixers(x, yA, yB, yC, z, gate1, shift2, scale2, lp):
    n = x.shape[0] * x.shape[1]
    flat = lambda t: t.reshape(n, t.shape[-1])
    merged = merge_gated(flat(yA), flat(yB), flat(yC), flat(z), lp['p_a'], lp['p_b'], lp['p_c'])
    return out_proj_norm(merged, x, gate1, scale2, shift2, lp['ln1_g'], lp['ln1_b'], lp['w_out'])


def peer_and_norm(part, lp):
    x, ht, gate2 = part
    yt = peer_ffn_tokens(ht, lp['peer_wqt'], lp['peer_keys'], lp['peer_u'], lp['peer_vt'])
    return peer_residual_norm(yt, x, gate2, lp['ln2_g'], lp['ln2_b'])


def context_layer(x, cond, lp, layer):
    Bsz, L, _ = x.shape
    shift1, scale1, gate1, shift2, scale2, gate2 = adaln_params(cond, lp)
    z, zA = mixer_inputs(x, shift1, scale1, lp)
    S0 = jnp.zeros((Bsz, A_HEADS, A_HEAD_DIM, A_HEAD_DIM), jnp.float32)
    yA, S_f, S_b = rwkv7_mixer(zA, S0, S0, lp)
    yB = gqa_mixer(z, lp['b_sink'])
    lam, lam_init = diff_lambda(lp['c_lam'], layer)
    yC = diff_attention_mixer(z, lam, lam_init, lp['c_subln'])
    x_mid, h_mid = merge_mixers(x, yA, yB, yC, z, gate1, shift2, scale2, lp)

    def cache_layout(first_block, n_heads):
        t = z[..., first_block * LANE:(first_block + n_heads) * LANE].reshape(Bsz, L, n_heads, LANE)
        return jnp.transpose(t, (0, 2, 1, 3))
    ctx = (S_f, S_b, cache_layout(ZB_K, B_KV_HEADS), cache_layout(ZB_V, B_KV_HEADS),
           cache_layout(ZC_K, C_HEADS), cache_layout(ZC_V, C_HEADS))
    return (x_mid, h_mid, gate2), ctx


def latent_layer(x, cond, lp, layer, S_f0, S_b0, cache_b, cache_c, rope_b, rope_c):
    shift1, scale1, gate1, shift2, scale2, gate2 = adaln_params(cond, lp)
    z, zA = mixer_inputs(x, shift1, scale1, lp)
    yA, _, _ = rwkv7_mixer(zA, S_f0, S_b0, lp)
    yB = gqa_mixer(z, lp['b_sink'], cache=cache_b, layer=layer, rope=rope_b)
    lam, lam_init = diff_lambda(lp['c_lam'], layer)
    yC = diff_attention_mixer(z, lam, lam_init, lp['c_subln'], cache=cache_c, layer=layer, rope=rope_c)
    x_mid, h_mid = merge_mixers(x, yA, yB, yC, z, gate1, shift2, scale2, lp)
    return x_mid, h_mid, gate2


_BF16_WEIGHTS = ('w_mod', 'w_in', 'a_w_up', 'a_a_up', 'a_g_up', 'p_a', 'p_b', 'p_c', 'w_out', 'peer_u')


def kernel(x_prompt, x_sample, state_a_fwd, state_a_bwd, cache_b_k, cache_b_v, cache_c_k, cache_c_v, c, c_ctx, w_mod, b_mod, w_in, a_conv, a_w0, a_w_up, a_a0, a_a_up, a_g_up, a_k_k, a_k_a, a_r_k, a_gn_g, a_gn_b, b_sink, c_lam, c_subln, p_a, p_b, p_c, w_out, ln1_g, ln1_b, ln2_g, ln2_b, peer_wq, peer_keys, peer_u, peer_v):
    params = dict(w_mod=w_mod, b_mod=b_mod, w_in=w_in, a_conv=a_conv, a_w0=a_w0, a_w_up=a_w_up, a_a0=a_a0,
                  a_a_up=a_a_up, a_g_up=a_g_up, a_k_k=a_k_k, a_k_a=a_k_a, a_r_k=a_r_k, a_gn_g=a_gn_g,
                  a_gn_b=a_gn_b, b_sink=b_sink, c_lam=c_lam, c_subln=c_subln, p_a=p_a, p_b=p_b, p_c=p_c,
                  w_out=w_out, ln1_g=ln1_g, ln1_b=ln1_b, ln2_g=ln2_g, ln2_b=ln2_b, peer_wq=peer_wq,
                  peer_keys=peer_keys, peer_u=peer_u, peer_v=peer_v)
    for name in _BF16_WEIGHTS:
        params[name] = params[name].astype(jnp.bfloat16)
    w_in_bf = params['w_in']
    params['w_in'] = jnp.concatenate([w_in_bf[..., IN_COLS - 3 * D_MODEL:], w_in_bf[..., :IN_COLS - 3 * D_MODEL]], axis=-1)
    params['peer_wqt'] = jnp.swapaxes(params.pop('peer_wq'), 1, 2).astype(jnp.bfloat16)
    params['peer_vt'] = jnp.swapaxes(params.pop('peer_v'), 1, 2).astype(jnp.bfloat16)
    n_lat = x_sample.shape[1]
    rope_b = rope_lane_tables(n_lat, B_HEAD_DIM)
    rope_c = rope_lane_tables(n_lat, C_HEAD_DIM)
    cond_ctx = jnp.broadcast_to(c_ctx[None, :], (x_prompt.shape[0], D_MODEL))
    y_prompt, y_sample = x_prompt, x_sample
    new = [[], [], [], [], [], []]
    for layer in range(DEPTH):
        lp = {name: val[layer] for name, val in params.items()}
        part_ctx, ctx = context_layer(y_prompt, cond_ctx, lp, layer)
        for lst, t in zip(new, ctx):
            lst.append(t)
        part_lat = latent_layer(y_sample, c, lp, layer,
                                state_a_fwd[:, layer], state_a_bwd[:, layer],
                                (cache_b_k, cache_b_v), (cache_c_k, cache_c_v), rope_b, rope_c)
        y_prompt = peer_and_norm(part_ctx, lp)
        y_sample = peer_and_norm(part_lat, lp)
    return (y_prompt, y_sample) + tuple(jnp.stack(lst, axis=1) for lst in new)
```

```python
import math
from functools import partial

import jax
import jax.numpy as jnp
import numpy as np
from jax import lax
from jax.experimental import pallas as pl
from jax.experimental.pallas import tpu as pltpu

D_MODEL = 2048
DEPTH = 4
GRID_W = 64
BLOCK = 128
A_HEADS = 16
A_HEAD_DIM = 64
A_WIDTH = A_HEADS * A_HEAD_DIM
W_LORA = 64
ICL_LORA = 64
G_LORA = 128
A_COLS = 3 * A_WIDTH + G_LORA + 2 * W_LORA + 2 * ICL_LORA
A_GN_EPS = 64e-5
B_HEADS = 4
B_KV_HEADS = 2
B_HEAD_DIM = 128
B_WIDTH = B_HEADS * B_HEAD_DIM
B_KV_WIDTH = B_KV_HEADS * B_HEAD_DIM
WINDOW = 128
C_HEADS = 4
C_HEAD_DIM = 64
C_WIDTH = C_HEADS * 2 * C_HEAD_DIM
IN_COLS = A_COLS + B_WIDTH + 2 * B_KV_WIDTH + 3 * C_WIDTH + 3 * D_MODEL
PEER_HEADS = 8
N_KEYS = 128
PEER_QDIM = 256
PEER_TOPK = 16
ROPE_BASE = 10000.0
LN_EPS = 1e-5
NEG_INF = -1e30
DEEPNORM_ALPHA = (2 * DEPTH) ** 0.25

LANE = 128
VMEM_LIMIT_BYTES = 56 * 1024 * 1024


def _mm_kernel(x_ref, w_ref, o_ref):
    o_ref[...] = jnp.dot(x_ref[...].astype(jnp.bfloat16), w_ref[...],
                         preferred_element_type=jnp.float32)


def _pick_tile(n, cap, unit):
    if n <= cap:
        return n
    best = None
    for t in range(unit, cap + 1, unit):
        if n % t == 0:
            best = t
    assert best is not None, (n, cap, unit)
    return best


def mm(x, w):
    M, K = x.shape
    N = w.shape[1]
    tm = _pick_tile(M, 512, 8)
    tn = _pick_tile(N, 1024, LANE)
    return pl.pallas_call(
        _mm_kernel,
        grid=(N // tn, M // tm),
        in_specs=[pl.BlockSpec((tm, K), lambda j, i: (i, 0)),
                  pl.BlockSpec((K, tn), lambda j, i: (0, j))],
        out_specs=pl.BlockSpec((tm, tn), lambda j, i: (i, j)),
        out_shape=jax.ShapeDtypeStruct((M, N), jnp.float32),
        compiler_params=pltpu.CompilerParams(
            dimension_semantics=("arbitrary", "arbitrary"),
            vmem_limit_bytes=VMEM_LIMIT_BYTES),
        name="mm",
    )(x, w)


def mm3(x, w):
    B, T, K = x.shape
    return mm(x.reshape(B * T, K), w).reshape(B, T, w.shape[1])


ROW_TILE = 1024
NORM_ROW_TILE = 512
Z_GATES = 0
Z_A = 3 * D_MODEL
Z_B = Z_A + A_COLS
Z_C = Z_B + B_WIDTH + 2 * B_KV_WIDTH


def _row_blocks(n_batch, n_time, rows):
    tt = min(n_time, rows)
    bt = rows // tt
    assert n_time % tt == 0 and n_batch % bt == 0, (n_batch, n_time, rows)
    return bt, tt


def _layer_norm_rows(x):
    mu = jnp.mean(x, axis=-1, keepdims=True)
    xc = x - mu
    return xc * lax.rsqrt(jnp.mean(xc * xc, axis=-1, keepdims=True) + LN_EPS)


def _ln_mod_mm_kernel(x_ref, scale_ref, shift_ref, w_ref, o_ref, h_ref):
    bt, tt, K = x_ref.shape

    @pl.when(pl.program_id(2) == 0)
    def _():
        h = _layer_norm_rows(x_ref[...]) * (1.0 + scale_ref[...]) + shift_ref[...]
        h_ref[...] = h.reshape(bt * tt, K).astype(jnp.bfloat16)

    o = jnp.dot(h_ref[...], w_ref[...], preferred_element_type=jnp.float32)
    o_ref[...] = o.reshape(bt, tt, o.shape[-1])


def ln_mod_matmul(x, scale, shift, w):
    B, T, K = x.shape
    N = w.shape[1]
    bt, tt = _row_blocks(B, T, ROW_TILE)
    tn = _pick_tile(N, 1024, LANE)
    vec = pl.BlockSpec((bt, 1, K), lambda b, t, j: (b, 0, 0))
    return pl.pallas_call(
        _ln_mod_mm_kernel,
        grid=(B // bt, T // tt, N // tn),
        in_specs=[pl.BlockSpec((bt, tt, K), lambda b, t, j: (b, t, 0)), vec, vec,
                  pl.BlockSpec((K, tn), lambda b, t, j: (0, j))],
        out_specs=pl.BlockSpec((bt, tt, tn), lambda b, t, j: (b, t, j)),
        out_shape=jax.ShapeDtypeStruct((B, T, N), jnp.float32),
        scratch_shapes=[pltpu.VMEM((bt * tt, K), jnp.bfloat16)],
        compiler_params=pltpu.CompilerParams(
            dimension_semantics=("arbitrary", "arbitrary", "arbitrary"), vmem_limit_bytes=VMEM_LIMIT_BYTES),
        name="ln_mod_matmul",
    )(x, scale, shift, w)


def _merge_kernel(ya_ref, yb_ref, yc_ref, ga_ref, gb_ref, gc_ref, pa_ref, pb_ref, pc_ref, o_ref,
                  ya_bf, yb_bf, yc_bf):
    @pl.when(pl.program_id(1) == 0)
    def _():
        ya_bf[...] = ya_ref[...].astype(jnp.bfloat16)
        yb_bf[...] = yb_ref[...].astype(jnp.bfloat16)
        yc_bf[...] = yc_ref[...].astype(jnp.bfloat16)

    f = lambda y, p: jnp.dot(y[...], p[...], preferred_element_type=jnp.float32)
    o = (jax.nn.sigmoid(ga_ref[...]) * f(ya_bf, pa_ref) + jax.nn.sigmoid(gb_ref[...]) * f(yb_bf, pb_ref)
         + jax.nn.sigmoid(gc_ref[...]) * f(yc_bf, pc_ref))
    o_ref[...] = o.astype(jnp.bfloat16)


def merge_gated(yA, yB, yC, z, p_a, p_b, p_c):
    N = yA.shape[0]
    D = p_a.shape[1]
    tm = _pick_tile(N, ROW_TILE, 8)
    tn = 512
    assert Z_GATES == 0 and D % tn == 0
    ysp = lambda y: pl.BlockSpec((tm, y.shape[1]), lambda i, j: (i, 0))
    gsp = lambda k: pl.BlockSpec((tm, tn), lambda i, j: (i, k * (D // tn) + j))
    psp = lambda p: pl.BlockSpec((p.shape[0], tn), lambda i, j: (0, j))
    return pl.pallas_call(
        _merge_kernel,
        grid=(N // tm, D // tn),
        in_specs=[ysp(yA), ysp(yB), ysp(yC), gsp(0), gsp(1), gsp(2), psp(p_a), psp(p_b), psp(p_c)],
        out_specs=pl.BlockSpec((tm, tn), lambda i, j: (i, j)),
        out_shape=jax.ShapeDtypeStruct((N, D), jnp.bfloat16),
        scratch_shapes=[pltpu.VMEM((tm, y.shape[1]), jnp.bfloat16) for y in (yA, yB, yC)],
        compiler_params=pltpu.CompilerParams(
            dimension_semantics=("arbitrary", "arbitrary"), vmem_limit_bytes=VMEM_LIMIT_BYTES),
        name="merge_gated",
    )(yA, yB, yC, z, z, z, p_a, p_b, p_c)


def _out_norm_kernel(m_ref, x_ref, gate_ref, scale_ref, shift_ref, lng_ref, lnb_ref, w_ref, x1_ref, ht_ref):
    bt, tt, D = x_ref.shape
    y = jnp.dot(m_ref[...].reshape(bt * tt, D), w_ref[...], preferred_element_type=jnp.float32)
    x1 = _layer_norm_rows(DEEPNORM_ALPHA * x_ref[...] + gate_ref[...] * y.reshape(bt, tt, D))
    x1 = x1 * lng_ref[...] + lnb_ref[...]
    x1_ref[...] = x1
    h = _layer_norm_rows(x1) * (1.0 + scale_ref[...]) + shift_ref[...]
    ht_ref[...] = jnp.transpose(h.reshape(bt * tt, D)).astype(jnp.bfloat16)


def out_proj_norm(merged, x, gate1, scale2, shift2, ln_g, ln_b, w_out):
    B, T, D = x.shape
    bt, tt = _row_blocks(B, T, NORM_ROW_TILE)
    rows = pl.BlockSpec((bt, tt, D), lambda b, t: (b, t, 0))
    vec = pl.BlockSpec((bt, 1, D), lambda b, t: (b, 0, 0))
    par = pl.BlockSpec((1, 1, D), lambda b, t: (0, 0, 0))
    return pl.pallas_call(
        _out_norm_kernel,
        grid=(B // bt, T // tt),
        in_specs=[rows, rows, vec, vec, vec, par, par, pl.BlockSpec((D, D), lambda b, t: (0, 0))],
        out_specs=[rows, pl.BlockSpec((D, bt * tt), lambda b, t: (0, b * (T // tt) + t))],
        out_shape=[jax.ShapeDtypeStruct((B, T, D), jnp.float32), jax.ShapeDtypeStruct((D, B * T), jnp.bfloat16)],
        compiler_params=pltpu.CompilerParams(
            dimension_semantics=("arbitrary", "arbitrary"), vmem_limit_bytes=VMEM_LIMIT_BYTES),
        name="out_proj_norm",
    )(merged.reshape(B, T, D), x, gate1, scale2, shift2, ln_g.reshape(1, 1, D), ln_b.reshape(1, 1, D), w_out)


def _peer_norm_kernel(yt_ref, x_ref, gate_ref, lng_ref, lnb_ref, o_ref):
    bt, tt, D = x_ref.shape
    y = jnp.transpose(yt_ref[...]).reshape(bt, tt, D)
    x = _layer_norm_rows(DEEPNORM_ALPHA * x_ref[...] + gate_ref[...] * y)
    o_ref[...] = x * lng_ref[...] + lnb_ref[...]


def peer_residual_norm(yt, x, gate2, ln_g, ln_b):
    B, T, D = x.shape
    bt, tt = _row_blocks(B, T, NORM_ROW_TILE)
    rows = pl.BlockSpec((bt, tt, D), lambda b, t: (b, t, 0))
    par = pl.BlockSpec((1, 1, D), lambda b, t: (0, 0, 0))
    return pl.pallas_call(
        _peer_norm_kernel,
        grid=(B // bt, T // tt),
        in_specs=[pl.BlockSpec((D, bt * tt), lambda b, t: (0, b * (T // tt) + t)), rows,
                  pl.BlockSpec((bt, 1, D), lambda b, t: (b, 0, 0)), par, par],
        out_specs=rows,
        out_shape=jax.ShapeDtypeStruct((B, T, D), jnp.float32),
        compiler_params=pltpu.CompilerParams(
            dimension_semantics=("arbitrary", "arbitrary"), vmem_limit_bytes=VMEM_LIMIT_BYTES),
        name="peer_residual_norm",
    )(yt, x, gate2, ln_g.reshape(1, 1, D), ln_b.reshape(1, 1, D))


RWKV_CHUNK = 64
RWKV_HEADS_PER_STEP = 16

_NN = (((2,), (1,)), ((0,), (0,)))
_NT = (((2,), (2,)), ((0,), (0,)))
_TN = (((1,), (1,)), ((0,), (0,)))


def _split2(x):
    hi = x.astype(jnp.bfloat16)
    lo = (x - hi.astype(jnp.float32)).astype(jnp.bfloat16)
    return hi, lo


def _dot3(a, b, dims=_NN):
    f = lambda x, y: lax.dot_general(x, y, dims, preferred_element_type=jnp.float32)
    return f(a[0], b[0]) + (f(a[0], b[1]) + f(a[1], b[0]))


def _dot1(a, b, dims=_NN):
    return lax.dot_general(a[0], b[0], dims, preferred_element_type=jnp.float32)


def _split_heads(x):
    n = A_HEAD_DIM
    return jnp.stack([x[:, h * n:(h + 1) * n] for h in range(x.shape[1] // n)], axis=0)


def _lora_half(x, d, w_ref):
    lane = lax.broadcasted_iota(jnp.int32, x.shape, 1)
    xd = jnp.where(lane // (LANE // 2) == d, x, 0.0).astype(jnp.bfloat16)
    return jnp.dot(xd, w_ref[...], preferred_element_type=jnp.float32)


def _rwkv_chunk_kernel(zr_ref, zk_ref, zv_ref, wd_ref, ad_ref, wup_ref, aup_ref, w0_ref, a0_ref, kk_ref, ka_ref,
                       s0_ref, y_ref, sfin_ref, st_ref):
    d = pl.program_id(0)
    c = pl.program_id(3)
    n_chunks = pl.num_programs(3)
    HB = st_ref.shape[0]
    C = zr_ref.shape[0]

    @pl.when(c == 0)
    def _():
        st_ref[...] = s0_ref[...]

    k_all = zk_ref[...]
    lw_all = -math.exp(-0.5) * jax.nn.sigmoid(_lora_half(jnp.tanh(wd_ref[...]), d, wup_ref) + w0_ref[...])
    icl_all = jax.nn.sigmoid(_lora_half(ad_ref[...], d, aup_ref) + a0_ref[...])
    r, v = _split_heads(zr_ref[...]), _split_heads(zv_ref[...])
    lw = _split_heads(lw_all)
    k = _split_heads(k_all * (1.0 + (icl_all - 1.0) * ka_ref[...]))
    kk = _split_heads(k_all * kk_ref[...])
    kk = kk * lax.rsqrt(jnp.sum(kk * kk, axis=-1, keepdims=True) + 1e-12)
    a = -kk
    b = kk * _split_heads(icl_all)

    ti = lax.broadcasted_iota(jnp.int32, (HB, C, C), 1)
    si = lax.broadcasted_iota(jnp.int32, (HB, C, C), 2)
    diff = (si - ti) * (1 - 2 * d)
    m_incl = diff <= 0
    m_strict = diff < 0
    m_incl_bf = jnp.where(m_incl, 1.0, 0.0).astype(jnp.bfloat16)
    eye = jnp.where(diff == 0, 1.0, 0.0).astype(jnp.float32)

    st = st_ref[...]

    lw_hi, lw_lo = _split2(lw)
    lw_lo2 = (lw - lw_hi.astype(jnp.float32) - lw_lo.astype(jnp.float32)).astype(jnp.bfloat16)
    f = lambda y: lax.dot_general(m_incl_bf, y, _NN, preferred_element_type=jnp.float32)
    cum = f(lw_hi) + (f(lw_lo) + f(lw_lo2))
    total = jnp.sum(lw, axis=1, keepdims=True)
    e_cum = jnp.exp(cum)
    e_inv = jnp.exp(-cum)
    e_tot = jnp.exp(total)
    hi = lambda x: (x.astype(jnp.bfloat16),)
    at_f = a * jnp.exp(cum - lw)
    rt_f = r * e_cum
    at, rt = hi(at_f), hi(rt_f)
    bt_f = b * e_inv
    kt_f = k * e_inv
    bh, kh = _split2(bt_f * e_tot), _split2(kt_f * e_tot)
    vs = _split2(v)

    scores = _dot1(hi(jnp.concatenate([at_f, rt_f], axis=1)),
                   hi(jnp.concatenate([bt_f, kt_f], axis=1)), _NT)
    l_ab = jnp.where(m_strict, scores[:, :C, :C], 0.0)
    l_ak = jnp.where(m_strict, scores[:, :C, C:], 0.0)
    m_rb = jnp.where(m_incl, scores[:, C:, :C], 0.0)
    m_rk = jnp.where(m_incl, scores[:, C:, C:], 0.0)

    tm = eye + l_ab
    lp = hi(l_ab)
    for _ in range(int(math.log2(C)) - 1):
        lp = hi(_dot1(lp, lp))
        tm = tm + _dot1(hi(tm), lp)
    tms = hi(tm)
    p = _split2(_dot1(tms, at))
    q = _dot1(tms, hi(_dot1(hi(l_ak), vs)))

    sts = _split2(st)
    u = _split2(_dot3(p, sts) + q)
    y = _dot1(rt, sts) + _dot1(hi(m_rb), u) + _dot1(hi(m_rk), vs)
    y_ref[...] = jnp.concatenate([y[h] for h in range(HB)], axis=-1)
    st_ref[...] = jnp.swapaxes(e_tot, 1, 2) * st + _dot3(bh, u, _TN) + _dot3(kh, vs, _TN)

    @pl.when(c == n_chunks - 1)
    def _():
        sfin_ref[...] = st_ref[...]


def rwkv7_chunked(zc, w_up, a_up, w0, a0, k_k, k_a, s0t):
    B, T, _ = zc.shape
    D = 2
    N, H, W = A_HEAD_DIM, A_HEADS, A_WIDTH
    C, HB = RWKV_CHUNK, RWKV_HEADS_PER_STEP
    G = H // HB
    lanes = HB * N
    assert T % C == 0 and H % HB == 0 and 2 * W_LORA == LANE and 2 * ICL_LORA == LANE
    n_chunks = T // C
    lora_block = (3 * W + G_LORA) // LANE
    chunk = lambda d, c: c + d * (n_chunks - 1 - 2 * c)
    zcol = lambda base: pl.BlockSpec((None, C, lanes), lambda d, bb, g, c: (bb, chunk(d, c), base * G + g))
    lora = lambda off: pl.BlockSpec((None, C, LANE), lambda d, bb, g, c: (bb, chunk(d, c), lora_block + off))
    up = pl.BlockSpec((LANE, lanes), lambda d, bb, g, c: (0, g))
    per_dir = pl.BlockSpec((None, None, C, lanes), lambda d, bb, g, c: (d, bb, chunk(d, c), g))
    dir_row = pl.BlockSpec((None, 1, lanes), lambda d, bb, g, c: (d, 0, g))
    row = pl.BlockSpec((1, lanes), lambda d, bb, g, c: (0, g))
    state = pl.BlockSpec((None, None, HB, N, N), lambda d, bb, g, c: (d, bb, g, 0, 0))
    return pl.pallas_call(
        _rwkv_chunk_kernel,
        grid=(D, B, G, n_chunks),
        in_specs=[zcol(0), zcol(1), zcol(2), lora(0), lora(1), up, up, dir_row, dir_row, row, row, state],
        out_specs=[per_dir, state],
        out_shape=[jax.ShapeDtypeStruct((D, B, T, W), jnp.float32),
                   jax.ShapeDtypeStruct((D, B, H, N, N), jnp.float32)],
        scratch_shapes=[pltpu.VMEM((HB, N, N), jnp.float32)],
        compiler_params=pltpu.CompilerParams(
            dimension_semantics=("arbitrary", "arbitrary", "arbitrary", "arbitrary"),
            vmem_limit_bytes=VMEM_LIMIT_BYTES),
        name="rwkv7_chunked",
    )(zc, zc, zc, zc, zc, w_up.reshape(LANE, W), a_up.reshape(LANE, W), w0.reshape(D, 1, W), a0.reshape(D, 1, W),
      k_k.reshape(1, W), k_a.reshape(1, W), s0t)


CONV_COLS = 384
CONV_ROWS = 2048
SUBLANE = 8


def _conv3_kernel(x_ref, prev_ref, next_ref, w_ref, o_ref):
    t = pl.program_id(1)
    x = x_ref[...]
    tt = x.shape[0]
    row = lax.broadcasted_iota(jnp.int32, x.shape, 0)
    before = jnp.where(t > 0, prev_ref[SUBLANE - 1:SUBLANE, :], 0.0)
    after = jnp.where(t < pl.num_programs(1) - 1, next_ref[0:1, :], 0.0)
    x_prev = jnp.where(row == 0, before, pltpu.roll(x, 1, axis=0))
    x_next = jnp.where(row == tt - 1, after, pltpu.roll(x, tt - 1, axis=0))
    o_ref[...] = x_prev * w_ref[0:1, :] + x * w_ref[1:2, :] + x_next * w_ref[2:3, :]


def centred_conv3(z, w):
    B, T, _ = z.shape
    tt = min(T, CONV_ROWS)
    cb = CONV_COLS
    assert Z_A % cb == 0 and A_COLS % cb == 0 and T % tt == 0
    base = Z_A // cb
    groups = tt // SUBLANE
    last = T // SUBLANE - 1
    return pl.pallas_call(
        _conv3_kernel,
        grid=(B, T // tt, A_COLS // cb),
        in_specs=[pl.BlockSpec((None, tt, cb), lambda b, t, j: (b, t, base + j)),
                  pl.BlockSpec((None, SUBLANE, cb), lambda b, t, j: (b, jnp.maximum(t * groups - 1, 0), base + j)),
                  pl.BlockSpec((None, SUBLANE, cb), lambda b, t, j: (b, jnp.minimum((t + 1) * groups, last), base + j)),
                  pl.BlockSpec((3, cb), lambda b, t, j: (0, j))],
        out_specs=pl.BlockSpec((None, tt, cb), lambda b, t, j: (b, t, j)),
        out_shape=jax.ShapeDtypeStruct((B, T, A_COLS), jnp.float32),
        compiler_params=pltpu.CompilerParams(
            dimension_semantics=("arbitrary", "arbitrary", "arbitrary"), vmem_limit_bytes=VMEM_LIMIT_BYTES),
        name="conv3",
    )(z, z, z, w)


def _rwkv_post_kernel(yf_ref, yb_ref, r_ref, k_ref, v_ref, gd_ref, ad_ref, aup_ref, a0f_ref, a0b_ref,
                      ka_ref, rk_ref, gng_ref, gnb_ref, gup_ref, o_ref, ones_ref):
    W = o_ref.shape[-1]
    rows = o_ref.shape[0] * o_ref.shape[1]

    @pl.when((pl.program_id(0) == 0) & (pl.program_id(1) == 0))
    def _():
        hr = lax.broadcasted_iota(jnp.int32, (W, W), 0) // A_HEAD_DIM
        hc = lax.broadcasted_iota(jnp.int32, (W, W), 1) // A_HEAD_DIM
        ones_ref[...] = jnp.where(hr == hc, 1.0, 0.0).astype(jnp.bfloat16)

    def head_sum(x):
        hi, lo = _split2(x)
        f = lambda p: jnp.dot(p, ones_ref[...], preferred_element_type=jnp.float32)
        return f(hi) + f(lo)

    flat = lambda ref: ref[...].reshape(rows, ref.shape[-1])
    inv_n = 1.0 / A_HEAD_DIM
    y = flat(yf_ref) + flat(yb_ref)
    yc = y - head_sum(y) * inv_n
    var = head_sum(yc * yc) * inv_n
    yn = yc * lax.rsqrt(var + A_GN_EPS) * gng_ref[...] + gnb_ref[...]
    ad = flat(ad_ref)
    icl_sum = (jax.nn.sigmoid(_lora_half(ad, 0, aup_ref) + a0f_ref[...])
               + jax.nn.sigmoid(_lora_half(ad, 1, aup_ref) + a0b_ref[...]))
    kd_sum = flat(k_ref) * (2.0 + (icl_sum - 2.0) * ka_ref[...])
    bonus = head_sum(flat(r_ref) * kd_sum * rk_ref[...]) * flat(v_ref)
    g = jnp.dot(jax.nn.sigmoid(flat(gd_ref)).astype(jnp.bfloat16), gup_ref[...], preferred_element_type=jnp.float32)
    o_ref[...] = ((yn + bonus) * g).reshape(o_ref.shape)


def rwkv_post(ys, za, a_up, a0, k_a, r_k, gn_g, gn_b, g_up):
    _, B, T, W = ys.shape
    bt, tt = _row_blocks(B, T, NORM_ROW_TILE)
    both = lambda d: pl.BlockSpec((None, bt, tt, W), lambda b, t: (d, b, t, 0))
    zcol = lambda j: pl.BlockSpec((bt, tt, W), lambda b, t: (b, t, j))
    dvec = lambda d: pl.BlockSpec((None, 1, W), lambda b, t: (d, 0, 0))
    vec = pl.BlockSpec((1, W), lambda b, t: (0, 0))
    assert (3 * W) % G_LORA == 0
    return pl.pallas_call(
        _rwkv_post_kernel,
        grid=(B // bt, T // tt),
        in_specs=[both(0), both(1), zcol(0), zcol(1), zcol(2),
                  pl.BlockSpec((bt, tt, G_LORA), lambda b, t: (b, t, 3 * W // G_LORA)),
                  pl.BlockSpec((bt, tt, LANE), lambda b, t: (b, t, (3 * W + G_LORA) // LANE + 1)),
                  pl.BlockSpec((LANE, W), lambda b, t: (0, 0)), dvec(0), dvec(1), vec, vec, vec, vec,
                  pl.BlockSpec((G_LORA, W), lambda b, t: (0, 0))],
        out_specs=pl.BlockSpec((bt, tt, W), lambda b, t: (b, t, 0)),
        out_shape=jax.ShapeDtypeStruct((B, T, W), jnp.float32),
        scratch_shapes=[pltpu.VMEM((W, W), jnp.bfloat16)],
        compiler_params=pltpu.CompilerParams(
            dimension_semantics=("arbitrary", "arbitrary"), vmem_limit_bytes=VMEM_LIMIT_BYTES),
        name="rwkv_post",
    )(ys, ys, za, za, za, za, za, a_up.reshape(LANE, W), a0.reshape(2, 1, W), a0.reshape(2, 1, W),
      k_a.reshape(1, W), r_k.reshape(1, W), gn_g.reshape(1, W), gn_b.reshape(1, W), g_up)


PEER_ROUTE_TOKENS = 256
PEER_TOKENS = 512
PEER_E1_PER_STEP = 8
PEER_E1_PER_PART = 2
PEER_CAND_ROWS = 56


def _top_rows(s_ref, n_rows, k, emit):
    for a in range(k):
        s = s_ref[0:n_rows, :]
        mx = jnp.max(s, axis=0, keepdims=True)
        emit(a, mx)
        if a + 1 < k:
            s_ref[0:n_rows, :] = jnp.where(s == mx, NEG_INF, s)


def _peer_route_kernel(ht_ref, wqt_ref, k1_ref, k2_ref, s1_ref, s2_ref, g1_ref, e2_ref, th_ref,
                       work_ref, t1_ref, t2_ref, cand_ref):
    half = PEER_QDIM // 2
    qt = jnp.dot(wqt_ref[...], ht_ref[...], preferred_element_type=jnp.float32)
    for h in range(PEER_HEADS):
        for which, (k_ref, s_out, t_ref) in enumerate(((k1_ref, s1_ref, t1_ref), (k2_ref, s2_ref, t2_ref))):
            q = qt[h * PEER_QDIM + which * half: h * PEER_QDIM + (which + 1) * half, :]
            q_hi, q_lo = _split2(q)
            k_hi, k_lo = _split2(k_ref[h])
            f = lambda x, y: jnp.dot(x, y, preferred_element_type=jnp.float32)
            s = f(k_hi, q_hi) + (f(k_hi, q_lo) + f(k_lo, q_hi))
            s_out[h] = s
            work_ref[...] = s

            def emit(a, mx, t_ref=t_ref):
                t_ref[a:a + 1, :] = mx
            _top_rows(work_ref, N_KEYS, PEER_TOPK, emit)
        t1 = t1_ref[...]
        t2 = t2_ref[...]
        m1 = t1[0:1, :]
        m2 = t2[0:1, :]
        row = 0
        for a in range(PEER_TOPK):
            nb = PEER_TOPK // (a + 1)
            cand_ref[row:row + nb, :] = t1[a:a + 1, :] + t2[0:nb, :]
            row += nb
        cand_ref[row:PEER_CAND_ROWS, :] = jnp.full((PEER_CAND_ROWS - row, t1.shape[1]), NEG_INF, jnp.float32)
        acc = {}

        def emit_c(a, mx):
            e = jnp.exp(mx - (m1 + m2))
            acc['z'] = e if a == 0 else acc['z'] + e
            acc['th'] = mx
        _top_rows(cand_ref, PEER_CAND_ROWS, PEER_TOPK, emit_c)
        th_ref[h:h + 1, :] = acc['th']
        g1_ref[h] = jnp.exp(s1_ref[h] - m1) / acc['z']
        e2_ref[h] = jnp.exp(s2_ref[h] - m2)


def peer_route(ht, wqt, keys):
    D, N = ht.shape
    tm = PEER_ROUTE_TOKENS
    assert N % tm == 0
    big = jax.ShapeDtypeStruct((PEER_HEADS, N_KEYS, N), jnp.float32)
    big_spec = pl.BlockSpec((PEER_HEADS, N_KEYS, tm), lambda i: (0, 0, i))
    key_spec = pl.BlockSpec((PEER_HEADS, N_KEYS, PEER_QDIM // 2), lambda i: (0, 0, 0))
    return pl.pallas_call(
        _peer_route_kernel,
        grid=(N // tm,),
        in_specs=[pl.BlockSpec((D, tm), lambda i: (0, i)),
                  pl.BlockSpec((PEER_HEADS * PEER_QDIM, D), lambda i: (0, 0)),
                  key_spec, key_spec],
        out_specs=[big_spec, big_spec, big_spec, big_spec, pl.BlockSpec((PEER_HEADS, tm), lambda i: (0, i))],
        out_shape=[big, big, big, big, jax.ShapeDtypeStruct((PEER_HEADS, N), jnp.float32)],
        scratch_shapes=[pltpu.VMEM((N_KEYS, tm), jnp.float32),
                        pltpu.VMEM((PEER_TOPK, tm), jnp.float32),
                        pltpu.VMEM((PEER_TOPK, tm), jnp.float32),
                        pltpu.VMEM((PEER_CAND_ROWS, tm), jnp.float32)],
        compiler_params=pltpu.CompilerParams(
            dimension_semantics=("arbitrary",), vmem_limit_bytes=VMEM_LIMIT_BYTES),
        name="peer_route",
    )(ht, wqt, keys[0], keys[1])


def _gelu_tanh(x):
    return 0.5 * x * (1.0 + jnp.tanh(math.sqrt(2.0 / math.pi) * (x + 0.044715 * (x * x * x))))


def _peer_expert_kernel(ht_ref, u_ref, vt_ref, s1_ref, s2_ref, g1_ref, e2_ref, th_ref, o_ref, wg_ref):
    j = pl.program_id(1)
    tm = ht_ref.shape[1]

    @pl.when(j == 0)
    def _():
        o_ref[...] = jnp.zeros_like(o_ref)

    e1_rows = pl.ds(pl.multiple_of(j * PEER_E1_PER_STEP, PEER_E1_PER_STEP), PEER_E1_PER_STEP)
    ht = ht_ref[...]
    n_parts = PEER_E1_PER_STEP // PEER_E1_PER_PART
    part_slice = lambda p: slice(p * PEER_E1_PER_PART * N_KEYS, (p + 1) * PEER_E1_PER_PART * N_KEYS)
    activation = lambda p: _gelu_tanh(jnp.dot(u_ref[part_slice(p), :], ht, preferred_element_type=jnp.float32))
    act_next = activation(0)
    for part in range(n_parts):
        act = act_next
        if part + 1 < n_parts:
            act_next = activation(part + 1)
        for tc in range(tm // LANE):
            cols = slice(tc * LANE, (tc + 1) * LANE)
            s1_rows = [s1_ref[h, e1_rows, cols] for h in range(PEER_HEADS)]
            g1_rows = [g1_ref[h, e1_rows, cols] for h in range(PEER_HEADS)]
            for ee in range(PEER_E1_PER_PART):
                e = part * PEER_E1_PER_PART + ee
                w = None
                for h in range(PEER_HEADS):
                    score = s2_ref[h, :, cols] + s1_rows[h][e:e + 1, :]
                    gate = e2_ref[h, :, cols] * g1_rows[h][e:e + 1, :]
                    term = jnp.where(score >= th_ref[h:h + 1, cols], gate, 0.0)
                    w = term if w is None else w + term
                rows = slice(ee * N_KEYS, (ee + 1) * N_KEYS)
                wg_ref[part, rows, cols] = (w * act[rows, cols]).astype(jnp.bfloat16)
        if part >= 1:
            o_ref[...] += jnp.dot(vt_ref[:, part_slice(part - 1)], wg_ref[part - 1],
                                  preferred_element_type=jnp.float32)
    o_ref[...] += jnp.dot(vt_ref[:, part_slice(n_parts - 1)], wg_ref[n_parts - 1],
                          preferred_element_type=jnp.float32)


def peer_experts(ht, u, vt, s1, s2, g1, e2, th):
    D, N = ht.shape
    E = u.shape[0]
    tm = PEER_TOKENS
    te = PEER_E1_PER_STEP * N_KEYS
    assert N % tm == 0 and E % te == 0
    big_spec = pl.BlockSpec((PEER_HEADS, N_KEYS, tm), lambda i, j: (0, 0, i))
    return pl.pallas_call(
        _peer_expert_kernel,
        grid=(N // tm, E // te),
        in_specs=[pl.BlockSpec((D, tm), lambda i, j: (0, i)),
                  pl.BlockSpec((te, D), lambda i, j: (j, 0)),
                  pl.BlockSpec((D, te), lambda i, j: (0, j)),
                  big_spec, big_spec, big_spec, big_spec,
                  pl.BlockSpec((PEER_HEADS, tm), lambda i, j: (0, i))],
        out_specs=pl.BlockSpec((D, tm), lambda i, j: (0, i)),
        out_shape=jax.ShapeDtypeStruct((D, N), jnp.float32),
        scratch_shapes=[pltpu.VMEM((PEER_E1_PER_STEP // PEER_E1_PER_PART, PEER_E1_PER_PART * N_KEYS, tm), jnp.bfloat16)],
        compiler_params=pltpu.CompilerParams(
            dimension_semantics=("arbitrary", "arbitrary"), vmem_limit_bytes=VMEM_LIMIT_BYTES),
        name="peer_experts",
    )(ht, u, vt, s1, s2, g1, e2, th)


def peer_ffn_tokens(ht, wqt, keys, u, vt):
    s1, s2, g1, e2, th = peer_route(ht, wqt, keys)
    return peer_experts(ht, u, vt, s1, s2, g1, e2, th)


ZB_Q = Z_B // LANE
ZB_K = ZB_Q + B_HEADS
ZB_V = ZB_K + B_KV_HEADS
ZC_Q = ZB_V + B_KV_HEADS
ZC_K = ZC_Q + C_HEADS
ZC_V = ZC_K + C_HEADS
DIFF_Q_ROWS = 256

_DOT_NT = (((1,), (1,)), ((), ()))


def rope_lane_tables(n_tokens, head_dim):
    rows = n_tokens // GRID_W
    row_pos = jnp.repeat(jnp.arange(rows, dtype=jnp.float32), GRID_W)
    col_pos = jnp.tile(jnp.arange(GRID_W, dtype=jnp.float32), rows)
    n_freq = head_dim // 4
    freqs = ROPE_BASE ** (-jnp.arange(n_freq, dtype=jnp.float32) / n_freq)
    ang = jnp.concatenate([row_pos[:, None] * freqs, col_pos[:, None] * freqs], axis=-1)
    cos = jnp.repeat(jnp.cos(ang), 2, axis=-1)
    sin = jnp.repeat(jnp.sin(ang), 2, axis=-1) * jnp.tile(jnp.array([-1.0, 1.0], jnp.float32), head_dim // 2)
    reps = LANE // head_dim
    return jnp.tile(cos, (1, reps)), jnp.tile(sin, (1, reps))


def _rope(x, cos, sin_signed):
    lane = lax.broadcasted_iota(jnp.int32, x.shape, 1)
    partner = jnp.where(lane % 2 == 0, pltpu.roll(x, LANE - 1, axis=1), pltpu.roll(x, 1, axis=1))
    return x * cos + partner * sin_signed


def _bf(x):
    return x.astype(jnp.bfloat16)


def _diff_attn_kernel(lam_ref, q_ref, k_ref, v_ref, *rest, latent, lam_init):
    if latent:
        kc_ref, vc_ref, cq_ref, sq_ref, ck_ref, sk_ref, g_ref, o_ref, kr_ref = rest
    else:
        g_ref, o_ref, kr_ref = rest
    i = pl.program_id(2)

    @pl.when(i == 0)
    def _():
        k = k_ref[...]
        kr_ref[...] = _bf(_rope(k, ck_ref[...], sk_ref[...]) if latent else k)

    q = q_ref[...]
    if latent:
        q = _rope(q, cq_ref[...], sq_ref[...])
    lane = lax.broadcasted_iota(jnp.int32, q.shape, 1)
    scale = C_HEAD_DIM ** -0.5
    lam = lam_ref[0]
    kr = kr_ref[...]
    vb = _bf(v_ref[...])
    if latent:
        kcb, vcb = _bf(kc_ref[...]), _bf(vc_ref[...])

    def softmax_parts(qh):
        s = lax.dot_general(qh, kr, _DOT_NT, preferred_element_type=jnp.float32) * scale
        m = jnp.max(s, axis=-1, keepdims=True)
        if latent:
            sc = lax.dot_general(qh, kcb, _DOT_NT, preferred_element_type=jnp.float32) * scale
            m = jnp.maximum(m, jnp.max(sc, axis=-1, keepdims=True))
            ec = jnp.exp(sc - m)
        e = jnp.exp(s - m)
        z = jnp.sum(e, axis=-1, keepdims=True)
        if latent:
            z = z + jnp.sum(ec, axis=-1, keepdims=True)
            return e, ec, 1.0 / z
        return e, None, 1.0 / z

    e1, ec1, r1 = softmax_parts(_bf(jnp.where(lane < C_HEAD_DIM, q, 0.0)))
    e2, ec2, r2 = softmax_parts(_bf(jnp.where(lane >= C_HEAD_DIM, q, 0.0)))
    r2 = lam * r2
    o = jnp.dot(_bf(e1 * r1 - e2 * r2), vb, preferred_element_type=jnp.float32)
    if latent:
        o = o + jnp.dot(_bf(ec1 * r1 - ec2 * r2), vcb, preferred_element_type=jnp.float32)
    o = o * lax.rsqrt(jnp.mean(o * o, axis=-1, keepdims=True) + 1e-6) * g_ref[...] * (1.0 - lam_init)
    o_ref[...] = o


def diff_attention_mixer(z, lam, lam_init, subln_g, cache=None, layer=None, rope=None):
    B, T, _ = z.shape
    latent = cache is not None
    tq = min(DIFF_Q_ROWS, T)
    head = lambda base: pl.BlockSpec((None, tq, LANE), lambda b, h, i: (b, i, base + h))
    whole = lambda base: pl.BlockSpec((None, T, LANE), lambda b, h, i: (b, 0, base + h))
    in_specs = [pl.BlockSpec(memory_space=pltpu.SMEM), head(ZC_Q), whole(ZC_K), whole(ZC_V)]
    args = [lam.reshape(1), z, z, z]
    if latent:
        P = cache[0].shape[3]
        cspec = pl.BlockSpec((None, None, None, P, LANE), lambda b, h, i: (b, layer, h, 0, 0))
        tq_spec = pl.BlockSpec((tq, LANE), lambda b, h, i: (i, 0))
        tk_spec = pl.BlockSpec((T, LANE), lambda b, h, i: (0, 0))
        in_specs += [cspec, cspec, tq_spec, tq_spec, tk_spec, tk_spec]
        args += [cache[0], cache[1], rope[0], rope[1], rope[0], rope[1]]
    in_specs.append(pl.BlockSpec((1, LANE), lambda b, h, i: (0, 0)))
    args.append(subln_g.reshape(1, LANE))
    return pl.pallas_call(
        partial(_diff_attn_kernel, latent=latent, lam_init=lam_init),
        grid=(B, C_HEADS, T // tq),
        in_specs=in_specs,
        out_specs=pl.BlockSpec((None, tq, LANE), lambda b, h, i: (b, i, h)),
        out_shape=jax.ShapeDtypeStruct((B, T, C_HEADS * LANE), jnp.float32),
        scratch_shapes=[pltpu.VMEM((T, LANE), jnp.bfloat16)],
        compiler_params=pltpu.CompilerParams(
            dimension_semantics=("arbitrary", "arbitrary", "arbitrary"), vmem_limit_bytes=VMEM_LIMIT_BYTES),
        name="diff_attention",
    )(*args)


def _gqa_kernel(sink_ref, q0_ref, q1_ref, *rest, latent, n_tokens):
    if latent:
        (kp_ref, kn_ref, kx_ref, vp_ref, vn_ref, vx_ref, kc_ref, vc_ref,
         cq_ref, sq_ref, cp_ref, sp_ref, cx_ref, sx_ref, o_ref) = rest
    else:
        kc_ref, vc_ref, o_ref = rest
    kvh = pl.program_id(1)
    n = pl.program_id(2)
    scale = B_HEAD_DIM ** -0.5
    q0, q1 = q0_ref[...], q1_ref[...]
    if latent:
        q0 = _rope(q0, cq_ref[...], sq_ref[...])
        q1 = _rope(q1, cq_ref[...], sq_ref[...])
    q = _bf(jnp.concatenate([q0, q1], axis=0))
    row = lax.broadcasted_iota(jnp.int32, (2 * BLOCK, 1), 0)
    sink = jnp.where(row < BLOCK, sink_ref[2 * kvh], sink_ref[2 * kvh + 1])

    s_c = lax.dot_general(q, _bf(kc_ref[...]), _DOT_NT, preferred_element_type=jnp.float32) * scale
    m = jnp.maximum(jnp.max(s_c, axis=-1, keepdims=True), sink)
    if latent:
        kw = jnp.concatenate([_rope(kp_ref[...], cp_ref[...], sp_ref[...]),
                              _rope(kn_ref[...], cq_ref[...], sq_ref[...]),
                              _rope(kx_ref[...], cx_ref[...], sx_ref[...])], axis=0)
        vw = jnp.concatenate([vp_ref[...], vn_ref[...], vx_ref[...]], axis=0)
        s_w = lax.dot_general(q, _bf(kw), _DOT_NT, preferred_element_type=jnp.float32) * scale
        qpos = n * BLOCK + lax.broadcasted_iota(jnp.int32, s_w.shape, 0) % BLOCK
        kpos = (n - 1) * BLOCK + lax.broadcasted_iota(jnp.int32, s_w.shape, 1)
        valid = (jnp.abs(qpos - kpos) <= WINDOW) & (kpos >= 0) & (kpos < n_tokens)
        s_w = jnp.where(valid, s_w, NEG_INF)
        m = jnp.maximum(m, jnp.max(s_w, axis=-1, keepdims=True))
        e_w = jnp.exp(s_w - m)
    e_c = jnp.exp(s_c - m)
    zsum = jnp.sum(e_c, axis=-1, keepdims=True) + jnp.exp(sink - m)
    o = jnp.dot(_bf(e_c), _bf(vc_ref[...]), preferred_element_type=jnp.float32)
    if latent:
        zsum = zsum + jnp.sum(e_w, axis=-1, keepdims=True)
        o = o + jnp.dot(_bf(e_w), _bf(vw), preferred_element_type=jnp.float32)
    o = o / zsum
    o_ref[:, 0:LANE] = o[0:BLOCK]
    o_ref[:, LANE:2 * LANE] = o[BLOCK:2 * BLOCK]


def gqa_mixer(z, sink, cache=None, layer=None, rope=None):
    B, T, _ = z.shape
    latent = cache is not None
    nb = T // BLOCK
    blk = lambda base, off: pl.BlockSpec(
        (None, BLOCK, LANE), lambda b, kvh, n: (b, jnp.clip(n + off, 0, nb - 1), base + kvh))
    qspec = lambda g: pl.BlockSpec((None, BLOCK, LANE), lambda b, kvh, n: (b, n, ZB_Q + 2 * kvh + g))
    in_specs = [pl.BlockSpec(memory_space=pltpu.SMEM), qspec(0), qspec(1)]
    args = [sink, z, z]
    if latent:
        P = cache[0].shape[3]
        cspec = pl.BlockSpec((None, None, None, P, LANE), lambda b, kvh, n: (b, layer, kvh, 0, 0))
        tab = lambda off: pl.BlockSpec((BLOCK, LANE), lambda b, kvh, n: (jnp.clip(n + off, 0, nb - 1), 0))
        in_specs += [blk(ZB_K, -1), blk(ZB_K, 0), blk(ZB_K, 1), blk(ZB_V, -1), blk(ZB_V, 0), blk(ZB_V, 1),
                     cspec, cspec, tab(0), tab(0), tab(-1), tab(-1), tab(1), tab(1)]
        args += [z] * 6 + [cache[0], cache[1]] + [rope[0], rope[1]] * 3
    else:
        whole = lambda base: pl.BlockSpec((None, T, LANE), lambda b, kvh, n: (b, 0, base + kvh))
        in_specs += [whole(ZB_K), whole(ZB_V)]
        args += [z, z]
    return pl.pallas_call(
        partial(_gqa_kernel, latent=latent, n_tokens=T),
        grid=(B, B_KV_HEADS, nb),
        in_specs=in_specs,
        out_specs=pl.BlockSpec((None, BLOCK, 2 * LANE), lambda b, kvh, n: (b, n, kvh)),
        out_shape=jax.ShapeDtypeStruct((B, T, B_HEADS * LANE), jnp.float32),
        compiler_params=pltpu.CompilerParams(
            dimension_semantics=("arbitrary", "arbitrary", "arbitrary"), vmem_limit_bytes=VMEM_LIMIT_BYTES),
        name="gqa_attention",
    )(*args)


def split_cols(z, sizes):
    return jnp.split(z, [int(s) for s in np.cumsum(sizes)[:-1]], axis=-1)


def layer_norm(x, g=None, b=None):
    mu = jnp.mean(x, axis=-1, keepdims=True)
    var = jnp.mean(jnp.square(x - mu), axis=-1, keepdims=True)
    y = (x - mu) * lax.rsqrt(var + LN_EPS)
    if g is not None:
        y = y * g + b
    return y


def rwkv7_mixer(za, S0_f, S0_b, lp):
    s0t = jnp.swapaxes(jnp.stack([S0_f, S0_b], axis=0), -1, -2)
    ys, sfin_t = rwkv7_chunked(za, lp['a_w_up'], lp['a_a_up'], lp['a_w0'], lp['a_a0'], lp['a_k_k'], lp['a_k_a'], s0t)
    states = jnp.swapaxes(sfin_t, -1, -2)
    out = rwkv_post(ys, za, lp['a_a_up'], lp['a_a0'], lp['a_k_a'], lp['a_r_k'], lp['a_gn_g'], lp['a_gn_b'],
                    lp['a_g_up'])
    return out, states[0], states[1]


def diff_lambda(lv, layer):
    lam_init = 0.8 - 0.6 * math.exp(-0.3 * layer)
    lam = jnp.exp(jnp.sum(lv[0] * lv[1])) - jnp.exp(jnp.sum(lv[2] * lv[3])) + lam_init
    return lam, lam_init


def adaln_params(cond, lp):
    mod = mm(jax.nn.silu(cond), lp['w_mod']) + lp['b_mod']
    return jnp.split(mod[:, None, :], 6, axis=-1)


def mixer_inputs(x, shift, scale, lp):
    z = ln_mod_matmul(x, scale, shift, lp['w_in'])
    return z, centred_conv3(z, lp['a_conv'])


def merge_mixers(x, yA, yB, yC, z, gate1, shift2, scale2, lp):
    n = x.shape[0] * x.shape[1]
    flat = lambda t: t.reshape(n, t.shape[-1])
    merged = merge_gated(flat(yA), flat(yB), flat(yC), flat(z), lp['p_a'], lp['p_b'], lp['p_c'])
    return out_proj_norm(merged, x, gate1, scale2, shift2, lp['ln1_g'], lp['ln1_b'], lp['w_out'])


def peer_and_norm(part, lp):
    x, ht, gate2 = part
    yt = peer_ffn_tokens(ht, lp['peer_wqt'], lp['peer_keys'], lp['peer_u'], lp['peer_vt'])
    return peer_residual_norm(yt, x, gate2, lp['ln2_g'], lp['ln2_b'])


def context_layer(x, cond, lp, layer):
    Bsz, L, _ = x.shape
    shift1, scale1, gate1, shift2, scale2, gate2 = adaln_params(cond, lp)
    z, zA = mixer_inputs(x, shift1, scale1, lp)
    S0 = jnp.zeros((Bsz, A_HEADS, A_HEAD_DIM, A_HEAD_DIM), jnp.float32)
    yA, S_f, S_b = rwkv7_mixer(zA, S0, S0, lp)
    yB = gqa_mixer(z, lp['b_sink'])
    lam, lam_init = diff_lambda(lp['c_lam'], layer)
    yC = diff_attention_mixer(z, lam, lam_init, lp['c_subln'])
    x_mid, h_mid = merge_mixers(x, yA, yB, yC, z, gate1, shift2, scale2, lp)

    def cache_layout(first_block, n_heads):
        t = z[..., first_block * LANE:(first_block + n_heads) * LANE].reshape(Bsz, L, n_heads, LANE)
        return jnp.transpose(t, (0, 2, 1, 3))
    ctx = (S_f, S_b, cache_layout(ZB_K, B_KV_HEADS), cache_layout(ZB_V, B_KV_HEADS),
           cache_layout(ZC_K, C_HEADS), cache_layout(ZC_V, C_HEADS))
    return (x_mid, h_mid, gate2), ctx


def latent_layer(x, cond, lp, layer, S_f0, S_b0, cache_b, cache_c, rope_b, rope_c):
    shift1, scale1, gate1, shift2, scale2, gate2 = adaln_params(cond, lp)
    z, zA = mixer_inputs(x, shift1, scale1, lp)
    yA, _, _ = rwkv7_mixer(zA, S_f0, S_b0, lp)
    yB = gqa_mixer(z, lp['b_sink'], cache=cache_b, layer=layer, rope=rope_b)
    lam, lam_init = diff_lambda(lp['c_lam'], layer)
    yC = diff_attention_mixer(z, lam, lam_init, lp['c_subln'], cache=cache_c, layer=layer, rope=rope_c)
    x_mid, h_mid = merge_mixers(x, yA, yB, yC, z, gate1, shift2, scale2, lp)
    return x_mid, h_mid, gate2


_BF16_WEIGHTS = ('w_mod', 'w_in', 'a_w_up', 'a_a_up', 'a_g_up', 'p_a', 'p_b', 'p_c', 'w_out', 'peer_u')


def kernel(x_prompt, x_sample, state_a_fwd, state_a_bwd, cache_b_k, cache_b_v, cache_c_k, cache_c_v, c, c_ctx, w_mod, b_mod, w_in, a_conv, a_w0, a_w_up, a_a0, a_a_up, a_g_up, a_k_k, a_k_a, a_r_k, a_gn_g, a_gn_b, b_sink, c_lam, c_subln, p_a, p_b, p_c, w_out, ln1_g, ln1_b, ln2_g, ln2_b, peer_wq, peer_keys, peer_u, peer_v):
    params = dict(w_mod=w_mod, b_mod=b_mod, w_in=w_in, a_conv=a_conv, a_w0=a_w0, a_w_up=a_w_up, a_a0=a_a0,
                  a_a_up=a_a_up, a_g_up=a_g_up, a_k_k=a_k_k, a_k_a=a_k_a, a_r_k=a_r_k, a_gn_g=a_gn_g,
                  a_gn_b=a_gn_b, b_sink=b_sink, c_lam=c_lam, c_subln=c_subln, p_a=p_a, p_b=p_b, p_c=p_c,
                  w_out=w_out, ln1_g=ln1_g, ln1_b=ln1_b, ln2_g=ln2_g, ln2_b=ln2_b, peer_wq=peer_wq,
                  peer_keys=peer_keys, peer_u=peer_u, peer_v=peer_v)
    for name in _BF16_WEIGHTS:
        params[name] = params[name].astype(jnp.bfloat16)
    w_in_bf = params['w_in']
    params['w_in'] = jnp.concatenate([w_in_bf[..., IN_COLS - 3 * D_MODEL:], w_in_bf[..., :IN_COLS - 3 * D_MODEL]], axis=-1)
    params['peer_wqt'] = jnp.swapaxes(params.pop('peer_wq'), 1, 2).astype(jnp.bfloat16)
    params['peer_vt'] = jnp.swapaxes(params.pop('peer_v'), 1, 2).astype(jnp.bfloat16)
    n_lat = x_sample.shape[1]
    rope_b = rope_lane_tables(n_lat, B_HEAD_DIM)
    rope_c = rope_lane_tables(n_lat, C_HEAD_DIM)
    cond_ctx = jnp.broadcast_to(c_ctx[None, :], (x_prompt.shape[0], D_MODEL))
    y_prompt, y_sample = x_prompt, x_sample
    new = [[], [], [], [], [], []]
    for layer in range(DEPTH):
        lp = {name: val[layer] for name, val in params.items()}
        part_ctx, ctx = context_layer(y_prompt, cond_ctx, lp, layer)
        for lst, t in zip(new, ctx):
            lst.append(t)
        part_lat = latent_layer(y_sample, c, lp, layer,
                                state_a_fwd[:, layer], state_a_bwd[:, layer],
                                (cache_b_k, cache_b_v), (cache_c_k, cache_c_v), rope_b, rope_c)
        y_prompt = peer_and_norm(part_ctx, lp)
        y_sample = peer_and_norm(part_lat, lp)
    return (y_prompt, y_sample) + tuple(jnp.stack(lst, axis=1) for lst in new)
```

```python
import math
from functools import partial

import jax
import jax.numpy as jnp
from jax import lax
from jax.experimental import pallas as pl
from jax.experimental.pallas import tpu as pltpu

D_MODEL = 2048
DEPTH = 4
GRID_W = 64
BLOCK = 128
A_HEADS = 16
A_HEAD_DIM = 64
A_WIDTH = A_HEADS * A_HEAD_DIM
W_LORA = 64
ICL_LORA = 64
G_LORA = 128
A_COLS = 3 * A_WIDTH + G_LORA + 2 * W_LORA + 2 * ICL_LORA
A_GN_EPS = 64e-5
B_HEADS = 4
B_KV_HEADS = 2
B_HEAD_DIM = 128
B_WIDTH = B_HEADS * B_HEAD_DIM
B_KV_WIDTH = B_KV_HEADS * B_HEAD_DIM
WINDOW = 128
C_HEADS = 4
C_HEAD_DIM = 64
C_WIDTH = C_HEADS * 2 * C_HEAD_DIM
IN_COLS = A_COLS + B_WIDTH + 2 * B_KV_WIDTH + 3 * C_WIDTH + 3 * D_MODEL
PEER_HEADS = 8
N_KEYS = 128
PEER_QDIM = 256
PEER_TOPK = 16
ROPE_BASE = 10000.0
LN_EPS = 1e-5
NEG_INF = -1e30
DEEPNORM_ALPHA = (2 * DEPTH) ** 0.25

LANE = 128
VMEM_LIMIT_BYTES = 56 * 1024 * 1024


def _mm_kernel(x_ref, w_ref, o_ref):
    o_ref[...] = jnp.dot(x_ref[...].astype(jnp.bfloat16), w_ref[...],
                         preferred_element_type=jnp.float32)


def _pick_tile(n, cap, unit):
    if n <= cap:
        return n
    best = None
    for t in range(unit, cap + 1, unit):
        if n % t == 0:
            best = t
    assert best is not None, (n, cap, unit)
    return best


def mm(x, w):
    M, K = x.shape
    N = w.shape[1]
    tm = _pick_tile(M, 512, 8)
    tn = _pick_tile(N, 1024, LANE)
    return pl.pallas_call(
        _mm_kernel,
        grid=(N // tn, M // tm),
        in_specs=[pl.BlockSpec((tm, K), lambda j, i: (i, 0)),
                  pl.BlockSpec((K, tn), lambda j, i: (0, j))],
        out_specs=pl.BlockSpec((tm, tn), lambda j, i: (i, j)),
        out_shape=jax.ShapeDtypeStruct((M, N), jnp.float32),
        compiler_params=pltpu.CompilerParams(
            dimension_semantics=("arbitrary", "arbitrary"),
            vmem_limit_bytes=VMEM_LIMIT_BYTES),
        name="mm",
    )(x, w)


ROW_TILE = 1024
NORM_ROW_TILE = 512
Z_GATES = 0
Z_A = 3 * D_MODEL
Z_B = Z_A + A_COLS
Z_C = Z_B + B_WIDTH + 2 * B_KV_WIDTH


def _row_blocks(n_batch, n_time, rows):
    tt = min(n_time, rows)
    bt = rows // tt
    assert n_time % tt == 0 and n_batch % bt == 0, (n_batch, n_time, rows)
    return bt, tt


def _layer_norm_rows(x):
    mu = jnp.mean(x, axis=-1, keepdims=True)
    xc = x - mu
    return xc * lax.rsqrt(jnp.mean(xc * xc, axis=-1, keepdims=True) + LN_EPS)


def _ln_mod_mm_kernel(x_ref, scale_ref, shift_ref, w_ref, o_ref, h_ref):
    bt, tt, K = x_ref.shape

    @pl.when(pl.program_id(2) == 0)
    def _():
        h = _layer_norm_rows(x_ref[...]) * (1.0 + scale_ref[...]) + shift_ref[...]
        h_ref[...] = h.reshape(bt * tt, K).astype(jnp.bfloat16)

    o = jnp.dot(h_ref[...], w_ref[...], preferred_element_type=jnp.float32)
    o_ref[...] = o.reshape(bt, tt, o.shape[-1])


def ln_mod_matmul(x, scale, shift, w):
    B, T, K = x.shape
    N = w.shape[1]
    bt, tt = _row_blocks(B, T, ROW_TILE)
    tn = _pick_tile(N, 1024, LANE)
    vec = pl.BlockSpec((bt, 1, K), lambda b, t, j: (b, 0, 0))
    return pl.pallas_call(
        _ln_mod_mm_kernel,
        grid=(B // bt, T // tt, N // tn),
        in_specs=[pl.BlockSpec((bt, tt, K), lambda b, t, j: (b, t, 0)), vec, vec,
                  pl.BlockSpec((K, tn), lambda b, t, j: (0, j))],
        out_specs=pl.BlockSpec((bt, tt, tn), lambda b, t, j: (b, t, j)),
        out_shape=jax.ShapeDtypeStruct((B, T, N), jnp.float32),
        scratch_shapes=[pltpu.VMEM((bt * tt, K), jnp.bfloat16)],
        compiler_params=pltpu.CompilerParams(
            dimension_semantics=("arbitrary", "arbitrary", "arbitrary"), vmem_limit_bytes=VMEM_LIMIT_BYTES),
        name="ln_mod_matmul",
    )(x, scale, shift, w)


def _merge_kernel(ya_ref, yb_ref, yc_ref, ga_ref, gb_ref, gc_ref, pa_ref, pb_ref, pc_ref, o_ref,
                  ya_bf, yb_bf, yc_bf):
    @pl.when(pl.program_id(1) == 0)
    def _():
        ya_bf[...] = ya_ref[...].astype(jnp.bfloat16)
        yb_bf[...] = yb_ref[...].astype(jnp.bfloat16)
        yc_bf[...] = yc_ref[...].astype(jnp.bfloat16)

    f = lambda y, p: jnp.dot(y[...], p[...], preferred_element_type=jnp.float32)
    o = (jax.nn.sigmoid(ga_ref[...]) * f(ya_bf, pa_ref) + jax.nn.sigmoid(gb_ref[...]) * f(yb_bf, pb_ref)
         + jax.nn.sigmoid(gc_ref[...]) * f(yc_bf, pc_ref))
    o_ref[...] = o.astype(jnp.bfloat16)


def merge_gated(yA, yB, yC, z, p_a, p_b, p_c):
    N = yA.shape[0]
    D = p_a.shape[1]
    tm = _pick_tile(N, ROW_TILE, 8)
    tn = 512
    assert Z_GATES == 0 and D % tn == 0
    ysp = lambda y: pl.BlockSpec((tm, y.shape[1]), lambda i, j: (i, 0))
    gsp = lambda k: pl.BlockSpec((tm, tn), lambda i, j: (i, k * (D // tn) + j))
    psp = lambda p: pl.BlockSpec((p.shape[0], tn), lambda i, j: (0, j))
    return pl.pallas_call(
        _merge_kernel,
        grid=(N // tm, D // tn),
        in_specs=[ysp(yA), ysp(yB), ysp(yC), gsp(0), gsp(1), gsp(2), psp(p_a), psp(p_b), psp(p_c)],
        out_specs=pl.BlockSpec((tm, tn), lambda i, j: (i, j)),
        out_shape=jax.ShapeDtypeStruct((N, D), jnp.bfloat16),
        scratch_shapes=[pltpu.VMEM((tm, y.shape[1]), jnp.bfloat16) for y in (yA, yB, yC)],
        compiler_params=pltpu.CompilerParams(
            dimension_semantics=("arbitrary", "arbitrary"), vmem_limit_bytes=VMEM_LIMIT_BYTES),
        name="merge_gated",
    )(yA, yB, yC, z, z, z, p_a, p_b, p_c)


def _out_norm_kernel(m_ref, x_ref, gate_ref, scale_ref, shift_ref, lng_ref, lnb_ref, w_ref, x1_ref, ht_ref):
    bt, tt, D = x_ref.shape
    y = jnp.dot(m_ref[...].reshape(bt * tt, D), w_ref[...], preferred_element_type=jnp.float32)
    x1 = _layer_norm_rows(DEEPNORM_ALPHA * x_ref[...] + gate_ref[...] * y.reshape(bt, tt, D))
    x1 = x1 * lng_ref[...] + lnb_ref[...]
    x1_ref[...] = x1
    h = _layer_norm_rows(x1) * (1.0 + scale_ref[...]) + shift_ref[...]
    ht_ref[...] = jnp.transpose(h.reshape(bt * tt, D)).astype(jnp.bfloat16)


def out_proj_norm(merged, x, gate1, scale2, shift2, ln_g, ln_b, w_out):
    B, T, D = x.shape
    bt, tt = _row_blocks(B, T, NORM_ROW_TILE)
    rows = pl.BlockSpec((bt, tt, D), lambda b, t: (b, t, 0))
    vec = pl.BlockSpec((bt, 1, D), lambda b, t: (b, 0, 0))
    par = pl.BlockSpec((1, 1, D), lambda b, t: (0, 0, 0))
    return pl.pallas_call(
        _out_norm_kernel,
        grid=(B // bt, T // tt),
        in_specs=[rows, rows, vec, vec, vec, par, par, pl.BlockSpec((D, D), lambda b, t: (0, 0))],
        out_specs=[rows, pl.BlockSpec((D, bt * tt), lambda b, t: (0, b * (T // tt) + t))],
        out_shape=[jax.ShapeDtypeStruct((B, T, D), jnp.float32), jax.ShapeDtypeStruct((D, B * T), jnp.bfloat16)],
        compiler_params=pltpu.CompilerParams(
            dimension_semantics=("arbitrary", "arbitrary"), vmem_limit_bytes=VMEM_LIMIT_BYTES),
        name="out_proj_norm",
    )(merged.reshape(B, T, D), x, gate1, scale2, shift2, ln_g.reshape(1, 1, D), ln_b.reshape(1, 1, D), w_out)


def _peer_norm_kernel(yt_ref, x_ref, gate_ref, lng_ref, lnb_ref, o_ref):
    bt, tt, D = x_ref.shape
    y = jnp.transpose(yt_ref[...]).reshape(bt, tt, D)
    x = _layer_norm_rows(DEEPNORM_ALPHA * x_ref[...] + gate_ref[...] * y)
    o_ref[...] = x * lng_ref[...] + lnb_ref[...]


def peer_residual_norm(yt, x, gate2, ln_g, ln_b):
    B, T, D = x.shape
    bt, tt = _row_blocks(B, T, NORM_ROW_TILE)
    rows = pl.BlockSpec((bt, tt, D), lambda b, t: (b, t, 0))
    par = pl.BlockSpec((1, 1, D), lambda b, t: (0, 0, 0))
    return pl.pallas_call(
        _peer_norm_kernel,
        grid=(B // bt, T // tt),
        in_specs=[pl.BlockSpec((D, bt * tt), lambda b, t: (0, b * (T // tt) + t)), rows,
                  pl.BlockSpec((bt, 1, D), lambda b, t: (b, 0, 0)), par, par],
        out_specs=rows,
        out_shape=jax.ShapeDtypeStruct((B, T, D), jnp.float32),
        compiler_params=pltpu.CompilerParams(
            dimension_semantics=("arbitrary", "arbitrary"), vmem_limit_bytes=VMEM_LIMIT_BYTES),
        name="peer_residual_norm",
    )(yt, x, gate2, ln_g.reshape(1, 1, D), ln_b.reshape(1, 1, D))


RWKV_CHUNK = 64
RWKV_HEADS_PER_STEP = 16

_NN = (((2,), (1,)), ((0,), (0,)))
_NT = (((2,), (2,)), ((0,), (0,)))
_TN = (((1,), (1,)), ((0,), (0,)))


def _split2(x):
    hi = x.astype(jnp.bfloat16)
    lo = (x - hi.astype(jnp.float32)).astype(jnp.bfloat16)
    return hi, lo


def _dot3(a, b, dims=_NN):
    f = lambda x, y: lax.dot_general(x, y, dims, preferred_element_type=jnp.float32)
    return f(a[0], b[0]) + (f(a[0], b[1]) + f(a[1], b[0]))


def _dot1(a, b, dims=_NN):
    return lax.dot_general(a[0], b[0], dims, preferred_element_type=jnp.float32)


def _split_heads(x):
    n = A_HEAD_DIM
    return jnp.stack([x[:, h * n:(h + 1) * n] for h in range(x.shape[1] // n)], axis=0)


def _lora_half(x, d, w_ref):
    lane = lax.broadcasted_iota(jnp.int32, x.shape, 1)
    xd = jnp.where(lane // (LANE // 2) == d, x, 0.0).astype(jnp.bfloat16)
    return jnp.dot(xd, w_ref[...], preferred_element_type=jnp.float32)


def _rwkv_chunk_kernel(zr_ref, zk_ref, zv_ref, wd_ref, ad_ref, wup_ref, aup_ref, w0_ref, a0_ref, kk_ref, ka_ref,
                       s0_ref, y_ref, sfin_ref, st_ref):
    d = pl.program_id(0)
    c = pl.program_id(3)
    n_chunks = pl.num_programs(3)
    HB = st_ref.shape[0]
    C = zr_ref.shape[0]

    @pl.when(c == 0)
    def _():
        st_ref[...] = s0_ref[...]

    k_all = zk_ref[...]
    lw_all = -math.exp(-0.5) * jax.nn.sigmoid(_lora_half(jnp.tanh(wd_ref[...]), d, wup_ref) + w0_ref[...])
    icl_all = jax.nn.sigmoid(_lora_half(ad_ref[...], d, aup_ref) + a0_ref[...])
    r, v = _split_heads(zr_ref[...]), _split_heads(zv_ref[...])
    lw = _split_heads(lw_all)
    k = _split_heads(k_all * (1.0 + (icl_all - 1.0) * ka_ref[...]))
    kk = _split_heads(k_all * kk_ref[...])
    kk = kk * lax.rsqrt(jnp.sum(kk * kk, axis=-1, keepdims=True) + 1e-12)
    a = -kk
    b = kk * _split_heads(icl_all)

    ti = lax.broadcasted_iota(jnp.int32, (HB, C, C), 1)
    si = lax.broadcasted_iota(jnp.int32, (HB, C, C), 2)
    diff = (si - ti) * (1 - 2 * d)
    m_incl = diff <= 0
    m_strict = diff < 0
    m_incl_bf = jnp.where(m_incl, 1.0, 0.0).astype(jnp.bfloat16)
    eye = jnp.where(diff == 0, 1.0, 0.0).astype(jnp.float32)

    st = st_ref[...]

    lw_hi, lw_lo = _split2(lw)
    lw_lo2 = (lw - lw_hi.astype(jnp.float32) - lw_lo.astype(jnp.float32)).astype(jnp.bfloat16)
    f = lambda y: lax.dot_general(m_incl_bf, y, _NN, preferred_element_type=jnp.float32)
    cum = f(lw_hi) + (f(lw_lo) + f(lw_lo2))
    total = jnp.sum(lw, axis=1, keepdims=True)
    e_cum = jnp.exp(cum)
    e_inv = jnp.exp(-cum)
    e_tot = jnp.exp(total)
    hi = lambda x: (x.astype(jnp.bfloat16),)
    at_f = a * jnp.exp(cum - lw)
    rt_f = r * e_cum
    at, rt = hi(at_f), hi(rt_f)
    bt_f = b * e_inv
    kt_f = k * e_inv
    bh, kh = _split2(bt_f * e_tot), _split2(kt_f * e_tot)
    vs = _split2(v)

    scores = _dot1(hi(jnp.concatenate([at_f, rt_f], axis=1)),
                   hi(jnp.concatenate([bt_f, kt_f], axis=1)), _NT)
    l_ab = jnp.where(m_strict, scores[:, :C, :C], 0.0)
    l_ak = jnp.where(m_strict, scores[:, :C, C:], 0.0)
    m_rb = jnp.where(m_incl, scores[:, C:, :C], 0.0)
    m_rk = jnp.where(m_incl, scores[:, C:, C:], 0.0)

    tm = eye + l_ab
    lp = hi(l_ab)
    for _ in range(int(math.log2(C)) - 1):
        lp = hi(_dot1(lp, lp))
        tm = tm + _dot1(hi(tm), lp)
    tms = hi(tm)
    p = _split2(_dot1(tms, at))
    q = _dot1(tms, hi(_dot1(hi(l_ak), vs)))

    sts = _split2(st)
    u = _split2(_dot3(p, sts) + q)
    y = _dot1(rt, sts) + _dot1(hi(m_rb), u) + _dot1(hi(m_rk), vs)
    y_ref[...] = jnp.concatenate([y[h] for h in range(HB)], axis=-1)
    st_ref[...] = jnp.swapaxes(e_tot, 1, 2) * st + _dot3(bh, u, _TN) + _dot3(kh, vs, _TN)

    @pl.when(c == n_chunks - 1)
    def _():
        sfin_ref[...] = st_ref[...]


def rwkv7_chunked(zc, w_up, a_up, w0, a0, k_k, k_a, s0t):
    B, T, _ = zc.shape
    D = 2
    N, H, W = A_HEAD_DIM, A_HEADS, A_WIDTH
    C, HB = RWKV_CHUNK, RWKV_HEADS_PER_STEP
    G = H // HB
    lanes = HB * N
    assert T % C == 0 and H % HB == 0 and 2 * W_LORA == LANE and 2 * ICL_LORA == LANE
    n_chunks = T // C
    lora_block = (3 * W + G_LORA) // LANE
    chunk = lambda d, c: c + d * (n_chunks - 1 - 2 * c)
    zcol = lambda base: pl.BlockSpec((None, C, lanes), lambda d, bb, g, c: (bb, chunk(d, c), base * G + g))
    lora = lambda off: pl.BlockSpec((None, C, LANE), lambda d, bb, g, c: (bb, chunk(d, c), lora_block + off))
    up = pl.BlockSpec((LANE, lanes), lambda d, bb, g, c: (0, g))
    per_dir = pl.BlockSpec((None, None, C, lanes), lambda d, bb, g, c: (d, bb, chunk(d, c), g))
    dir_row = pl.BlockSpec((None, 1, lanes), lambda d, bb, g, c: (d, 0, g))
    row = pl.BlockSpec((1, lanes), lambda d, bb, g, c: (0, g))
    state = pl.BlockSpec((None, None, HB, N, N), lambda d, bb, g, c: (d, bb, g, 0, 0))
    return pl.pallas_call(
        _rwkv_chunk_kernel,
        grid=(D, B, G, n_chunks),
        in_specs=[zcol(0), zcol(1), zcol(2), lora(0), lora(1), up, up, dir_row, dir_row, row, row, state],
        out_specs=[per_dir, state],
        out_shape=[jax.ShapeDtypeStruct((D, B, T, W), jnp.float32),
                   jax.ShapeDtypeStruct((D, B, H, N, N), jnp.float32)],
        scratch_shapes=[pltpu.VMEM((HB, N, N), jnp.float32)],
        compiler_params=pltpu.CompilerParams(
            dimension_semantics=("arbitrary", "arbitrary", "arbitrary", "arbitrary"),
            vmem_limit_bytes=VMEM_LIMIT_BYTES),
        name="rwkv7_chunked",
    )(zc, zc, zc, zc, zc, w_up.reshape(LANE, W), a_up.reshape(LANE, W), w0.reshape(D, 1, W), a0.reshape(D, 1, W),
      k_k.reshape(1, W), k_a.reshape(1, W), s0t)


CONV_COLS = 384
CONV_ROWS = 2048
SUBLANE = 8


def _conv3_kernel(x_ref, prev_ref, next_ref, w_ref, o_ref):
    t = pl.program_id(1)
    x = x_ref[...]
    tt = x.shape[0]
    row = lax.broadcasted_iota(jnp.int32, x.shape, 0)
    before = jnp.where(t > 0, prev_ref[SUBLANE - 1:SUBLANE, :], 0.0)
    after = jnp.where(t < pl.num_programs(1) - 1, next_ref[0:1, :], 0.0)
    x_prev = jnp.where(row == 0, before, pltpu.roll(x, 1, axis=0))
    x_next = jnp.where(row == tt - 1, after, pltpu.roll(x, tt - 1, axis=0))
    o_ref[...] = x_prev * w_ref[0:1, :] + x * w_ref[1:2, :] + x_next * w_ref[2:3, :]


def centred_conv3(z, w):
    B, T, _ = z.shape
    tt = min(T, CONV_ROWS)
    cb = CONV_COLS
    assert Z_A % cb == 0 and A_COLS % cb == 0 and T % tt == 0
    base = Z_A // cb
    groups = tt // SUBLANE
    last = T // SUBLANE - 1
    return pl.pallas_call(
        _conv3_kernel,
        grid=(B, T // tt, A_COLS // cb),
        in_specs=[pl.BlockSpec((None, tt, cb), lambda b, t, j: (b, t, base + j)),
                  pl.BlockSpec((None, SUBLANE, cb), lambda b, t, j: (b, jnp.maximum(t * groups - 1, 0), base + j)),
                  pl.BlockSpec((None, SUBLANE, cb), lambda b, t, j: (b, jnp.minimum((t + 1) * groups, last), base + j)),
                  pl.BlockSpec((3, cb), lambda b, t, j: (0, j))],
        out_specs=pl.BlockSpec((None, tt, cb), lambda b, t, j: (b, t, j)),
        out_shape=jax.ShapeDtypeStruct((B, T, A_COLS), jnp.float32),
        compiler_params=pltpu.CompilerParams(
            dimension_semantics=("arbitrary", "arbitrary", "arbitrary"), vmem_limit_bytes=VMEM_LIMIT_BYTES),
        name="conv3",
    )(z, z, z, w)


def _rwkv_post_kernel(yf_ref, yb_ref, r_ref, k_ref, v_ref, gd_ref, ad_ref, aup_ref, a0f_ref, a0b_ref,
                      ka_ref, rk_ref, gng_ref, gnb_ref, gup_ref, o_ref, ones_ref):
    W = o_ref.shape[-1]
    rows = o_ref.shape[0] * o_ref.shape[1]

    @pl.when((pl.program_id(0) == 0) & (pl.program_id(1) == 0))
    def _():
        hr = lax.broadcasted_iota(jnp.int32, (W, W), 0) // A_HEAD_DIM
        hc = lax.broadcasted_iota(jnp.int32, (W, W), 1) // A_HEAD_DIM
        ones_ref[...] = jnp.where(hr == hc, 1.0, 0.0).astype(jnp.bfloat16)

    def head_sum(x):
        hi, lo = _split2(x)
        f = lambda p: jnp.dot(p, ones_ref[...], preferred_element_type=jnp.float32)
        return f(hi) + f(lo)

    flat = lambda ref: ref[...].reshape(rows, ref.shape[-1])
    inv_n = 1.0 / A_HEAD_DIM
    y = flat(yf_ref) + flat(yb_ref)
    yc = y - head_sum(y) * inv_n
    var = head_sum(yc * yc) * inv_n
    yn = yc * lax.rsqrt(var + A_GN_EPS) * gng_ref[...] + gnb_ref[...]
    ad = flat(ad_ref)
    icl_sum = (jax.nn.sigmoid(_lora_half(ad, 0, aup_ref) + a0f_ref[...])
               + jax.nn.sigmoid(_lora_half(ad, 1, aup_ref) + a0b_ref[...]))
    kd_sum = flat(k_ref) * (2.0 + (icl_sum - 2.0) * ka_ref[...])
    bonus = head_sum(flat(r_ref) * kd_sum * rk_ref[...]) * flat(v_ref)
    g = jnp.dot(jax.nn.sigmoid(flat(gd_ref)).astype(jnp.bfloat16), gup_ref[...], preferred_element_type=jnp.float32)
    o_ref[...] = ((yn + bonus) * g).reshape(o_ref.shape)


def rwkv_post(ys, za, a_up, a0, k_a, r_k, gn_g, gn_b, g_up):
    _, B, T, W = ys.shape
    bt, tt = _row_blocks(B, T, NORM_ROW_TILE)
    both = lambda d: pl.BlockSpec((None, bt, tt, W), lambda b, t: (d, b, t, 0))
    zcol = lambda j: pl.BlockSpec((bt, tt, W), lambda b, t: (b, t, j))
    dvec = lambda d: pl.BlockSpec((None, 1, W), lambda b, t: (d, 0, 0))
    vec = pl.BlockSpec((1, W), lambda b, t: (0, 0))
    assert (3 * W) % G_LORA == 0
    return pl.pallas_call(
        _rwkv_post_kernel,
        grid=(B // bt, T // tt),
        in_specs=[both(0), both(1), zcol(0), zcol(1), zcol(2),
                  pl.BlockSpec((bt, tt, G_LORA), lambda b, t: (b, t, 3 * W // G_LORA)),
                  pl.BlockSpec((bt, tt, LANE), lambda b, t: (b, t, (3 * W + G_LORA) // LANE + 1)),
                  pl.BlockSpec((LANE, W), lambda b, t: (0, 0)), dvec(0), dvec(1), vec, vec, vec, vec,
                  pl.BlockSpec((G_LORA, W), lambda b, t: (0, 0))],
        out_specs=pl.BlockSpec((bt, tt, W), lambda b, t: (b, t, 0)),
        out_shape=jax.ShapeDtypeStruct((B, T, W), jnp.float32),
        scratch_shapes=[pltpu.VMEM((W, W), jnp.bfloat16)],
        compiler_params=pltpu.CompilerParams(
            dimension_semantics=("arbitrary", "arbitrary"), vmem_limit_bytes=VMEM_LIMIT_BYTES),
        name="rwkv_post",
    )(ys, ys, za, za, za, za, za, a_up.reshape(LANE, W), a0.reshape(2, 1, W), a0.reshape(2, 1, W),
      k_a.reshape(1, W), r_k.reshape(1, W), gn_g.reshape(1, W), gn_b.reshape(1, W), g_up)


PEER_ROUTE_TOKENS = 256
PEER_TOKENS = 512
PEER_E1_PER_STEP = 8
PEER_E1_PER_PART = 2
PEER_CAND_ROWS = 56


def _top_rows(s_ref, n_rows, k, emit):
    for a in range(k):
        s = s_ref[0:n_rows, :]
        mx = jnp.max(s, axis=0, keepdims=True)
        emit(a, mx)
        if a + 1 < k:
            s_ref[0:n_rows, :] = jnp.where(s == mx, NEG_INF, s)


def _peer_route_kernel(ht_ref, wqt_ref, k1_ref, k2_ref, s1_ref, s2_ref, g1_ref, e2_ref, th_ref,
                       work_ref, t1_ref, t2_ref, cand_ref):
    half = PEER_QDIM // 2
    qt = jnp.dot(wqt_ref[...], ht_ref[...], preferred_element_type=jnp.float32)
    for h in range(PEER_HEADS):
        for which, (k_ref, s_out, t_ref) in enumerate(((k1_ref, s1_ref, t1_ref), (k2_ref, s2_ref, t2_ref))):
            q = qt[h * PEER_QDIM + which * half: h * PEER_QDIM + (which + 1) * half, :]
            q_hi, q_lo = _split2(q)
            k_hi, k_lo = _split2(k_ref[h])
            f = lambda x, y: jnp.dot(x, y, preferred_element_type=jnp.float32)
            s = f(k_hi, q_hi) + (f(k_hi, q_lo) + f(k_lo, q_hi))
            s_out[h] = s
            work_ref[...] = s

            def emit(a, mx, t_ref=t_ref):
                t_ref[a:a + 1, :] = mx
            _top_rows(work_ref, N_KEYS, PEER_TOPK, emit)
        t1 = t1_ref[...]
        t2 = t2_ref[...]
        m1 = t1[0:1, :]
        m2 = t2[0:1, :]
        row = 0
        for a in range(PEER_TOPK):
            nb = PEER_TOPK // (a + 1)
            cand_ref[row:row + nb, :] = t1[a:a + 1, :] + t2[0:nb, :]
            row += nb
        cand_ref[row:PEER_CAND_ROWS, :] = jnp.full((PEER_CAND_ROWS - row, t1.shape[1]), NEG_INF, jnp.float32)
        acc = {}

        def emit_c(a, mx):
            e = jnp.exp(mx - (m1 + m2))
            acc['z'] = e if a == 0 else acc['z'] + e
            acc['th'] = mx
        _top_rows(cand_ref, PEER_CAND_ROWS, PEER_TOPK, emit_c)
        th_ref[h:h + 1, :] = acc['th']
        g1_ref[h] = jnp.exp(s1_ref[h] - m1) / acc['z']
        e2_ref[h] = jnp.exp(s2_ref[h] - m2)


def peer_route(ht, wqt, keys):
    D, N = ht.shape
    tm = PEER_ROUTE_TOKENS
    assert N % tm == 0
    big = jax.ShapeDtypeStruct((PEER_HEADS, N_KEYS, N), jnp.float32)
    big_spec = pl.BlockSpec((PEER_HEADS, N_KEYS, tm), lambda i: (0, 0, i))
    key_spec = pl.BlockSpec((PEER_HEADS, N_KEYS, PEER_QDIM // 2), lambda i: (0, 0, 0))
    return pl.pallas_call(
        _peer_route_kernel,
        grid=(N // tm,),
        in_specs=[pl.BlockSpec((D, tm), lambda i: (0, i)),
                  pl.BlockSpec((PEER_HEADS * PEER_QDIM, D), lambda i: (0, 0)),
                  key_spec, key_spec],
        out_specs=[big_spec, big_spec, big_spec, big_spec, pl.BlockSpec((PEER_HEADS, tm), lambda i: (0, i))],
        out_shape=[big, big, big, big, jax.ShapeDtypeStruct((PEER_HEADS, N), jnp.float32)],
        scratch_shapes=[pltpu.VMEM((N_KEYS, tm), jnp.float32),
                        pltpu.VMEM((PEER_TOPK, tm), jnp.float32),
                        pltpu.VMEM((PEER_TOPK, tm), jnp.float32),
                        pltpu.VMEM((PEER_CAND_ROWS, tm), jnp.float32)],
        compiler_params=pltpu.CompilerParams(
            dimension_semantics=("arbitrary",), vmem_limit_bytes=VMEM_LIMIT_BYTES),
        name="peer_route",
    )(ht, wqt, keys[0], keys[1])


def _gelu_tanh(x):
    return 0.5 * x * (1.0 + jnp.tanh(math.sqrt(2.0 / math.pi) * (x + 0.044715 * (x * x * x))))


def _peer_expert_kernel(ht_ref, u_ref, vt_ref, s1_ref, s2_ref, g1_ref, e2_ref, th_ref, o_ref, wg_ref):
    j = pl.program_id(1)
    tm = ht_ref.shape[1]

    @pl.when(j == 0)
    def _():
        o_ref[...] = jnp.zeros_like(o_ref)

    e1_rows = pl.ds(pl.multiple_of(j * PEER_E1_PER_STEP, PEER_E1_PER_STEP), PEER_E1_PER_STEP)
    ht = ht_ref[...]
    n_parts = PEER_E1_PER_STEP // PEER_E1_PER_PART
    part_slice = lambda p: slice(p * PEER_E1_PER_PART * N_KEYS, (p + 1) * PEER_E1_PER_PART * N_KEYS)
    activation = lambda p: _gelu_tanh(jnp.dot(u_ref[part_slice(p), :], ht, preferred_element_type=jnp.float32))
    act_next = activation(0)
    for part in range(n_parts):
        act = act_next
        if part + 1 < n_parts:
            act_next = activation(part + 1)
        for tc in range(tm // LANE):
            cols = slice(tc * LANE, (tc + 1) * LANE)
            s1_rows = [s1_ref[h, e1_rows, cols] for h in range(PEER_HEADS)]
            g1_rows = [g1_ref[h, e1_rows, cols] for h in range(PEER_HEADS)]
            for ee in range(PEER_E1_PER_PART):
                e = part * PEER_E1_PER_PART + ee
                w = None
                for h in range(PEER_HEADS):
                    score = s2_ref[h, :, cols] + s1_rows[h][e:e + 1, :]
                    gate = e2_ref[h, :, cols] * g1_rows[h][e:e + 1, :]
                    term = jnp.where(score >= th_ref[h:h + 1, cols], gate, 0.0)
                    w = term if w is None else w + term
                rows = slice(ee * N_KEYS, (ee + 1) * N_KEYS)
                wg_ref[part, rows, cols] = (w * act[rows, cols]).astype(jnp.bfloat16)
        if part >= 1:
            o_ref[...] += jnp.dot(vt_ref[:, part_slice(part - 1)], wg_ref[part - 1],
                                  preferred_element_type=jnp.float32)
    o_ref[...] += jnp.dot(vt_ref[:, part_slice(n_parts - 1)], wg_ref[n_parts - 1],
                          preferred_element_type=jnp.float32)


def peer_experts(ht, u, vt, s1, s2, g1, e2, th):
    D, N = ht.shape
    E = u.shape[0]
    tm = PEER_TOKENS
    te = PEER_E1_PER_STEP * N_KEYS
    assert N % tm == 0 and E % te == 0
    big_spec = pl.BlockSpec((PEER_HEADS, N_KEYS, tm), lambda i, j: (0, 0, i))
    return pl.pallas_call(
        _peer_expert_kernel,
        grid=(N // tm, E // te),
        in_specs=[pl.BlockSpec((D, tm), lambda i, j: (0, i)),
                  pl.BlockSpec((te, D), lambda i, j: (j, 0)),
                  pl.BlockSpec((D, te), lambda i, j: (0, j)),
                  big_spec, big_spec, big_spec, big_spec,
                  pl.BlockSpec((PEER_HEADS, tm), lambda i, j: (0, i))],
        out_specs=pl.BlockSpec((D, tm), lambda i, j: (0, i)),
        out_shape=jax.ShapeDtypeStruct((D, N), jnp.float32),
        scratch_shapes=[pltpu.VMEM((PEER_E1_PER_STEP // PEER_E1_PER_PART, PEER_E1_PER_PART * N_KEYS, tm), jnp.bfloat16)],
        compiler_params=pltpu.CompilerParams(
            dimension_semantics=("arbitrary", "arbitrary"), vmem_limit_bytes=VMEM_LIMIT_BYTES),
        name="peer_experts",
    )(ht, u, vt, s1, s2, g1, e2, th)


def peer_ffn_tokens(ht, wqt, keys, u, vt):
    s1, s2, g1, e2, th = peer_route(ht, wqt, keys)
    return peer_experts(ht, u, vt, s1, s2, g1, e2, th)


ZB_Q = Z_B // LANE
ZB_K = ZB_Q + B_HEADS
ZB_V = ZB_K + B_KV_HEADS
ZC_Q = ZB_V + B_KV_HEADS
ZC_K = ZC_Q + C_HEADS
ZC_V = ZC_K + C_HEADS
DIFF_Q_ROWS = 256

_DOT_NT = (((1,), (1,)), ((), ()))


def rope_lane_tables(n_tokens, head_dim):
    rows = n_tokens // GRID_W
    row_pos = jnp.repeat(jnp.arange(rows, dtype=jnp.float32), GRID_W)
    col_pos = jnp.tile(jnp.arange(GRID_W, dtype=jnp.float32), rows)
    n_freq = head_dim // 4
    freqs = ROPE_BASE ** (-jnp.arange(n_freq, dtype=jnp.float32) / n_freq)
    ang = jnp.concatenate([row_pos[:, None] * freqs, col_pos[:, None] * freqs], axis=-1)
    cos = jnp.repeat(jnp.cos(ang), 2, axis=-1)
    sin = jnp.repeat(jnp.sin(ang), 2, axis=-1) * jnp.tile(jnp.array([-1.0, 1.0], jnp.float32), head_dim // 2)
    reps = LANE // head_dim
    return jnp.tile(cos, (1, reps)), jnp.tile(sin, (1, reps))


def _rope(x, cos, sin_signed):
    lane = lax.broadcasted_iota(jnp.int32, x.shape, 1)
    partner = jnp.where(lane % 2 == 0, pltpu.roll(x, LANE - 1, axis=1), pltpu.roll(x, 1, axis=1))
    return x * cos + partner * sin_signed


def _bf(x):
    return x.astype(jnp.bfloat16)


def _diff_attn_kernel(lam_ref, q_ref, k_ref, v_ref, *rest, latent, lam_init):
    if latent:
        kc_ref, vc_ref, cq_ref, sq_ref, ck_ref, sk_ref, g_ref, o_ref, kr_ref, vb_ref = rest
    else:
        g_ref, o_ref, kr_ref, vb_ref = rest
    i = pl.program_id(2)

    @pl.when(i == 0)
    def _():
        k = k_ref[...]
        kr_ref[...] = _bf(_rope(k, ck_ref[...], sk_ref[...]) if latent else k)
        vb_ref[...] = _bf(v_ref[...])

    q = q_ref[...]
    if latent:
        q = _rope(q, cq_ref[...], sq_ref[...])
    lane = lax.broadcasted_iota(jnp.int32, q.shape, 1)
    scale = C_HEAD_DIM ** -0.5
    lam = lam_ref[0]
    kr = kr_ref[...]
    vb = vb_ref[...]
    if latent:
        kcb, vcb = _bf(kc_ref[...]), _bf(vc_ref[...])

    def softmax_parts(qh):
        s = lax.dot_general(qh, kr, _DOT_NT, preferred_element_type=jnp.float32) * scale
        m = jnp.max(s, axis=-1, keepdims=True)
        if latent:
            sc = lax.dot_general(qh, kcb, _DOT_NT, preferred_element_type=jnp.float32) * scale
            m = jnp.maximum(m, jnp.max(sc, axis=-1, keepdims=True))
            ec = jnp.exp(sc - m)
        e = jnp.exp(s - m)
        z = jnp.sum(e, axis=-1, keepdims=True)
        if latent:
            z = z + jnp.sum(ec, axis=-1, keepdims=True)
            return e, ec, 1.0 / z
        return e, None, 1.0 / z

    e1, ec1, r1 = softmax_parts(_bf(jnp.where(lane < C_HEAD_DIM, q, 0.0)))
    e2, ec2, r2 = softmax_parts(_bf(jnp.where(lane >= C_HEAD_DIM, q, 0.0)))
    r2 = lam * r2
    o = jnp.dot(_bf(e1 * r1 - e2 * r2), vb, preferred_element_type=jnp.float32)
    if latent:
        o = o + jnp.dot(_bf(ec1 * r1 - ec2 * r2), vcb, preferred_element_type=jnp.float32)
    o = o * lax.rsqrt(jnp.mean(o * o, axis=-1, keepdims=True) + 1e-6) * g_ref[...] * (1.0 - lam_init)
    o_ref[...] = o


def diff_attention_mixer(z, lam, lam_init, subln_g, cache=None, layer=None, rope=None):
    B, T, _ = z.shape
    latent = cache is not None
    tq = min(DIFF_Q_ROWS, T)
    head = lambda base: pl.BlockSpec((None, tq, LANE), lambda b, h, i: (b, i, base + h))
    whole = lambda base: pl.BlockSpec((None, T, LANE), lambda b, h, i: (b, 0, base + h))
    in_specs = [pl.BlockSpec(memory_space=pltpu.SMEM), head(ZC_Q), whole(ZC_K), whole(ZC_V)]
    args = [lam.reshape(1), z, z, z]
    if latent:
        P = cache[0].shape[3]
        cspec = pl.BlockSpec((None, None, None, P, LANE), lambda b, h, i: (b, layer, h, 0, 0))
        tq_spec = pl.BlockSpec((tq, LANE), lambda b, h, i: (i, 0))
        tk_spec = pl.BlockSpec((T, LANE), lambda b, h, i: (0, 0))
        in_specs += [cspec, cspec, tq_spec, tq_spec, tk_spec, tk_spec]
        args += [cache[0], cache[1], rope[0], rope[1], rope[0], rope[1]]
    in_specs.append(pl.BlockSpec((1, LANE), lambda b, h, i: (0, 0)))
    args.append(subln_g.reshape(1, LANE))
    return pl.pallas_call(
        partial(_diff_attn_kernel, latent=latent, lam_init=lam_init),
        grid=(B, C_HEADS, T // tq),
        in_specs=in_specs,
        out_specs=pl.BlockSpec((None, tq, LANE), lambda b, h, i: (b, i, h)),
        out_shape=jax.ShapeDtypeStruct((B, T, C_HEADS * LANE), jnp.float32),
        scratch_shapes=[pltpu.VMEM((T, LANE), jnp.bfloat16), pltpu.VMEM((T, LANE), jnp.bfloat16)],
        compiler_params=pltpu.CompilerParams(
            dimension_semantics=("arbitrary", "arbitrary", "arbitrary"), vmem_limit_bytes=VMEM_LIMIT_BYTES),
        name="diff_attention",
    )(*args)


def _gqa_kernel(sink_ref, q0_ref, q1_ref, *rest, latent, n_tokens):
    if latent:
        (kp_ref, kn_ref, kx_ref, vp_ref, vn_ref, vx_ref, kc_ref, vc_ref,
         cq_ref, sq_ref, cp_ref, sp_ref, cx_ref, sx_ref, o_ref) = rest
    else:
        kc_ref, vc_ref, o_ref = rest
    kvh = pl.program_id(1)
    n = pl.program_id(2)
    scale = B_HEAD_DIM ** -0.5
    q0, q1 = q0_ref[...], q1_ref[...]
    if latent:
        q0 = _rope(q0, cq_ref[...], sq_ref[...])
        q1 = _rope(q1, cq_ref[...], sq_ref[...])
    q = _bf(jnp.concatenate([q0, q1], axis=0))
    row = lax.broadcasted_iota(jnp.int32, (2 * BLOCK, 1), 0)
    sink = jnp.where(row < BLOCK, sink_ref[2 * kvh], sink_ref[2 * kvh + 1])

    s_c = lax.dot_general(q, _bf(kc_ref[...]), _DOT_NT, preferred_element_type=jnp.float32) * scale
    m = jnp.maximum(jnp.max(s_c, axis=-1, keepdims=True), sink)
    if latent:
        kw = jnp.concatenate([_rope(kp_ref[...], cp_ref[...], sp_ref[...]),
                              _rope(kn_ref[...], cq_ref[...], sq_ref[...]),
                              _rope(kx_ref[...], cx_ref[...], sx_ref[...])], axis=0)
        vw = jnp.concatenate([vp_ref[...], vn_ref[...], vx_ref[...]], axis=0)
        s_w = lax.dot_general(q, _bf(kw), _DOT_NT, preferred_element_type=jnp.float32) * scale
        qpos = n * BLOCK + lax.broadcasted_iota(jnp.int32, s_w.shape, 0) % BLOCK
        kpos = (n - 1) * BLOCK + lax.broadcasted_iota(jnp.int32, s_w.shape, 1)
        valid = (jnp.abs(qpos - kpos) <= WINDOW) & (kpos >= 0) & (kpos < n_tokens)
        s_w = jnp.where(valid, s_w, NEG_INF)
        m = jnp.maximum(m, jnp.max(s_w, axis=-1, keepdims=True))
        e_w = jnp.exp(s_w - m)
    e_c = jnp.exp(s_c - m)
    zsum = jnp.sum(e_c, axis=-1, keepdims=True) + jnp.exp(sink - m)
    o = jnp.dot(_bf(e_c), _bf(vc_ref[...]), preferred_element_type=jnp.float32)
    if latent:
        zsum = zsum + jnp.sum(e_w, axis=-1, keepdims=True)
        o = o + jnp.dot(_bf(e_w), _bf(vw), preferred_element_type=jnp.float32)
    o = o / zsum
    o_ref[:, 0:LANE] = o[0:BLOCK]
    o_ref[:, LANE:2 * LANE] = o[BLOCK:2 * BLOCK]


def gqa_mixer(z, sink, cache=None, layer=None, rope=None):
    B, T, _ = z.shape
    latent = cache is not None
    nb = T // BLOCK
    blk = lambda base, off: pl.BlockSpec(
        (None, BLOCK, LANE), lambda b, kvh, n: (b, jnp.clip(n + off, 0, nb - 1), base + kvh))
    qspec = lambda g: pl.BlockSpec((None, BLOCK, LANE), lambda b, kvh, n: (b, n, ZB_Q + 2 * kvh + g))
    in_specs = [pl.BlockSpec(memory_space=pltpu.SMEM), qspec(0), qspec(1)]
    args = [sink, z, z]
    if latent:
        P = cache[0].shape[3]
        cspec = pl.BlockSpec((None, None, None, P, LANE), lambda b, kvh, n: (b, layer, kvh, 0, 0))
        tab = lambda off: pl.BlockSpec((BLOCK, LANE), lambda b, kvh, n: (jnp.clip(n + off, 0, nb - 1), 0))
        in_specs += [blk(ZB_K, -1), blk(ZB_K, 0), blk(ZB_K, 1), blk(ZB_V, -1), blk(ZB_V, 0), blk(ZB_V, 1),
                     cspec, cspec, tab(0), tab(0), tab(-1), tab(-1), tab(1), tab(1)]
        args += [z] * 6 + [cache[0], cache[1]] + [rope[0], rope[1]] * 3
    else:
        whole = lambda base: pl.BlockSpec((None, T, LANE), lambda b, kvh, n: (b, 0, base + kvh))
        in_specs += [whole(ZB_K), whole(ZB_V)]
        args += [z, z]
    return pl.pallas_call(
        partial(_gqa_kernel, latent=latent, n_tokens=T),
        grid=(B, B_KV_HEADS, nb),
        in_specs=in_specs,
        out_specs=pl.BlockSpec((None, BLOCK, 2 * LANE), lambda b, kvh, n: (b, n, kvh)),
        out_shape=jax.ShapeDtypeStruct((B, T, B_HEADS * LANE), jnp.float32),
        compiler_params=pltpu.CompilerParams(
            dimension_semantics=("arbitrary", "arbitrary", "arbitrary"), vmem_limit_bytes=VMEM_LIMIT_BYTES),
        name="gqa_attention",
    )(*args)


def rwkv7_mixer(za, S0_f, S0_b, lp):
    s0t = jnp.swapaxes(jnp.stack([S0_f, S0_b], axis=0), -1, -2)
    ys, sfin_t = rwkv7_chunked(za, lp['a_w_up'], lp['a_a_up'], lp['a_w0'], lp['a_a0'], lp['a_k_k'], lp['a_k_a'], s0t)
    states = jnp.swapaxes(sfin_t, -1, -2)
    out = rwkv_post(ys, za, lp['a_a_up'], lp['a_a0'], lp['a_k_a'], lp['a_r_k'], lp['a_gn_g'], lp['a_gn_b'],
                    lp['a_g_up'])
    return out, states[0], states[1]


def diff_lambda(lv, layer):
    lam_init = 0.8 - 0.6 * math.exp(-0.3 * layer)
    lam = jnp.exp(jnp.sum(lv[0] * lv[1])) - jnp.exp(jnp.sum(lv[2] * lv[3])) + lam_init
    return lam, lam_init


def adaln_params(cond, lp):
    mod = mm(jax.nn.silu(cond), lp['w_mod']) + lp['b_mod']
    return jnp.split(mod[:, None, :], 6, axis=-1)


def mixer_inputs(x, shift, scale, lp):
    z = ln_mod_matmul(x, scale, shift, lp['w_in'])
    return z, centred_conv3(z, lp['a_conv'])


def merge_mixers(x, yA, yB, yC, z, gate1, shift2, scale2, lp):
    n = x.shape[0] * x.shape[1]
    flat = lambda t: t.reshape(n, t.shape[-1])
    merged = merge_gated(flat(yA), flat(yB), flat(yC), flat(z), lp['p_a'], lp['p_b'], lp['p_c'])
    return out_proj_norm(merged, x, gate1, scale2, shift2, lp['ln1_g'], lp['ln1_b'], lp['w_out'])


def peer_and_norm(part, lp):
    x, ht, gate2 = part
    yt = peer_ffn_tokens(ht, lp['peer_wqt'], lp['peer_keys'], lp['peer_u'], lp['peer_vt'])
    return peer_residual_norm(yt, x, gate2, lp['ln2_g'], lp['ln2_b'])


def context_layer(x, cond, lp, layer):
    Bsz, L, _ = x.shape
    shift1, scale1, gate1, shift2, scale2, gate2 = adaln_params(cond, lp)
    z, zA = mixer_inputs(x, shift1, scale1, lp)
    S0 = jnp.zeros((Bsz, A_HEADS, A_HEAD_DIM, A_HEAD_DIM), jnp.float32)
    yA, S_f, S_b = rwkv7_mixer(zA, S0, S0, lp)
    yB = gqa_mixer(z, lp['b_sink'])
    lam, lam_init = diff_lambda(lp['c_lam'], layer)
    yC = diff_attention_mixer(z, lam, lam_init, lp['c_subln'])
    x_mid, h_mid = merge_mixers(x, yA, yB, yC, z, gate1, shift2, scale2, lp)

    def cache_layout(first_block, n_heads):
        t = z[..., first_block * LANE:(first_block + n_heads) * LANE].reshape(Bsz, L, n_heads, LANE)
        return jnp.transpose(t, (0, 2, 1, 3))
    ctx = (S_f, S_b, cache_layout(ZB_K, B_KV_HEADS), cache_layout(ZB_V, B_KV_HEADS),
           cache_layout(ZC_K, C_HEADS), cache_layout(ZC_V, C_HEADS))
    return (x_mid, h_mid, gate2), ctx


def latent_layer(x, cond, lp, layer, S_f0, S_b0, cache_b, cache_c, rope_b, rope_c):
    shift1, scale1, gate1, shift2, scale2, gate2 = adaln_params(cond, lp)
    z, zA = mixer_inputs(x, shift1, scale1, lp)
    yA, _, _ = rwkv7_mixer(zA, S_f0, S_b0, lp)
    yB = gqa_mixer(z, lp['b_sink'], cache=cache_b, layer=layer, rope=rope_b)
    lam, lam_init = diff_lambda(lp['c_lam'], layer)
    yC = diff_attention_mixer(z, lam, lam_init, lp['c_subln'], cache=cache_c, layer=layer, rope=rope_c)
    x_mid, h_mid = merge_mixers(x, yA, yB, yC, z, gate1, shift2, scale2, lp)
    return x_mid, h_mid, gate2


_BF16_WEIGHTS = ('w_mod', 'w_in', 'a_w_up', 'a_a_up', 'a_g_up', 'p_a', 'p_b', 'p_c', 'w_out', 'peer_u')


def kernel(x_prompt, x_sample, state_a_fwd, state_a_bwd, cache_b_k, cache_b_v, cache_c_k, cache_c_v, c, c_ctx, w_mod, b_mod, w_in, a_conv, a_w0, a_w_up, a_a0, a_a_up, a_g_up, a_k_k, a_k_a, a_r_k, a_gn_g, a_gn_b, b_sink, c_lam, c_subln, p_a, p_b, p_c, w_out, ln1_g, ln1_b, ln2_g, ln2_b, peer_wq, peer_keys, peer_u, peer_v):
    params = dict(w_mod=w_mod, b_mod=b_mod, w_in=w_in, a_conv=a_conv, a_w0=a_w0, a_w_up=a_w_up, a_a0=a_a0,
                  a_a_up=a_a_up, a_g_up=a_g_up, a_k_k=a_k_k, a_k_a=a_k_a, a_r_k=a_r_k, a_gn_g=a_gn_g,
                  a_gn_b=a_gn_b, b_sink=b_sink, c_lam=c_lam, c_subln=c_subln, p_a=p_a, p_b=p_b, p_c=p_c,
                  w_out=w_out, ln1_g=ln1_g, ln1_b=ln1_b, ln2_g=ln2_g, ln2_b=ln2_b, peer_wq=peer_wq,
                  peer_keys=peer_keys, peer_u=peer_u, peer_v=peer_v)
    for name in _BF16_WEIGHTS:
        params[name] = params[name].astype(jnp.bfloat16)
    w_in_bf = params['w_in']
    params['w_in'] = jnp.concatenate([w_in_bf[..., IN_COLS - 3 * D_MODEL:], w_in_bf[..., :IN_COLS - 3 * D_MODEL]], axis=-1)
    params['peer_wqt'] = jnp.swapaxes(params.pop('peer_wq'), 1, 2).astype(jnp.bfloat16)
    params['peer_vt'] = jnp.swapaxes(params.pop('peer_v'), 1, 2).astype(jnp.bfloat16)
    n_lat = x_sample.shape[1]
    rope_b = rope_lane_tables(n_lat, B_HEAD_DIM)
    rope_c = rope_lane_tables(n_lat, C_HEAD_DIM)
    cond_ctx = jnp.broadcast_to(c_ctx[None, :], (x_prompt.shape[0], D_MODEL))
    y_prompt, y_sample = x_prompt, x_sample
    new = [[], [], [], [], [], []]
    for layer in range(DEPTH):
        lp = {name: val[layer] for name, val in params.items()}
        part_ctx, ctx = context_layer(y_prompt, cond_ctx, lp, layer)
        for lst, t in zip(new, ctx):
            lst.append(t)
        part_lat = latent_layer(y_sample, c, lp, layer,
                                state_a_fwd[:, layer], state_a_bwd[:, layer],
                                (cache_b_k, cache_b_v), (cache_c_k, cache_c_v), rope_b, rope_c)
        y_prompt = peer_and_norm(part_ctx, lp)
        y_sample = peer_and_norm(part_lat, lp)
    return (y_prompt, y_sample) + tuple(jnp.stack(lst, axis=1) for lst in new)
```

```python
import math
from functools import partial

import jax
import jax.numpy as jnp
import numpy as np
from jax import lax
from jax.experimental import pallas as pl
from jax.experimental.pallas import tpu as pltpu

D_MODEL = 2048
DEPTH = 4
GRID_W = 64
BLOCK = 128
A_HEADS = 16
A_HEAD_DIM = 64
A_WIDTH = A_HEADS * A_HEAD_DIM
W_LORA = 64
ICL_LORA = 64
G_LORA = 128
A_COLS = 3 * A_WIDTH + G_LORA + 2 * W_LORA + 2 * ICL_LORA
A_GN_EPS = 64e-5
B_HEADS = 4
B_KV_HEADS = 2
B_HEAD_DIM = 128
B_WIDTH = B_HEADS * B_HEAD_DIM
B_KV_WIDTH = B_KV_HEADS * B_HEAD_DIM
WINDOW = 128
C_HEADS = 4
C_HEAD_DIM = 64
C_WIDTH = C_HEADS * 2 * C_HEAD_DIM
IN_COLS = A_COLS + B_WIDTH + 2 * B_KV_WIDTH + 3 * C_WIDTH + 3 * D_MODEL
PEER_HEADS = 8
N_KEYS = 128
PEER_QDIM = 256
PEER_TOPK = 16
ROPE_BASE = 10000.0
LN_EPS = 1e-5
NEG_INF = -1e30
DEEPNORM_ALPHA = (2 * DEPTH) ** 0.25

LANE = 128
VMEM_LIMIT_BYTES = 56 * 1024 * 1024


def _mm_kernel(x_ref, w_ref, o_ref):
    o_ref[...] = jnp.dot(x_ref[...].astype(jnp.bfloat16), w_ref[...],
                         preferred_element_type=jnp.float32)


def _pick_tile(n, cap, unit):
    if n <= cap:
        return n
    best = None
    for t in range(unit, cap + 1, unit):
        if n % t == 0:
            best = t
    assert best is not None, (n, cap, unit)
    return best


def mm(x, w):
    M, K = x.shape
    N = w.shape[1]
    tm = _pick_tile(M, 512, 8)
    tn = _pick_tile(N, 1024, LANE)
    return pl.pallas_call(
        _mm_kernel,
        grid=(N // tn, M // tm),
        in_specs=[pl.BlockSpec((tm, K), lambda j, i: (i, 0)),
                  pl.BlockSpec((K, tn), lambda j, i: (0, j))],
        out_specs=pl.BlockSpec((tm, tn), lambda j, i: (i, j)),
        out_shape=jax.ShapeDtypeStruct((M, N), jnp.float32),
        compiler_params=pltpu.CompilerParams(
            dimension_semantics=("arbitrary", "arbitrary"),
            vmem_limit_bytes=VMEM_LIMIT_BYTES),
        name="mm",
    )(x, w)


def mm3(x, w):
    B, T, K = x.shape
    return mm(x.reshape(B * T, K), w).reshape(B, T, w.shape[1])


ROW_TILE = 1024
NORM_ROW_TILE = 512
Z_GATES = 0
Z_A = 3 * D_MODEL
Z_B = Z_A + A_COLS
Z_C = Z_B + B_WIDTH + 2 * B_KV_WIDTH


def _row_blocks(n_batch, n_time, rows):
    tt = min(n_time, rows)
    bt = rows // tt
    assert n_time % tt == 0 and n_batch % bt == 0, (n_batch, n_time, rows)
    return bt, tt


def _layer_norm_rows(x):
    mu = jnp.mean(x, axis=-1, keepdims=True)
    xc = x - mu
    return xc * lax.rsqrt(jnp.mean(xc * xc, axis=-1, keepdims=True) + LN_EPS)


def _ln_mod_mm_kernel(x_ref, scale_ref, shift_ref, w_ref, o_ref, h_ref):
    bt, tt, K = x_ref.shape

    @pl.when(pl.program_id(2) == 0)
    def _():
        h = _layer_norm_rows(x_ref[...]) * (1.0 + scale_ref[...]) + shift_ref[...]
        h_ref[...] = h.reshape(bt * tt, K).astype(jnp.bfloat16)

    o = jnp.dot(h_ref[...], w_ref[...], preferred_element_type=jnp.float32)
    o_ref[...] = o.reshape(bt, tt, o.shape[-1])


def ln_mod_matmul(x, scale, shift, w):
    B, T, K = x.shape
    N = w.shape[1]
    bt, tt = _row_blocks(B, T, ROW_TILE)
    tn = _pick_tile(N, 1024, LANE)
    vec = pl.BlockSpec((bt, 1, K), lambda b, t, j: (b, 0, 0))
    return pl.pallas_call(
        _ln_mod_mm_kernel,
        grid=(B // bt, T // tt, N // tn),
        in_specs=[pl.BlockSpec((bt, tt, K), lambda b, t, j: (b, t, 0)), vec, vec,
                  pl.BlockSpec((K, tn), lambda b, t, j: (0, j))],
        out_specs=pl.BlockSpec((bt, tt, tn), lambda b, t, j: (b, t, j)),
        out_shape=jax.ShapeDtypeStruct((B, T, N), jnp.float32),
        scratch_shapes=[pltpu.VMEM((bt * tt, K), jnp.bfloat16)],
        compiler_params=pltpu.CompilerParams(
            dimension_semantics=("arbitrary", "arbitrary", "arbitrary"), vmem_limit_bytes=VMEM_LIMIT_BYTES),
        name="ln_mod_matmul",
    )(x, scale, shift, w)


def _merge_kernel(ya_ref, yb_ref, yc_ref, ga_ref, gb_ref, gc_ref, pa_ref, pb_ref, pc_ref, o_ref,
                  ya_bf, yb_bf, yc_bf):
    @pl.when(pl.program_id(1) == 0)
    def _():
        ya_bf[...] = ya_ref[...].astype(jnp.bfloat16)
        yb_bf[...] = yb_ref[...].astype(jnp.bfloat16)
        yc_bf[...] = yc_ref[...].astype(jnp.bfloat16)

    f = lambda y, p: jnp.dot(y[...], p[...], preferred_element_type=jnp.float32)
    o = (jax.nn.sigmoid(ga_ref[...]) * f(ya_bf, pa_ref) + jax.nn.sigmoid(gb_ref[...]) * f(yb_bf, pb_ref)
         + jax.nn.sigmoid(gc_ref[...]) * f(yc_bf, pc_ref))
    o_ref[...] = o.astype(jnp.bfloat16)


def merge_gated(yA, yB, yC, z, p_a, p_b, p_c):
    N = yA.shape[0]
    D = p_a.shape[1]
    tm = _pick_tile(N, ROW_TILE, 8)
    tn = 512
    assert Z_GATES == 0 and D % tn == 0
    ysp = lambda y: pl.BlockSpec((tm, y.shape[1]), lambda i, j: (i, 0))
    gsp = lambda k: pl.BlockSpec((tm, tn), lambda i, j: (i, k * (D // tn) + j))
    psp = lambda p: pl.BlockSpec((p.shape[0], tn), lambda i, j: (0, j))
    return pl.pallas_call(
        _merge_kernel,
        grid=(N // tm, D // tn),
        in_specs=[ysp(yA), ysp(yB), ysp(yC), gsp(0), gsp(1), gsp(2), psp(p_a), psp(p_b), psp(p_c)],
        out_specs=pl.BlockSpec((tm, tn), lambda i, j: (i, j)),
        out_shape=jax.ShapeDtypeStruct((N, D), jnp.bfloat16),
        scratch_shapes=[pltpu.VMEM((tm, y.shape[1]), jnp.bfloat16) for y in (yA, yB, yC)],
        compiler_params=pltpu.CompilerParams(
            dimension_semantics=("arbitrary", "arbitrary"), vmem_limit_bytes=VMEM_LIMIT_BYTES),
        name="merge_gated",
    )(yA, yB, yC, z, z, z, p_a, p_b, p_c)


def _out_norm_kernel(m_ref, x_ref, gate_ref, scale_ref, shift_ref, lng_ref, lnb_ref, w_ref, x1_ref, ht_ref):
    bt, tt, D = x_ref.shape
    y = jnp.dot(m_ref[...].reshape(bt * tt, D), w_ref[...], preferred_element_type=jnp.float32)
    x1 = _layer_norm_rows(DEEPNORM_ALPHA * x_ref[...] + gate_ref[...] * y.reshape(bt, tt, D))
    x1 = x1 * lng_ref[...] + lnb_ref[...]
    x1_ref[...] = x1
    h = _layer_norm_rows(x1) * (1.0 + scale_ref[...]) + shift_ref[...]
    ht_ref[...] = jnp.transpose(h.reshape(bt * tt, D)).astype(jnp.bfloat16)


def out_proj_norm(merged, x, gate1, scale2, shift2, ln_g, ln_b, w_out):
    B, T, D = x.shape
    bt, tt = _row_blocks(B, T, NORM_ROW_TILE)
    rows = pl.BlockSpec((bt, tt, D), lambda b, t: (b, t, 0))
    vec = pl.BlockSpec((bt, 1, D), lambda b, t: (b, 0, 0))
    par = pl.BlockSpec((1, 1, D), lambda b, t: (0, 0, 0))
    return pl.pallas_call(
        _out_norm_kernel,
        grid=(B // bt, T // tt),
        in_specs=[rows, rows, vec, vec, vec, par, par, pl.BlockSpec((D, D), lambda b, t: (0, 0))],
        out_specs=[rows, pl.BlockSpec((D, bt * tt), lambda b, t: (0, b * (T // tt) + t))],
        out_shape=[jax.ShapeDtypeStruct((B, T, D), jnp.float32), jax.ShapeDtypeStruct((D, B * T), jnp.bfloat16)],
        compiler_params=pltpu.CompilerParams(
            dimension_semantics=("arbitrary", "arbitrary"), vmem_limit_bytes=VMEM_LIMIT_BYTES),
        name="out_proj_norm",
    )(merged.reshape(B, T, D), x, gate1, scale2, shift2, ln_g.reshape(1, 1, D), ln_b.reshape(1, 1, D), w_out)


def _peer_norm_kernel(yt_ref, x_ref, gate_ref, lng_ref, lnb_ref, o_ref):
    bt, tt, D = x_ref.shape
    y = jnp.transpose(yt_ref[...]).reshape(bt, tt, D)
    x = _layer_norm_rows(DEEPNORM_ALPHA * x_ref[...] + gate_ref[...] * y)
    o_ref[...] = x * lng_ref[...] + lnb_ref[...]


def peer_residual_norm(yt, x, gate2, ln_g, ln_b):
    B, T, D = x.shape
    bt, tt = _row_blocks(B, T, NORM_ROW_TILE)
    rows = pl.BlockSpec((bt, tt, D), lambda b, t: (b, t, 0))
    par = pl.BlockSpec((1, 1, D), lambda b, t: (0, 0, 0))
    return pl.pallas_call(
        _peer_norm_kernel,
        grid=(B // bt, T // tt),
        in_specs=[pl.BlockSpec((D, bt * tt), lambda b, t: (0, b * (T // tt) + t)), rows,
                  pl.BlockSpec((bt, 1, D), lambda b, t: (b, 0, 0)), par, par],
        out_specs=rows,
        out_shape=jax.ShapeDtypeStruct((B, T, D), jnp.float32),
        compiler_params=pltpu.CompilerParams(
            dimension_semantics=("arbitrary", "arbitrary"), vmem_limit_bytes=VMEM_LIMIT_BYTES),
        name="peer_residual_norm",
    )(yt, x, gate2, ln_g.reshape(1, 1, D), ln_b.reshape(1, 1, D))


RWKV_CHUNK = 64
RWKV_HEADS_PER_STEP = 16

_NN = (((2,), (1,)), ((0,), (0,)))
_NT = (((2,), (2,)), ((0,), (0,)))
_TN = (((1,), (1,)), ((0,), (0,)))


def _split2(x):
    hi = x.astype(jnp.bfloat16)
    lo = (x - hi.astype(jnp.float32)).astype(jnp.bfloat16)
    return hi, lo


def _dot1(a, b, dims=_NN):
    return lax.dot_general(a[0], b[0], dims, preferred_element_type=jnp.float32)


def _split_heads(x):
    n = A_HEAD_DIM
    return jnp.stack([x[:, h * n:(h + 1) * n] for h in range(x.shape[1] // n)], axis=0)


def _lora_half(x, d, w_ref):
    lane = lax.broadcasted_iota(jnp.int32, x.shape, 1)
    xd = jnp.where(lane // (LANE // 2) == d, x, 0.0).astype(jnp.bfloat16)
    return jnp.dot(xd, w_ref[...], preferred_element_type=jnp.float32)


def _rwkv_chunk_kernel(zr_ref, zk_ref, zv_ref, wd_ref, ad_ref, wup_ref, aup_ref, w0_ref, a0_ref, kk_ref, ka_ref,
                       s0_ref, y_ref, sfin_ref, st_ref):
    d = pl.program_id(0)
    c = pl.program_id(3)
    n_chunks = pl.num_programs(3)
    HB = st_ref.shape[0]
    C = zr_ref.shape[0]

    @pl.when(c == 0)
    def _():
        st_ref[...] = s0_ref[...]

    k_all = zk_ref[...]
    lw_all = -math.exp(-0.5) * jax.nn.sigmoid(_lora_half(jnp.tanh(wd_ref[...]), d, wup_ref) + w0_ref[...])
    icl_all = jax.nn.sigmoid(_lora_half(ad_ref[...], d, aup_ref) + a0_ref[...])
    r, v = _split_heads(zr_ref[...]), _split_heads(zv_ref[...])
    lw = _split_heads(lw_all)
    k = _split_heads(k_all * (1.0 + (icl_all - 1.0) * ka_ref[...]))
    kk = _split_heads(k_all * kk_ref[...])
    kk = kk * lax.rsqrt(jnp.sum(kk * kk, axis=-1, keepdims=True) + 1e-12)
    a = -kk
    b = kk * _split_heads(icl_all)

    ti = lax.broadcasted_iota(jnp.int32, (HB, C, C), 1)
    si = lax.broadcasted_iota(jnp.int32, (HB, C, C), 2)
    diff = (si - ti) * (1 - 2 * d)
    m_incl = diff <= 0
    m_strict = diff < 0
    m_incl_bf = jnp.where(m_incl, 1.0, 0.0).astype(jnp.bfloat16)
    eye = jnp.where(diff == 0, 1.0, 0.0).astype(jnp.float32)

    st = st_ref[...]

    lw_hi, lw_lo = _split2(lw)
    lw_lo2 = (lw - lw_hi.astype(jnp.float32) - lw_lo.astype(jnp.float32)).astype(jnp.bfloat16)
    f = lambda y: lax.dot_general(m_incl_bf, y, _NN, preferred_element_type=jnp.float32)
    cum = f(lw_hi) + (f(lw_lo) + f(lw_lo2))
    total = jnp.sum(lw, axis=1, keepdims=True)
    e_cum = jnp.exp(cum)
    e_inv = jnp.exp(-cum)
    e_tot = jnp.exp(total)
    hi = lambda x: (x.astype(jnp.bfloat16),)
    at_f = a * jnp.exp(cum - lw)
    rt_f = r * e_cum
    at, rt = hi(at_f), hi(rt_f)
    bt_f = b * e_inv
    kt_f = k * e_inv
    bh, kh = hi(bt_f * e_tot), hi(kt_f * e_tot)
    vs = hi(v)

    scores = _dot1(hi(jnp.concatenate([at_f, rt_f], axis=1)),
                   hi(jnp.concatenate([bt_f, kt_f], axis=1)), _NT)
    l_ab = jnp.where(m_strict, scores[:, :C, :C], 0.0)
    l_ak = jnp.where(m_strict, scores[:, :C, C:], 0.0)
    m_rb = jnp.where(m_incl, scores[:, C:, :C], 0.0)
    m_rk = jnp.where(m_incl, scores[:, C:, C:], 0.0)

    tm = eye + l_ab
    lp = hi(l_ab)
    for _ in range(int(math.log2(C)) - 1):
        lp = hi(_dot1(lp, lp))
        tm = tm + _dot1(hi(tm), lp)
    tms = hi(tm)
    p = hi(_dot1(tms, at))
    q = _dot1(tms, hi(_dot1(hi(l_ak), vs)))

    sts = hi(st)
    u = hi(_dot1(p, sts) + q)
    y = _dot1(rt, sts) + _dot1(hi(m_rb), u) + _dot1(hi(m_rk), vs)
    y_ref[...] = jnp.concatenate([y[h] for h in range(HB)], axis=-1)
    st_ref[...] = jnp.swapaxes(e_tot, 1, 2) * st + _dot1(bh, u, _TN) + _dot1(kh, vs, _TN)

    @pl.when(c == n_chunks - 1)
    def _():
        sfin_ref[...] = st_ref[...]


def rwkv7_chunked(zc, w_up, a_up, w0, a0, k_k, k_a, s0t):
    B, T, _ = zc.shape
    D = 2
    N, H, W = A_HEAD_DIM, A_HEADS, A_WIDTH
    C, HB = RWKV_CHUNK, RWKV_HEADS_PER_STEP
    G = H // HB
    lanes = HB * N
    assert T % C == 0 and H % HB == 0 and 2 * W_LORA == LANE and 2 * ICL_LORA == LANE
    n_chunks = T // C
    lora_block = (3 * W + G_LORA) // LANE
    chunk = lambda d, c: c + d * (n_chunks - 1 - 2 * c)
    zcol = lambda base: pl.BlockSpec((None, C, lanes), lambda d, bb, g, c: (bb, chunk(d, c), base * G + g))
    lora = lambda off: pl.BlockSpec((None, C, LANE), lambda d, bb, g, c: (bb, chunk(d, c), lora_block + off))
    up = pl.BlockSpec((LANE, lanes), lambda d, bb, g, c: (0, g))
    per_dir = pl.BlockSpec((None, None, C, lanes), lambda d, bb, g, c: (d, bb, chunk(d, c), g))
    dir_row = pl.BlockSpec((None, 1, lanes), lambda d, bb, g, c: (d, 0, g))
    row = pl.BlockSpec((1, lanes), lambda d, bb, g, c: (0, g))
    state = pl.BlockSpec((None, None, HB, N, N), lambda d, bb, g, c: (d, bb, g, 0, 0))
    return pl.pallas_call(
        _rwkv_chunk_kernel,
        grid=(D, B, G, n_chunks),
        in_specs=[zcol(0), zcol(1), zcol(2), lora(0), lora(1), up, up, dir_row, dir_row, row, row, state],
        out_specs=[per_dir, state],
        out_shape=[jax.ShapeDtypeStruct((D, B, T, W), jnp.float32),
                   jax.ShapeDtypeStruct((D, B, H, N, N), jnp.float32)],
        scratch_shapes=[pltpu.VMEM((HB, N, N), jnp.float32)],
        compiler_params=pltpu.CompilerParams(
            dimension_semantics=("arbitrary", "arbitrary", "arbitrary", "arbitrary"),
            vmem_limit_bytes=VMEM_LIMIT_BYTES),
        name="rwkv7_chunked",
    )(zc, zc, zc, zc, zc, w_up.reshape(LANE, W), a_up.reshape(LANE, W), w0.reshape(D, 1, W), a0.reshape(D, 1, W),
      k_k.reshape(1, W), k_a.reshape(1, W), s0t)


CONV_COLS = 384
CONV_ROWS = 2048
SUBLANE = 8


def _conv3_kernel(x_ref, prev_ref, next_ref, w_ref, o_ref):
    t = pl.program_id(1)
    x = x_ref[...]
    tt = x.shape[0]
    row = lax.broadcasted_iota(jnp.int32, x.shape, 0)
    before = jnp.where(t > 0, prev_ref[SUBLANE - 1:SUBLANE, :], 0.0)
    after = jnp.where(t < pl.num_programs(1) - 1, next_ref[0:1, :], 0.0)
    x_prev = jnp.where(row == 0, before, pltpu.roll(x, 1, axis=0))
    x_next = jnp.where(row == tt - 1, after, pltpu.roll(x, tt - 1, axis=0))
    o_ref[...] = x_prev * w_ref[0:1, :] + x * w_ref[1:2, :] + x_next * w_ref[2:3, :]


def centred_conv3(z, w):
    B, T, _ = z.shape
    tt = min(T, CONV_ROWS)
    cb = CONV_COLS
    assert Z_A % cb == 0 and A_COLS % cb == 0 and T % tt == 0
    base = Z_A // cb
    groups = tt // SUBLANE
    last = T // SUBLANE - 1
    return pl.pallas_call(
        _conv3_kernel,
        grid=(B, T // tt, A_COLS // cb),
        in_specs=[pl.BlockSpec((None, tt, cb), lambda b, t, j: (b, t, base + j)),
                  pl.BlockSpec((None, SUBLANE, cb), lambda b, t, j: (b, jnp.maximum(t * groups - 1, 0), base + j)),
                  pl.BlockSpec((None, SUBLANE, cb), lambda b, t, j: (b, jnp.minimum((t + 1) * groups, last), base + j)),
                  pl.BlockSpec((3, cb), lambda b, t, j: (0, j))],
        out_specs=pl.BlockSpec((None, tt, cb), lambda b, t, j: (b, t, j)),
        out_shape=jax.ShapeDtypeStruct((B, T, A_COLS), jnp.float32),
        compiler_params=pltpu.CompilerParams(
            dimension_semantics=("arbitrary", "arbitrary", "arbitrary"), vmem_limit_bytes=VMEM_LIMIT_BYTES),
        name="conv3",
    )(z, z, z, w)


def _rwkv_post_kernel(yf_ref, yb_ref, r_ref, k_ref, v_ref, gd_ref, ad_ref, aup_ref, a0f_ref, a0b_ref,
                      ka_ref, rk_ref, gng_ref, gnb_ref, gup_ref, o_ref, ones_ref):
    W = o_ref.shape[-1]
    rows = o_ref.shape[0] * o_ref.shape[1]

    @pl.when((pl.program_id(0) == 0) & (pl.program_id(1) == 0))
    def _():
        hr = lax.broadcasted_iota(jnp.int32, (W, W), 0) // A_HEAD_DIM
        hc = lax.broadcasted_iota(jnp.int32, (W, W), 1) // A_HEAD_DIM
        ones_ref[...] = jnp.where(hr == hc, 1.0, 0.0).astype(jnp.bfloat16)

    def head_sum(x):
        hi, lo = _split2(x)
        f = lambda p: jnp.dot(p, ones_ref[...], preferred_element_type=jnp.float32)
        return f(hi) + f(lo)

    flat = lambda ref: ref[...].reshape(rows, ref.shape[-1])
    inv_n = 1.0 / A_HEAD_DIM
    y = flat(yf_ref) + flat(yb_ref)
    yc = y - head_sum(y) * inv_n
    var = head_sum(yc * yc) * inv_n
    yn = yc * lax.rsqrt(var + A_GN_EPS) * gng_ref[...] + gnb_ref[...]
    ad = flat(ad_ref)
    icl_sum = (jax.nn.sigmoid(_lora_half(ad, 0, aup_ref) + a0f_ref[...])
               + jax.nn.sigmoid(_lora_half(ad, 1, aup_ref) + a0b_ref[...]))
    kd_sum = flat(k_ref) * (2.0 + (icl_sum - 2.0) * ka_ref[...])
    bonus = head_sum(flat(r_ref) * kd_sum * rk_ref[...]) * flat(v_ref)
    g = jnp.dot(jax.nn.sigmoid(flat(gd_ref)).astype(jnp.bfloat16), gup_ref[...], preferred_element_type=jnp.float32)
    o_ref[...] = ((yn + bonus) * g).reshape(o_ref.shape)


def rwkv_post(ys, za, a_up, a0, k_a, r_k, gn_g, gn_b, g_up):
    _, B, T, W = ys.shape
    bt, tt = _row_blocks(B, T, NORM_ROW_TILE)
    both = lambda d: pl.BlockSpec((None, bt, tt, W), lambda b, t: (d, b, t, 0))
    zcol = lambda j: pl.BlockSpec((bt, tt, W), lambda b, t: (b, t, j))
    dvec = lambda d: pl.BlockSpec((None, 1, W), lambda b, t: (d, 0, 0))
    vec = pl.BlockSpec((1, W), lambda b, t: (0, 0))
    assert (3 * W) % G_LORA == 0
    return pl.pallas_call(
        _rwkv_post_kernel,
        grid=(B // bt, T // tt),
        in_specs=[both(0), both(1), zcol(0), zcol(1), zcol(2),
                  pl.BlockSpec((bt, tt, G_LORA), lambda b, t: (b, t, 3 * W // G_LORA)),
                  pl.BlockSpec((bt, tt, LANE), lambda b, t: (b, t, (3 * W + G_LORA) // LANE + 1)),
                  pl.BlockSpec((LANE, W), lambda b, t: (0, 0)), dvec(0), dvec(1), vec, vec, vec, vec,
                  pl.BlockSpec((G_LORA, W), lambda b, t: (0, 0))],
        out_specs=pl.BlockSpec((bt, tt, W), lambda b, t: (b, t, 0)),
        out_shape=jax.ShapeDtypeStruct((B, T, W), jnp.float32),
        scratch_shapes=[pltpu.VMEM((W, W), jnp.bfloat16)],
        compiler_params=pltpu.CompilerParams(
            dimension_semantics=("arbitrary", "arbitrary"), vmem_limit_bytes=VMEM_LIMIT_BYTES),
        name="rwkv_post",
    )(ys, ys, za, za, za, za, za, a_up.reshape(LANE, W), a0.reshape(2, 1, W), a0.reshape(2, 1, W),
      k_a.reshape(1, W), r_k.reshape(1, W), gn_g.reshape(1, W), gn_b.reshape(1, W), g_up)


PEER_ROUTE_TOKENS = 256
PEER_TOKENS = 512
PEER_E1_PER_STEP = 8
PEER_E1_PER_PART = 2
PEER_CAND_ROWS = 56


def _top_rows(s_ref, n_rows, k, emit):
    for a in range(k):
        s = s_ref[0:n_rows, :]
        mx = jnp.max(s, axis=0, keepdims=True)
        emit(a, mx)
        if a + 1 < k:
            s_ref[0:n_rows, :] = jnp.where(s == mx, NEG_INF, s)


def _peer_route_kernel(ht_ref, wqt_ref, k1_ref, k2_ref, s1_ref, s2_ref, g1_ref, e2_ref, th_ref,
                       work_ref, t1_ref, t2_ref, cand_ref):
    half = PEER_QDIM // 2
    qt = jnp.dot(wqt_ref[...], ht_ref[...], preferred_element_type=jnp.float32)
    for h in range(PEER_HEADS):
        for which, (k_ref, s_out, t_ref) in enumerate(((k1_ref, s1_ref, t1_ref), (k2_ref, s2_ref, t2_ref))):
            q = qt[h * PEER_QDIM + which * half: h * PEER_QDIM + (which + 1) * half, :]
            q_hi, q_lo = _split2(q)
            k_hi, k_lo = _split2(k_ref[h])
            f = lambda x, y: jnp.dot(x, y, preferred_element_type=jnp.float32)
            s = f(k_hi, q_hi) + (f(k_hi, q_lo) + f(k_lo, q_hi))
            s_out[h] = s
            work_ref[...] = s

            def emit(a, mx, t_ref=t_ref):
                t_ref[a:a + 1, :] = mx
            _top_rows(work_ref, N_KEYS, PEER_TOPK, emit)
        t1 = t1_ref[...]
        t2 = t2_ref[...]
        m1 = t1[0:1, :]
        m2 = t2[0:1, :]
        row = 0
        for a in range(PEER_TOPK):
            nb = PEER_TOPK // (a + 1)
            cand_ref[row:row + nb, :] = t1[a:a + 1, :] + t2[0:nb, :]
            row += nb
        cand_ref[row:PEER_CAND_ROWS, :] = jnp.full((PEER_CAND_ROWS - row, t1.shape[1]), NEG_INF, jnp.float32)
        acc = {}

        def emit_c(a, mx):
            e = jnp.exp(mx - (m1 + m2))
            acc['z'] = e if a == 0 else acc['z'] + e
            acc['th'] = mx
        _top_rows(cand_ref, PEER_CAND_ROWS, PEER_TOPK, emit_c)
        th_ref[h:h + 1, :] = acc['th']
        g1_ref[h] = jnp.exp(s1_ref[h] - m1) / acc['z']
        e2_ref[h] = jnp.exp(s2_ref[h] - m2)


def peer_route(ht, wqt, keys):
    D, N = ht.shape
    tm = PEER_ROUTE_TOKENS
    assert N % tm == 0
    big = jax.ShapeDtypeStruct((PEER_HEADS, N_KEYS, N), jnp.float32)
    big_spec = pl.BlockSpec((PEER_HEADS, N_KEYS, tm), lambda i: (0, 0, i))
    key_spec = pl.BlockSpec((PEER_HEADS, N_KEYS, PEER_QDIM // 2), lambda i: (0, 0, 0))
    return pl.pallas_call(
        _peer_route_kernel,
        grid=(N // tm,),
        in_specs=[pl.BlockSpec((D, tm), lambda i: (0, i)),
                  pl.BlockSpec((PEER_HEADS * PEER_QDIM, D), lambda i: (0, 0)),
                  key_spec, key_spec],
        out_specs=[big_spec, big_spec, big_spec, big_spec, pl.BlockSpec((PEER_HEADS, tm), lambda i: (0, i))],
        out_shape=[big, big, big, big, jax.ShapeDtypeStruct((PEER_HEADS, N), jnp.float32)],
        scratch_shapes=[pltpu.VMEM((N_KEYS, tm), jnp.float32),
                        pltpu.VMEM((PEER_TOPK, tm), jnp.float32),
                        pltpu.VMEM((PEER_TOPK, tm), jnp.float32),
                        pltpu.VMEM((PEER_CAND_ROWS, tm), jnp.float32)],
        compiler_params=pltpu.CompilerParams(
            dimension_semantics=("arbitrary",), vmem_limit_bytes=VMEM_LIMIT_BYTES),
        name="peer_route",
    )(ht, wqt, keys[0], keys[1])


def _gelu_tanh(x):
    return 0.5 * x * (1.0 + jnp.tanh(math.sqrt(2.0 / math.pi) * (x + 0.044715 * (x * x * x))))


def _peer_expert_kernel(ht_ref, u_ref, vt_ref, s1_ref, s2_ref, g1_ref, e2_ref, th_ref, o_ref, wg_ref):
    j = pl.program_id(1)
    tm = ht_ref.shape[1]

    @pl.when(j == 0)
    def _():
        o_ref[...] = jnp.zeros_like(o_ref)

    e1_rows = pl.ds(pl.multiple_of(j * PEER_E1_PER_STEP, PEER_E1_PER_STEP), PEER_E1_PER_STEP)
    ht = ht_ref[...]
    n_parts = PEER_E1_PER_STEP // PEER_E1_PER_PART
    part_slice = lambda p: slice(p * PEER_E1_PER_PART * N_KEYS, (p + 1) * PEER_E1_PER_PART * N_KEYS)
    activation = lambda p: _gelu_tanh(jnp.dot(u_ref[part_slice(p), :], ht, preferred_element_type=jnp.float32))
    act_next = activation(0)
    for part in range(n_parts):
        act = act_next
        if part + 1 < n_parts:
            act_next = activation(part + 1)
        for tc in range(tm // LANE):
            cols = slice(tc * LANE, (tc + 1) * LANE)
            s1_rows = [s1_ref[h, e1_rows, cols] for h in range(PEER_HEADS)]
            g1_rows = [g1_ref[h, e1_rows, cols] for h in range(PEER_HEADS)]
            for ee in range(PEER_E1_PER_PART):
                e = part * PEER_E1_PER_PART + ee
                w = None
                for h in range(PEER_HEADS):
                    score = s2_ref[h, :, cols] + s1_rows[h][e:e + 1, :]
                    gate = e2_ref[h, :, cols] * g1_rows[h][e:e + 1, :]
                    term = jnp.where(score >= th_ref[h:h + 1, cols], gate, 0.0)
                    w = term if w is None else w + term
                rows = slice(ee * N_KEYS, (ee + 1) * N_KEYS)
                wg_ref[part, rows, cols] = (w * act[rows, cols]).astype(jnp.bfloat16)
        if part >= 1:
            o_ref[...] += jnp.dot(vt_ref[:, part_slice(part - 1)], wg_ref[part - 1],
                                  preferred_element_type=jnp.float32)
    o_ref[...] += jnp.dot(vt_ref[:, part_slice(n_parts - 1)], wg_ref[n_parts - 1],
                          preferred_element_type=jnp.float32)


def peer_experts(ht, u, vt, s1, s2, g1, e2, th):
    D, N = ht.shape
    E = u.shape[0]
    tm = PEER_TOKENS
    te = PEER_E1_PER_STEP * N_KEYS
    assert N % tm == 0 and E % te == 0
    big_spec = pl.BlockSpec((PEER_HEADS, N_KEYS, tm), lambda i, j: (0, 0, i))
    return pl.pallas_call(
        _peer_expert_kernel,
        grid=(N // tm, E // te),
        in_specs=[pl.BlockSpec((D, tm), lambda i, j: (0, i)),
                  pl.BlockSpec((te, D), lambda i, j: (j, 0)),
                  pl.BlockSpec((D, te), lambda i, j: (0, j)),
                  big_spec, big_spec, big_spec, big_spec,
                  pl.BlockSpec((PEER_HEADS, tm), lambda i, j: (0, i))],
        out_specs=pl.BlockSpec((D, tm), lambda i, j: (0, i)),
        out_shape=jax.ShapeDtypeStruct((D, N), jnp.float32),
        scratch_shapes=[pltpu.VMEM((PEER_E1_PER_STEP // PEER_E1_PER_PART, PEER_E1_PER_PART * N_KEYS, tm), jnp.bfloat16)],
        compiler_params=pltpu.CompilerParams(
            dimension_semantics=("arbitrary", "arbitrary"), vmem_limit_bytes=VMEM_LIMIT_BYTES),
        name="peer_experts",
    )(ht, u, vt, s1, s2, g1, e2, th)


def peer_ffn_tokens(ht, wqt, keys, u, vt):
    s1, s2, g1, e2, th = peer_route(ht, wqt, keys)
    return peer_experts(ht, u, vt, s1, s2, g1, e2, th)


ZB_Q = Z_B // LANE
ZB_K = ZB_Q + B_HEADS
ZB_V = ZB_K + B_KV_HEADS
ZC_Q = ZB_V + B_KV_HEADS
ZC_K = ZC_Q + C_HEADS
ZC_V = ZC_K + C_HEADS
DIFF_Q_ROWS = 256

_DOT_NT = (((1,), (1,)), ((), ()))


def rope_lane_tables(n_tokens, head_dim):
    rows = n_tokens // GRID_W
    row_pos = jnp.repeat(jnp.arange(rows, dtype=jnp.float32), GRID_W)
    col_pos = jnp.tile(jnp.arange(GRID_W, dtype=jnp.float32), rows)
    n_freq = head_dim // 4
    freqs = ROPE_BASE ** (-jnp.arange(n_freq, dtype=jnp.float32) / n_freq)
    ang = jnp.concatenate([row_pos[:, None] * freqs, col_pos[:, None] * freqs], axis=-1)
    cos = jnp.repeat(jnp.cos(ang), 2, axis=-1)
    sin = jnp.repeat(jnp.sin(ang), 2, axis=-1) * jnp.tile(jnp.array([-1.0, 1.0], jnp.float32), head_dim // 2)
    reps = LANE // head_dim
    return jnp.tile(cos, (1, reps)), jnp.tile(sin, (1, reps))


def _rope(x, cos, sin_signed):
    lane = lax.broadcasted_iota(jnp.int32, x.shape, 1)
    partner = jnp.where(lane % 2 == 0, pltpu.roll(x, LANE - 1, axis=1), pltpu.roll(x, 1, axis=1))
    return x * cos + partner * sin_signed


def _bf(x):
    return x.astype(jnp.bfloat16)


def _diff_attn_kernel(lam_ref, q_ref, k_ref, v_ref, *rest, latent, lam_init):
    if latent:
        kc_ref, vc_ref, cq_ref, sq_ref, ck_ref, sk_ref, g_ref, o_ref, kr_ref = rest
    else:
        g_ref, o_ref, kr_ref = rest
    i = pl.program_id(2)

    @pl.when(i == 0)
    def _():
        k = k_ref[...]
        kr_ref[...] = _bf(_rope(k, ck_ref[...], sk_ref[...]) if latent else k)

    q = q_ref[...]
    if latent:
        q = _rope(q, cq_ref[...], sq_ref[...])
    lane = lax.broadcasted_iota(jnp.int32, q.shape, 1)
    scale = C_HEAD_DIM ** -0.5
    lam = lam_ref[0]
    kr = kr_ref[...]
    vb = _bf(v_ref[...])
    if latent:
        kcb, vcb = _bf(kc_ref[...]), _bf(vc_ref[...])

    def softmax_parts(qh):
        s = lax.dot_general(qh, kr, _DOT_NT, preferred_element_type=jnp.float32) * scale
        m = jnp.max(s, axis=-1, keepdims=True)
        if latent:
            sc = lax.dot_general(qh, kcb, _DOT_NT, preferred_element_type=jnp.float32) * scale
            m = jnp.maximum(m, jnp.max(sc, axis=-1, keepdims=True))
            ec = jnp.exp(sc - m)
        e = jnp.exp(s - m)
        z = jnp.sum(e, axis=-1, keepdims=True)
        if latent:
            z = z + jnp.sum(ec, axis=-1, keepdims=True)
            return e, ec, 1.0 / z
        return e, None, 1.0 / z

    e1, ec1, r1 = softmax_parts(_bf(jnp.where(lane < C_HEAD_DIM, q, 0.0)))
    e2, ec2, r2 = softmax_parts(_bf(jnp.where(lane >= C_HEAD_DIM, q, 0.0)))
    r2 = lam * r2
    o = jnp.dot(_bf(e1 * r1 - e2 * r2), vb, preferred_element_type=jnp.float32)
    if latent:
        o = o + jnp.dot(_bf(ec1 * r1 - ec2 * r2), vcb, preferred_element_type=jnp.float32)
    o = o * lax.rsqrt(jnp.mean(o * o, axis=-1, keepdims=True) + 1e-6) * g_ref[...] * (1.0 - lam_init)
    o_ref[...] = o


def diff_attention_mixer(z, lam, lam_init, subln_g, cache=None, layer=None, rope=None):
    B, T, _ = z.shape
    latent = cache is not None
    tq = min(DIFF_Q_ROWS, T)
    head = lambda base: pl.BlockSpec((None, tq, LANE), lambda b, h, i: (b, i, base + h))
    whole = lambda base: pl.BlockSpec((None, T, LANE), lambda b, h, i: (b, 0, base + h))
    in_specs = [pl.BlockSpec(memory_space=pltpu.SMEM), head(ZC_Q), whole(ZC_K), whole(ZC_V)]
    args = [lam.reshape(1), z, z, z]
    if latent:
        P = cache[0].shape[3]
        cspec = pl.BlockSpec((None, None, None, P, LANE), lambda b, h, i: (b, layer, h, 0, 0))
        tq_spec = pl.BlockSpec((tq, LANE), lambda b, h, i: (i, 0))
        tk_spec = pl.BlockSpec((T, LANE), lambda b, h, i: (0, 0))
        in_specs += [cspec, cspec, tq_spec, tq_spec, tk_spec, tk_spec]
        args += [cache[0], cache[1], rope[0], rope[1], rope[0], rope[1]]
    in_specs.append(pl.BlockSpec((1, LANE), lambda b, h, i: (0, 0)))
    args.append(subln_g.reshape(1, LANE))
    return pl.pallas_call(
        partial(_diff_attn_kernel, latent=latent, lam_init=lam_init),
        grid=(B, C_HEADS, T // tq),
        in_specs=in_specs,
        out_specs=pl.BlockSpec((None, tq, LANE), lambda b, h, i: (b, i, h)),
        out_shape=jax.ShapeDtypeStruct((B, T, C_HEADS * LANE), jnp.float32),
        scratch_shapes=[pltpu.VMEM((T, LANE), jnp.bfloat16)],
        compiler_params=pltpu.CompilerParams(
            dimension_semantics=("arbitrary", "arbitrary", "arbitrary"), vmem_limit_bytes=VMEM_LIMIT_BYTES),
        name="diff_attention",
    )(*args)


def _gqa_kernel(sink_ref, q0_ref, q1_ref, *rest, latent, n_tokens):
    if latent:
        (kp_ref, kn_ref, kx_ref, vp_ref, vn_ref, vx_ref, kc_ref, vc_ref,
         cq_ref, sq_ref, cp_ref, sp_ref, cx_ref, sx_ref, o_ref) = rest
    else:
        kc_ref, vc_ref, o_ref = rest
    kvh = pl.program_id(1)
    n = pl.program_id(2)
    scale = B_HEAD_DIM ** -0.5
    q0, q1 = q0_ref[...], q1_ref[...]
    if latent:
        q0 = _rope(q0, cq_ref[...], sq_ref[...])
        q1 = _rope(q1, cq_ref[...], sq_ref[...])
    q = _bf(jnp.concatenate([q0, q1], axis=0))
    row = lax.broadcasted_iota(jnp.int32, (2 * BLOCK, 1), 0)
    sink = jnp.where(row < BLOCK, sink_ref[2 * kvh], sink_ref[2 * kvh + 1])

    s_c = lax.dot_general(q, _bf(kc_ref[...]), _DOT_NT, preferred_element_type=jnp.float32) * scale
    m = jnp.maximum(jnp.max(s_c, axis=-1, keepdims=True), sink)
    if latent:
        kw = jnp.concatenate([_rope(kp_ref[...], cp_ref[...], sp_ref[...]),
                              _rope(kn_ref[...], cq_ref[...], sq_ref[...]),
                              _rope(kx_ref[...], cx_ref[...], sx_ref[...])], axis=0)
        vw = jnp.concatenate([vp_ref[...], vn_ref[...], vx_ref[...]], axis=0)
        s_w = lax.dot_general(q, _bf(kw), _DOT_NT, preferred_element_type=jnp.float32) * scale
        qpos = n * BLOCK + lax.broadcasted_iota(jnp.int32, s_w.shape, 0) % BLOCK
        kpos = (n - 1) * BLOCK + lax.broadcasted_iota(jnp.int32, s_w.shape, 1)
        valid = (jnp.abs(qpos - kpos) <= WINDOW) & (kpos >= 0) & (kpos < n_tokens)
        s_w = jnp.where(valid, s_w, NEG_INF)
        m = jnp.maximum(m, jnp.max(s_w, axis=-1, keepdims=True))
        e_w = jnp.exp(s_w - m)
    e_c = jnp.exp(s_c - m)
    zsum = jnp.sum(e_c, axis=-1, keepdims=True) + jnp.exp(sink - m)
    o = jnp.dot(_bf(e_c), _bf(vc_ref[...]), preferred_element_type=jnp.float32)
    if latent:
        zsum = zsum + jnp.sum(e_w, axis=-1, keepdims=True)
        o = o + jnp.dot(_bf(e_w), _bf(vw), preferred_element_type=jnp.float32)
    o = o / zsum
    o_ref[:, 0:LANE] = o[0:BLOCK]
    o_ref[:, LANE:2 * LANE] = o[BLOCK:2 * BLOCK]


def gqa_mixer(z, sink, cache=None, layer=None, rope=None):
    B, T, _ = z.shape
    latent = cache is not None
    nb = T // BLOCK
    blk = lambda base, off: pl.BlockSpec(
        (None, BLOCK, LANE), lambda b, kvh, n: (b, jnp.clip(n + off, 0, nb - 1), base + kvh))
    qspec = lambda g: pl.BlockSpec((None, BLOCK, LANE), lambda b, kvh, n: (b, n, ZB_Q + 2 * kvh + g))
    in_specs = [pl.BlockSpec(memory_space=pltpu.SMEM), qspec(0), qspec(1)]
    args = [sink, z, z]
    if latent:
        P = cache[0].shape[3]
        cspec = pl.BlockSpec((None, None, None, P, LANE), lambda b, kvh, n: (b, layer, kvh, 0, 0))
        tab = lambda off: pl.BlockSpec((BLOCK, LANE), lambda b, kvh, n: (jnp.clip(n + off, 0, nb - 1), 0))
        in_specs += [blk(ZB_K, -1), blk(ZB_K, 0), blk(ZB_K, 1), blk(ZB_V, -1), blk(ZB_V, 0), blk(ZB_V, 1),
                     cspec, cspec, tab(0), tab(0), tab(-1), tab(-1), tab(1), tab(1)]
        args += [z] * 6 + [cache[0], cache[1]] + [rope[0], rope[1]] * 3
    else:
        whole = lambda base: pl.BlockSpec((None, T, LANE), lambda b, kvh, n: (b, 0, base + kvh))
        in_specs += [whole(ZB_K), whole(ZB_V)]
        args += [z, z]
    return pl.pallas_call(
        partial(_gqa_kernel, latent=latent, n_tokens=T),
        grid=(B, B_KV_HEADS, nb),
        in_specs=in_specs,
        out_specs=pl.BlockSpec((None, BLOCK, 2 * LANE), lambda b, kvh, n: (b, n, kvh)),
        out_shape=jax.ShapeDtypeStruct((B, T, B_HEADS * LANE), jnp.float32),
        compiler_params=pltpu.CompilerParams(
            dimension_semantics=("arbitrary", "arbitrary", "arbitrary"), vmem_limit_bytes=VMEM_LIMIT_BYTES),
        name="gqa_attention",
    )(*args)


def split_cols(z, sizes):
    return jnp.split(z, [int(s) for s in np.cumsum(sizes)[:-1]], axis=-1)


def layer_norm(x, g=None, b=None):
    mu = jnp.mean(x, axis=-1, keepdims=True)
    var = jnp.mean(jnp.square(x - mu), axis=-1, keepdims=True)
    y = (x - mu) * lax.rsqrt(var + LN_EPS)
    if g is not None:
        y = y * g + b
    return y


def rwkv7_mixer(za, S0_f, S0_b, lp):
    s0t = jnp.swapaxes(jnp.stack([S0_f, S0_b], axis=0), -1, -2)
    ys, sfin_t = rwkv7_chunked(za, lp['a_w_up'], lp['a_a_up'], lp['a_w0'], lp['a_a0'], lp['a_k_k'], lp['a_k_a'], s0t)
    states = jnp.swapaxes(sfin_t, -1, -2)
    out = rwkv_post(ys, za, lp['a_a_up'], lp['a_a0'], lp['a_k_a'], lp['a_r_k'], lp['a_gn_g'], lp['a_gn_b'],
                    lp['a_g_up'])
    return out, states[0], states[1]


def diff_lambda(lv, layer):
    lam_init = 0.8 - 0.6 * math.exp(-0.3 * layer)
    lam = jnp.exp(jnp.sum(lv[0] * lv[1])) - jnp.exp(jnp.sum(lv[2] * lv[3])) + lam_init
    return lam, lam_init


def adaln_params(cond, lp):
    mod = mm(jax.nn.silu(cond), lp['w_mod']) + lp['b_mod']
    return jnp.split(mod[:, None, :], 6, axis=-1)


def mixer_inputs(x, shift, scale, lp):
    z = ln_mod_matmul(x, scale, shift, lp['w_in'])
    return z, centred_conv3(z, lp['a_conv'])


def merge_mixers(x, yA, yB, yC, z, gate1, shift2, scale2, lp):
    n = x.shape[0] * x.shape[1]
    flat = lambda t: t.reshape(n, t.shape[-1])
    merged = merge_gated(flat(yA), flat(yB), flat(yC), flat(z), lp['p_a'], lp['p_b'], lp['p_c'])
    return out_proj_norm(merged, x, gate1, scale2, shift2, lp['ln1_g'], lp['ln1_b'], lp['w_out'])


def peer_and_norm(part, lp):
    x, ht, gate2 = part
    yt = peer_ffn_tokens(ht, lp['peer_wqt'], lp['peer_keys'], lp['peer_u'], lp['peer_vt'])
    return peer_residual_norm(yt, x, gate2, lp['ln2_g'], lp['ln2_b'])


def context_layer(x, cond, lp, layer):
    Bsz, L, _ = x.shape
    shift1, scale1, gate1, shift2, scale2, gate2 = adaln_params(cond, lp)
    z, zA = mixer_inputs(x, shift1, scale1, lp)
    S0 = jnp.zeros((Bsz, A_HEADS, A_HEAD_DIM, A_HEAD_DIM), jnp.float32)
    yA, S_f, S_b = rwkv7_mixer(zA, S0, S0, lp)
    yB = gqa_mixer(z, lp['b_sink'])
    lam, lam_init = diff_lambda(lp['c_lam'], layer)
    yC = diff_attention_mixer(z, lam, lam_init, lp['c_subln'])
    x_mid, h_mid = merge_mixers(x, yA, yB, yC, z, gate1, shift2, scale2, lp)

    def cache_layout(first_block, n_heads):
        t = z[..., first_block * LANE:(first_block + n_heads) * LANE].reshape(Bsz, L, n_heads, LANE)
        return jnp.transpose(t, (0, 2, 1, 3))
    ctx = (S_f, S_b, cache_layout(ZB_K, B_KV_HEADS), cache_layout(ZB_V, B_KV_HEADS),
           cache_layout(ZC_K, C_HEADS), cache_layout(ZC_V, C_HEADS))
    return (x_mid, h_mid, gate2), ctx


def latent_layer(x, cond, lp, layer, S_f0, S_b0, cache_b, cache_c, rope_b, rope_c):
    shift1, scale1, gate1, shift2, scale2, gate2 = adaln_params(cond, lp)
    z, zA = mixer_inputs(x, shift1, scale1, lp)
    yA, _, _ = rwkv7_mixer(zA, S_f0, S_b0, lp)
    yB = gqa_mixer(z, lp['b_sink'], cache=cache_b, layer=layer, rope=rope_b)
    lam, lam_init = diff_lambda(lp['c_lam'], layer)
    yC = diff_attention_mixer(z, lam, lam_init, lp['c_subln'], cache=cache_c, layer=layer, rope=rope_c)
    x_mid, h_mid = merge_mixers(x, yA, yB, yC, z, gate1, shift2, scale2, lp)
    return x_mid, h_mid, gate2


_BF16_WEIGHTS = ('w_mod', 'w_in', 'a_w_up', 'a_a_up', 'a_g_up', 'p_a', 'p_b', 'p_c', 'w_out', 'peer_u')


def kernel(x_prompt, x_sample, state_a_fwd, state_a_bwd, cache_b_k, cache_b_v, cache_c_k, cache_c_v, c, c_ctx, w_mod, b_mod, w_in, a_conv, a_w0, a_w_up, a_a0, a_a_up, a_g_up, a_k_k, a_k_a, a_r_k, a_gn_g, a_gn_b, b_sink, c_lam, c_subln, p_a, p_b, p_c, w_out, ln1_g, ln1_b, ln2_g, ln2_b, peer_wq, peer_keys, peer_u, peer_v):
    params = dict(w_mod=w_mod, b_mod=b_mod, w_in=w_in, a_conv=a_conv, a_w0=a_w0, a_w_up=a_w_up, a_a0=a_a0,
                  a_a_up=a_a_up, a_g_up=a_g_up, a_k_k=a_k_k, a_k_a=a_k_a, a_r_k=a_r_k, a_gn_g=a_gn_g,
                  a_gn_b=a_gn_b, b_sink=b_sink, c_lam=c_lam, c_subln=c_subln, p_a=p_a, p_b=p_b, p_c=p_c,
                  w_out=w_out, ln1_g=ln1_g, ln1_b=ln1_b, ln2_g=ln2_g, ln2_b=ln2_b, peer_wq=peer_wq,
                  peer_keys=peer_keys, peer_u=peer_u, peer_v=peer_v)
    for name in _BF16_WEIGHTS:
        params[name] = params[name].astype(jnp.bfloat16)
    w_in_bf = params['w_in']
    params['w_in'] = jnp.concatenate([w_in_bf[..., IN_COLS - 3 * D_MODEL:], w_in_bf[..., :IN_COLS - 3 * D_MODEL]], axis=-1)
    params['peer_wqt'] = jnp.swapaxes(params.pop('peer_wq'), 1, 2).astype(jnp.bfloat16)
    params['peer_vt'] = jnp.swapaxes(params.pop('peer_v'), 1, 2).astype(jnp.bfloat16)
    n_lat = x_sample.shape[1]
    rope_b = rope_lane_tables(n_lat, B_HEAD_DIM)
    rope_c = rope_lane_tables(n_lat, C_HEAD_DIM)
    cond_ctx = jnp.broadcast_to(c_ctx[None, :], (x_prompt.shape[0], D_MODEL))
    y_prompt, y_sample = x_prompt, x_sample
    new = [[], [], [], [], [], []]
    for layer in range(DEPTH):
        lp = {name: val[layer] for name, val in params.items()}
        part_ctx, ctx = context_layer(y_prompt, cond_ctx, lp, layer)
        for lst, t in zip(new, ctx):
            lst.append(t)
        part_lat = latent_layer(y_sample, c, lp, layer,
                                state_a_fwd[:, layer], state_a_bwd[:, layer],
                                (cache_b_k, cache_b_v), (cache_c_k, cache_c_v), rope_b, rope_c)
        y_prompt = peer_and_norm(part_ctx, lp)
        y_sample = peer_and_norm(part_lat, lp)
    return (y_prompt, y_sample) + tuple(jnp.stack(lst, axis=1) for lst in new)
```
